```python
import math
import jax, jax.numpy as jnp
from jax import lax
import numpy as np

D_MODEL = 2048
BATCH = 4
SEQ = 4096
DEPTH = 1

MIX_WIDTH = D_MODEL
ATTN_WIDTH = MIX_WIDTH // 2
CONV_WIDTH = MIX_WIDTH - ATTN_WIDTH
HEAD_DIM = 128
N_ATTN_HEADS = ATTN_WIDTH // HEAD_DIM
N_CONV_GROUPS = CONV_WIDTH // HEAD_DIM
CONV_K = 3
DILATED_PATTERNS = ((128, 1), (512, 4), (2048, 16))
BLOCK = 128
IN_COLS = 3 * ATTN_WIDTH + 3 * CONV_WIDTH
MEM_LEN = 256
N_MEM_HEADS = 4
MEM_HEAD_DIM = D_MODEL // N_MEM_HEADS
N_EXPERTS = 64
TOP_K = 8
N_GROUPS = 8
TOPK_GROUPS = 4
EXPERT_FF = 512
SHARED_FF = 512
ROUTED_SCALE = 2.5
EXPERT_BLOCK = 128
LN_EPS = 1e-5
RMS_EPS = 1e-6
ALPHA = (2.0 * DEPTH) ** 0.25
BETA = (8.0 * DEPTH) ** -0.25

kernel_name = "hybrid_dilated_conv_moe_deepnorm"


def layer_norm(x, g, b):
    xf = x.astype(jnp.float32)
    mu = jnp.mean(xf, -1, keepdims=True)
    var = jnp.mean(jnp.square(xf - mu), -1, keepdims=True)
    return ((xf - mu) * lax.rsqrt(var + LN_EPS) * g + b).astype(x.dtype)


def rms_norm(x, g):
    xf = x.astype(jnp.float32)
    return (xf * lax.rsqrt(jnp.mean(jnp.square(xf), -1, keepdims=True) + RMS_EPS) * g).astype(x.dtype)


def dilated_branch(q, k, v, window, dilation):
    Bsz, S, H, Dh = q.shape
    n_back = window // dilation
    L = S // dilation
    nb = -(-L // BLOCK)
    Lp = nb * BLOCK

    def sub(t):
        return t.reshape(Bsz, L, dilation, H, Dh)

    qs = jnp.pad(sub(q), ((0, 0), (0, Lp - L), (0, 0), (0, 0), (0, 0)))
    qs = qs.reshape(Bsz, nb, BLOCK, dilation, H, Dh)

    def band(t):
        tp = jnp.pad(sub(t), ((0, 0), (BLOCK, Lp - L), (0, 0), (0, 0), (0, 0)))
        tp = tp.reshape(Bsz, nb + 1, BLOCK, dilation, H, Dh)
        return jnp.concatenate([tp[:, :-1], tp[:, 1:]], axis=2)

    kb, vb = band(k), band(v)
    s = jnp.einsum('bnqrhe,bnkrhe->bnrhqk', qs, kb).astype(jnp.float32) * (Dh ** -0.5)
    qi = jnp.arange(nb)[:, None, None] * BLOCK + jnp.arange(BLOCK)[None, :, None]
    ki = jnp.arange(nb)[:, None, None] * BLOCK - BLOCK + jnp.arange(2 * BLOCK)[None, None, :]
    dist = qi - ki
    valid = (dist >= 0) & (dist <= n_back) & (ki >= 0)
    s = jnp.where(valid[None, :, None, None], s, -jnp.inf)
    m = jnp.max(s, -1, keepdims=True)
    p = jnp.exp(s - m)
    l = jnp.sum(p, -1, keepdims=True)
    o = jnp.einsum('bnrhqk,bnkrhe->bnqrhe', (p / l).astype(v.dtype), vb)
    lse = (m + jnp.log(l))[..., 0]
    o = o.reshape(Bsz, Lp, dilation, H, Dh)[:, :L].reshape(Bsz, S, H, Dh)
    lse = jnp.transpose(lse, (0, 1, 4, 2, 3)).reshape(Bsz, Lp, dilation, H)[:, :L].reshape(Bsz, S, H)
    return o, lse


def short_gated_conv(b_gate, c_gate, h, conv_w):
    S = h.shape[1]
    z = c_gate * h
    zp = jnp.pad(z, ((0, 0), (CONV_K - 1, 0), (0, 0)))
    y = conv_w[0] * zp[:, 0:S] + conv_w[1] * zp[:, 1:S + 1] + conv_w[2] * zp[:, 2:S + 2]
    return b_gate * y


def hybrid_mixer(x, w_in, conv_w, g_attn_out, g_conv_out, w_out):
    Bsz, S, _ = x.shape
    proj = x @ w_in
    A, C = ATTN_WIDTH, CONV_WIDTH
    q, k, v, b_gate, c_gate, h = jnp.split(proj, [A, 2 * A, 3 * A, 3 * A + C, 3 * A + 2 * C], axis=-1)
    q = q.reshape(Bsz, S, N_ATTN_HEADS, HEAD_DIM)
    k = k.reshape(Bsz, S, N_ATTN_HEADS, HEAD_DIM)
    v = v.reshape(Bsz, S, N_ATTN_HEADS, HEAD_DIM)
    outs, lses = [], []
    for window, dilation in DILATED_PATTERNS:
        o_i, lse_i = dilated_branch(q, k, v, window, dilation)
        outs.append(o_i)
        lses.append(lse_i)
    wts = jax.nn.softmax(jnp.stack(lses, 0), axis=0).astype(x.dtype)
    attn = jnp.einsum('gbsh,gbshe->bshe', wts, jnp.stack(outs, 0)).reshape(Bsz, S, A)
    conv = short_gated_conv(b_gate, c_gate, h, conv_w)
    mixed = jnp.concatenate([rms_norm(attn, g_attn_out), rms_norm(conv, g_conv_out)], axis=-1)
    return mixed @ w_out


def memory_cross_attention(x, mem, w_q_mem, w_kv_mem, w_o_mem):
    Bsz, S, D = x.shape
    M = mem.shape[1]
    q = (x @ w_q_mem).reshape(Bsz, S, N_MEM_HEADS, MEM_HEAD_DIM)
    k, v = jnp.split(mem @ w_kv_mem, 2, axis=-1)
    k = k.reshape(Bsz, M, N_MEM_HEADS, MEM_HEAD_DIM)
    v = v.reshape(Bsz, M, N_MEM_HEADS, MEM_HEAD_DIM)
    s = jnp.einsum('bshe,bmhe->bhsm', q, k).astype(jnp.float32) * (MEM_HEAD_DIM ** -0.5)
    p = jax.nn.softmax(s, axis=-1).astype(x.dtype)
    o = jnp.einsum('bhsm,bmhe->bshe', p, v).reshape(Bsz, S, D)
    return o @ w_o_mem


def route(xf, w_router, router_bias):
    N = xf.shape[0]
    scores = jax.nn.sigmoid((xf @ w_router).astype(jnp.float32))
    choice = scores + router_bias.astype(jnp.float32)
    grp = choice.reshape(N, N_GROUPS, N_EXPERTS // N_GROUPS)
    grp_score = jnp.sum(lax.top_k(grp, 2)[0], axis=-1)
    _, top_g = lax.top_k(grp_score, TOPK_GROUPS)
    gmask = jnp.any(top_g[..., None] == jnp.arange(N_GROUPS)[None, None, :], axis=1)
    emask = jnp.repeat(gmask, N_EXPERTS // N_GROUPS, axis=-1)
    _, idx = lax.top_k(jnp.where(emask, choice, -jnp.inf), TOP_K)
    w = jnp.take_along_axis(scores, idx, axis=-1)
    gates = w / jnp.sum(w, -1, keepdims=True) * ROUTED_SCALE
    return idx, gates


def routed_experts(xf, idx, gates, w_gate, w_up, w_down):
    N, D = xf.shape
    A = N * TOP_K
    e_flat = idx.reshape(A)
    tok_flat = jnp.repeat(jnp.arange(N, dtype=jnp.int32), TOP_K)
    g_flat = gates.reshape(A).astype(xf.dtype)
    order = jnp.argsort(e_flat)
    e_sorted = e_flat[order]
    counts = jnp.zeros((N_EXPERTS,), jnp.int32).at[e_flat].add(1)
    padded = (counts + EXPERT_BLOCK - 1) // EXPERT_BLOCK * EXPERT_BLOCK
    start = jnp.cumsum(counts) - counts
    pend = jnp.cumsum(padded)
    pstart = pend - padded
    dest = pstart[e_sorted] + jnp.arange(A, dtype=jnp.int32) - start[e_sorted]
    n_blocks = -(-A // EXPERT_BLOCK) + N_EXPERTS
    R = n_blocks * EXPERT_BLOCK
    row_tok = jnp.full((R,), N, jnp.int32).at[dest].set(tok_flat[order])
    row_gate = jnp.zeros((R,), xf.dtype).at[dest].set(g_flat[order])
    block_start = jnp.arange(n_blocks, dtype=jnp.int32) * EXPERT_BLOCK
    block_exp = jnp.minimum(jnp.searchsorted(pend, block_start, side='right'), N_EXPERTS - 1)
    x_pad = jnp.concatenate([xf, jnp.zeros((1, D), xf.dtype)], axis=0)

    def run_block(args):
        tok, gate, e = args
        xb = x_pad[tok]
        hb = jax.nn.silu(xb @ w_gate[e]) * (xb @ w_up[e])
        return (hb @ w_down[e]) * gate[:, None]

    yb = lax.map(run_block, (row_tok.reshape(n_blocks, EXPERT_BLOCK),
                             row_gate.reshape(n_blocks, EXPERT_BLOCK), block_exp))
    return jax.ops.segment_sum(yb.reshape(R, D), row_tok, num_segments=N + 1)[:N]


def moe_ffn(x, w_router, router_bias, w_gate, w_up, w_down, ws_gate, ws_up, ws_down):
    Bsz, S, D = x.shape
    xf = x.reshape(Bsz * S, D)
    idx, gates = route(xf, w_router, router_bias)
    routed = routed_experts(xf, idx, gates, w_gate, w_up, w_down)
    shared = (jax.nn.silu(xf @ ws_gate) * (xf @ ws_up)) @ ws_down
    return (routed + shared).reshape(Bsz, S, D)


def setup_inputs(seed: int = 0) -> dict:
    key = jax.random.key(seed)
    ks = jax.random.split(key, 24)
    f32 = jnp.float32
    nrm = lambda k, shape, scale: jax.random.normal(k, shape, f32) * scale
    L = DEPTH
    return {
        "x": nrm(ks[0], (BATCH, SEQ, D_MODEL), 1.0),
        "mem": nrm(ks[1], (BATCH, MEM_LEN, D_MODEL), 1.0),
        "w_in": nrm(ks[2], (L, D_MODEL, IN_COLS), D_MODEL ** -0.5),
        "conv_w": nrm(ks[3], (L, CONV_K, CONV_WIDTH), CONV_K ** -0.5),
        "g_attn_out": 1.0 + nrm(ks[4], (L, ATTN_WIDTH), 0.02),
        "g_conv_out": 1.0 + nrm(ks[5], (L, CONV_WIDTH), 0.02),
        "w_out": nrm(ks[6], (L, MIX_WIDTH, D_MODEL), MIX_WIDTH ** -0.5 * BETA),
        "ln1_g": 1.0 + nrm(ks[7], (L, D_MODEL), 0.02),
        "ln1_b": nrm(ks[8], (L, D_MODEL), 0.02),
        "w_q_mem": nrm(ks[9], (L, D_MODEL, D_MODEL), D_MODEL ** -0.5),
        "w_kv_mem": nrm(ks[10], (L, D_MODEL, 2 * D_MODEL), D_MODEL ** -0.5),
        "w_o_mem": nrm(ks[11], (L, D_MODEL, D_MODEL), D_MODEL ** -0.5 * BETA),
        "ln2_g": 1.0 + nrm(ks[12], (L, D_MODEL), 0.02),
        "ln2_b": nrm(ks[13], (L, D_MODEL), 0.02),
        "w_router": nrm(ks[14], (L, D_MODEL, N_EXPERTS), D_MODEL ** -0.5),
        "router_bias": nrm(ks[15], (L, N_EXPERTS), 0.01),
        "w_gate": nrm(ks[16], (L, N_EXPERTS, D_MODEL, EXPERT_FF), D_MODEL ** -0.5),
        "w_up": nrm(ks[17], (L, N_EXPERTS, D_MODEL, EXPERT_FF), D_MODEL ** -0.5),
        "w_down": nrm(ks[18], (L, N_EXPERTS, EXPERT_FF, D_MODEL), EXPERT_FF ** -0.5 * BETA),
        "ws_gate": nrm(ks[19], (L, D_MODEL, SHARED_FF), D_MODEL ** -0.5),
        "ws_up": nrm(ks[20], (L, D_MODEL, SHARED_FF), D_MODEL ** -0.5),
        "ws_down": nrm(ks[21], (L, SHARED_FF, D_MODEL), SHARED_FF ** -0.5 * BETA),
        "ln3_g": 1.0 + nrm(ks[22], (L, D_MODEL), 0.02),
        "ln3_b": nrm(ks[23], (L, D_MODEL), 0.02),
    }


def reference(x, mem, w_in, conv_w, g_attn_out, g_conv_out, w_out, ln1_g, ln1_b,
              w_q_mem, w_kv_mem, w_o_mem, ln2_g, ln2_b, w_router, router_bias,
              w_gate, w_up, w_down, ws_gate, ws_up, ws_down, ln3_g, ln3_b):
    for l in range(DEPTH):
        y = hybrid_mixer(x, w_in[l], conv_w[l], g_attn_out[l], g_conv_out[l], w_out[l])
        x = layer_norm(ALPHA * x + y, ln1_g[l], ln1_b[l])
        y = memory_cross_attention(x, mem, w_q_mem[l], w_kv_mem[l], w_o_mem[l])
        x = layer_norm(ALPHA * x + y, ln2_g[l], ln2_b[l])
        y = moe_ffn(x, w_router[l], router_bias[l], w_gate[l], w_up[l], w_down[l],
                    ws_gate[l], ws_up[l], ws_down[l])
        x = layer_norm(ALPHA * x + y, ln3_g[l], ln3_b[l])
    return x
```

```python
import functools

import jax
import jax.numpy as jnp
from jax import lax
from jax.experimental import pallas as pl
from jax.experimental.pallas import tpu as pltpu

F32 = jnp.float32
BF16 = jnp.bfloat16
I32 = jnp.int32

HEAD_DIM = 128
N_ATTN_HEADS = 8
ATTN_WIDTH = N_ATTN_HEADS * HEAD_DIM
CONV_WIDTH = 1024
PROJ_PARTS = 6
DILATED_PATTERNS = ((128, 1), (512, 4), (2048, 16))
ATTN_BLOCK = 128
N_MEM_HEADS = 4
N_EXPERTS = 64
TOP_K = 8
N_GROUPS = 8
GROUP_SIZE = N_EXPERTS // N_GROUPS
TOPK_GROUPS = 4
ROUTED_SCALE = 2.5
LN_EPS = 1e-5
RMS_EPS = 1e-6
DEPTH = 1
ALPHA = (2.0 * DEPTH) ** 0.25

LANES = 128
SUBLANES = 8
ROW_CHUNKS = 16
VMEM_LIMIT = 56 * 1024 * 1024
ATTN_OUT_COLS = ATTN_WIDTH + LANES
EXPERT_ROWS = 256
TM_PROJ, TN_PROJ = 1024, 512
TM_MIX = 256
TM_XATTN = 256
TM_ROUTER = 512
TM_DISPATCH = 256
TM_COMBINE = 128


def _params(*semantics):
    return pltpu.CompilerParams(dimension_semantics=semantics, vmem_limit_bytes=VMEM_LIMIT)


def _layer_norm(r, g, b):
    mu = jnp.mean(r, axis=-1, keepdims=True)
    c = r - mu
    var = jnp.mean(c * c, axis=-1, keepdims=True)
    return c * lax.rsqrt(var + LN_EPS) * g + b


def _rms_norm(v, g):
    return v * lax.rsqrt(jnp.mean(v * v, axis=-1, keepdims=True) + RMS_EPS) * g


def _dot(a, b):
    return jnp.dot(a, b, preferred_element_type=F32)


def _dot_nt(a, b):
    return lax.dot_general(a, b, (((1,), (1,)), ((), ())), preferred_element_type=F32)


def _silu(v):
    return v / (1.0 + jnp.exp(-v))


def _matmul_kernel(a_ref, w_ref, o_ref):
    o_ref[...] = _dot(a_ref[...], w_ref[...]).astype(o_ref.dtype)


def _matmul(a, w, out_dtype, tm, tn):
    m, k = a.shape
    n = w.shape[1]
    return pl.pallas_call(
        _matmul_kernel,
        grid=(m // tm, n // tn),
        in_specs=[pl.BlockSpec((tm, k), lambda i, j: (i, 0)),
                  pl.BlockSpec((k, tn), lambda i, j: (0, j))],
        out_specs=pl.BlockSpec((tm, tn), lambda i, j: (i, j)),
        out_shape=jax.ShapeDtypeStruct((m, n), out_dtype),
        compiler_params=_params("parallel", "arbitrary"),
        name="matmul",
    )(a, w)


def _attn_kernel(q_ref, kp_ref, kc_ref, vp_ref, vc_ref, o_ref):
    n = pl.program_id(2)
    blk = ATTN_BLOCK
    row = lax.broadcasted_iota(I32, (blk, 2 * blk), 0)
    col = lax.broadcasted_iota(I32, (blk, 2 * blk), 1)
    dist = row + blk - col
    valid = (dist >= 0) & (dist <= blk) & ((col >= blk) | (n > 0))
    lane = lax.broadcasted_iota(I32, (blk, LANES), 1)
    lse_tile = jnp.zeros((blk, LANES), F32)
    scale = HEAD_DIM ** -0.5
    for h in range(N_ATTN_HEADS):
        sl = slice(h * HEAD_DIM, (h + 1) * HEAD_DIM)
        q = q_ref[:, sl].astype(BF16)
        k = jnp.concatenate([kp_ref[:, sl], kc_ref[:, sl]], axis=0).astype(BF16)
        v = jnp.concatenate([vp_ref[:, sl], vc_ref[:, sl]], axis=0).astype(BF16)
        s = _dot_nt(q, k) * scale
        s = jnp.where(valid, s, -jnp.inf)
        m = jnp.max(s, axis=-1, keepdims=True)
        p = jnp.exp(s - m)
        l = jnp.sum(p, axis=-1, keepdims=True)
        o_ref[:, sl] = _dot((p / l).astype(BF16), v)
        lse_tile = jnp.where(lane == h, m + jnp.log(l), lse_tile)
    o_ref[:, ATTN_WIDTH:] = lse_tile


def _attn_branch(proj, batch, seq, dilation):
    n_tok, cols = proj.shape
    sub_len = seq // dilation
    nb = sub_len // ATTN_BLOCK
    view = proj.reshape(n_tok // dilation, dilation * cols)
    w = ATTN_WIDTH

    def spec(part, prev):
        def index(b, r, n):
            blk = jnp.maximum(n - 1, 0) if prev else n
            return (b * nb + blk, r * PROJ_PARTS + part)
        return pl.BlockSpec((ATTN_BLOCK, w), index)

    out = pl.pallas_call(
        _attn_kernel,
        grid=(batch, dilation, nb),
        in_specs=[spec(0, False), spec(1, True), spec(1, False), spec(2, True), spec(2, False)],
        out_specs=pl.BlockSpec((ATTN_BLOCK, ATTN_OUT_COLS), lambda b, r, n: (b * nb + n, r)),
        out_shape=jax.ShapeDtypeStruct((n_tok // dilation, dilation * ATTN_OUT_COLS), F32),
        compiler_params=_params("parallel", "parallel", "arbitrary"),
        name=f"dilated_attn_d{dilation}",
    )(view, view, view, view, view)
    return out.reshape(n_tok, ATTN_OUT_COLS)


def _mix_out_kernel(a1_ref, a2_ref, a3_ref, b_ref, c_ref, h_ref, cp_ref, hp_ref, x_ref,
                    w_ref, cw_ref, ga_ref, gc_ref, lg_ref, lb_ref, o_ref, *, tiles_per_seq):
    i = pl.program_id(0)
    tm = x_ref.shape[0]
    branches = (a1_ref, a2_ref, a3_ref)
    lses = [a[:, ATTN_WIDTH:] for a in branches]
    top = jnp.maximum(jnp.maximum(lses[0], lses[1]), lses[2])
    es = [jnp.exp(v - top) for v in lses]
    den = es[0] + es[1] + es[2]
    wts = [e / den for e in es]
    heads = []
    for h in range(N_ATTN_HEADS):
        sl = slice(h * HEAD_DIM, (h + 1) * HEAD_DIM)
        acc = wts[0][:, h:h + 1] * a1_ref[:, sl]
        acc += wts[1][:, h:h + 1] * a2_ref[:, sl]
        acc += wts[2][:, h:h + 1] * a3_ref[:, sl]
        heads.append(acc)
    attn = jnp.concatenate(heads, axis=1)

    z = c_ref[...] * h_ref[...]
    zp = jnp.where(i % tiles_per_seq == 0, 0.0, cp_ref[...] * hp_ref[...])
    zz = jnp.concatenate([zp, z], axis=0)
    z1 = pltpu.roll(zz, 1, axis=0)[SUBLANES:]
    z2 = pltpu.roll(zz, 2, axis=0)[SUBLANES:]
    conv = b_ref[...] * (cw_ref[0:1, :] * z2 + cw_ref[1:2, :] * z1 + cw_ref[2:3, :] * z)

    mixed = jnp.concatenate([_rms_norm(attn, ga_ref[...]), _rms_norm(conv, gc_ref[...])], axis=1)
    y = _dot(mixed.astype(BF16), w_ref[...])
    o_ref[...] = _layer_norm(ALPHA * x_ref[...] + y, lg_ref[...], lb_ref[...])


def _mix_out(a1, a2, a3, proj, x, w_out, conv_w, g_attn, g_conv, ln_g, ln_b, seq):
    n_tok, d = x.shape
    tm = TM_MIX
    rows8 = tm // SUBLANES
    row = lambda i: (i, 0)
    const = lambda i: (0, 0)
    prev = lambda part: (lambda i: (jnp.maximum(i * rows8 - 1, 0), part))
    return pl.pallas_call(
        functools.partial(_mix_out_kernel, tiles_per_seq=seq // tm),
        grid=(n_tok // tm,),
        in_specs=[pl.BlockSpec((tm, ATTN_OUT_COLS), row)] * 3 + [
            pl.BlockSpec((tm, CONV_WIDTH), lambda i: (i, 3)),
            pl.BlockSpec((tm, CONV_WIDTH), lambda i: (i, 4)),
            pl.BlockSpec((tm, CONV_WIDTH), lambda i: (i, 5)),
            pl.BlockSpec((SUBLANES, CONV_WIDTH), prev(4)),
            pl.BlockSpec((SUBLANES, CONV_WIDTH), prev(5)),
            pl.BlockSpec((tm, d), row),
            pl.BlockSpec(w_out.shape, const),
            pl.BlockSpec(conv_w.shape, const),
            pl.BlockSpec(g_attn.shape, const),
            pl.BlockSpec(g_conv.shape, const),
            pl.BlockSpec(ln_g.shape, const),
            pl.BlockSpec(ln_b.shape, const)],
        out_specs=pl.BlockSpec((tm, d), row),
        out_shape=jax.ShapeDtypeStruct((n_tok, d), F32),
        compiler_params=_params("parallel"),
        name="mix_out_ln1",
    )(a1, a2, a3, proj, proj, proj, proj, proj, x, w_out, conv_w, g_attn, g_conv, ln_g, ln_b)


def _xattn_kernel(x_ref, wq_ref, k_ref, v_ref, wo_ref, lg_ref, lb_ref, o_ref):
    x = x_ref[...]
    d = x.shape[1]
    hd = d // N_MEM_HEADS
    q = _dot(x.astype(BF16), wq_ref[...]).astype(BF16)
    scale = hd ** -0.5
    outs = []
    for h in range(N_MEM_HEADS):
        sl = slice(h * hd, (h + 1) * hd)
        s = _dot_nt(q[:, sl], k_ref[:, sl]) * scale
        m = jnp.max(s, axis=-1, keepdims=True)
        p = jnp.exp(s - m)
        p = p / jnp.sum(p, axis=-1, keepdims=True)
        outs.append(_dot(p.astype(BF16), v_ref[:, sl]).astype(BF16))
    y = _dot(jnp.concatenate(outs, axis=1), wo_ref[...])
    o_ref[...] = _layer_norm(ALPHA * x + y, lg_ref[...], lb_ref[...])


def _xattn(x, w_q, kv, w_o, ln_g, ln_b, seq, mem_len):
    n_tok, d = x.shape
    tm = TM_XATTN
    row = lambda i: (i, 0)
    const = lambda i: (0, 0)
    tiles_per_seq = seq // tm
    return pl.pallas_call(
        _xattn_kernel,
        grid=(n_tok // tm,),
        in_specs=[pl.BlockSpec((tm, d), row),
                  pl.BlockSpec(w_q.shape, const),
                  pl.BlockSpec((mem_len, d), lambda i: (i // tiles_per_seq, 0)),
                  pl.BlockSpec((mem_len, d), lambda i: (i // tiles_per_seq, 1)),
                  pl.BlockSpec(w_o.shape, const),
                  pl.BlockSpec(ln_g.shape, const),
                  pl.BlockSpec(ln_b.shape, const)],
        out_specs=pl.BlockSpec((tm, d), row),
        out_shape=jax.ShapeDtypeStruct((n_tok, d), F32),
        compiler_params=_params("parallel"),
        name="xattn_ln2",
    )(x, w_q, kv, kv, w_o, ln_g, ln_b)


def _first_argmax(vals, index, sentinel):
    m = jnp.max(vals, axis=0, keepdims=True)
    i = jnp.min(jnp.where(vals == m, index, sentinel), axis=0, keepdims=True)
    return m, i


def _router_kernel(x_ref, wt_ref, bias_ref, idx_ref, gate_ref, rank_ref, cnt_ref):
    step = pl.program_id(0)
    tm = x_ref.shape[0]

    @pl.when(step == 0)
    def _():
        cnt_ref[...] = jnp.zeros_like(cnt_ref)

    logits = lax.dot_general(wt_ref[...], x_ref[...], (((1,), (1,)), ((), ())),
                             precision=lax.Precision.HIGHEST, preferred_element_type=F32)
    scores = 1.0 / (1.0 + jnp.exp(-logits))
    choice = scores + bias_ref[...]

    sub = lax.broadcasted_iota(I32, (GROUP_SIZE, tm), 0)
    group_scores = []
    for g in range(N_GROUPS):
        c = choice[g * GROUP_SIZE:(g + 1) * GROUP_SIZE, :]
        m1, i1 = _first_argmax(c, sub, GROUP_SIZE)
        m2 = jnp.max(jnp.where(sub == i1, -jnp.inf, c), axis=0, keepdims=True)
        group_scores.append(m1 + m2)
    gs = jnp.concatenate(group_scores, axis=0)
    gidx = lax.broadcasted_iota(I32, (N_GROUPS, tm), 0)
    gsel = jnp.zeros((N_GROUPS, tm), F32)
    for _ in range(TOPK_GROUPS):
        _, gi = _first_argmax(gs, gidx, N_GROUPS)
        hit = gidx == gi
        gsel = jnp.where(hit, 1.0, gsel)
        gs = jnp.where(hit, -jnp.inf, gs)
    masked = jnp.concatenate(
        [jnp.where(gsel[g:g + 1, :] > 0.0, choice[g * GROUP_SIZE:(g + 1) * GROUP_SIZE, :], -jnp.inf)
         for g in range(N_GROUPS)], axis=0)

    eidx = lax.broadcasted_iota(I32, (N_EXPERTS, tm), 0)
    hits, idxs, ws = [], [], []
    for _ in range(TOP_K):
        _, ei = _first_argmax(masked, eidx, N_EXPERTS)
        hit = eidx == ei
        hits.append(hit)
        idxs.append(ei)
        ws.append(jnp.sum(jnp.where(hit, scores, 0.0), axis=0, keepdims=True))
        masked = jnp.where(hit, -jnp.inf, masked)
    wsum = ws[0]
    for w in ws[1:]:
        wsum = wsum + w
    gate_ref[...] = jnp.concatenate([w / wsum * ROUTED_SCALE for w in ws], axis=0)
    idx_ref[...] = jnp.concatenate(idxs, axis=0)

    sel = hits[0]
    for hit in hits[1:]:
        sel = sel | hit
    self32 = jnp.where(sel, 1.0, 0.0)
    t_row = lax.broadcasted_iota(I32, (tm, tm), 0)
    t_col = lax.broadcasted_iota(I32, (tm, tm), 1)
    tri = jnp.where(t_row < t_col, 1.0, 0.0).astype(BF16)
    before = _dot(self32.astype(BF16), tri) + cnt_ref[:, 0:1]
    rank_ref[...] = jnp.concatenate(
        [jnp.sum(jnp.where(hit, before, 0.0), axis=0, keepdims=True) for hit in hits],
        axis=0).astype(I32)
    cnt_ref[...] += jnp.sum(self32, axis=1, keepdims=True)


def _router(x, w_router_t, bias_col):
    n_tok, d = x.shape
    tm = TM_ROUTER
    tok = lambda i: (0, i)
    return pl.pallas_call(
        _router_kernel,
        grid=(n_tok // tm,),
        in_specs=[pl.BlockSpec((tm, d), lambda i: (i, 0)),
                  pl.BlockSpec(w_router_t.shape, lambda i: (0, 0)),
                  pl.BlockSpec(bias_col.shape, lambda i: (0, 0))],
        out_specs=[pl.BlockSpec((TOP_K, tm), tok), pl.BlockSpec((TOP_K, tm), tok),
                   pl.BlockSpec((TOP_K, tm), tok),
                   pl.BlockSpec((N_EXPERTS, LANES), lambda i: (0, 0))],
        out_shape=[jax.ShapeDtypeStruct((TOP_K, n_tok), I32),
                   jax.ShapeDtypeStruct((TOP_K, n_tok), F32),
                   jax.ShapeDtypeStruct((TOP_K, n_tok), I32),
                   jax.ShapeDtypeStruct((N_EXPERTS, LANES), F32)],
        compiler_params=_params("arbitrary"),
        name="router",
    )(x, w_router_t, bias_col)


def _chunk_rows(row, n=1):
    return pl.ds(pl.multiple_of(row * ROW_CHUNKS, ROW_CHUNKS), n * ROW_CHUNKS)


def _to_chunks(ref, val):
    m = val.shape[0]
    for c in range(ROW_CHUNKS):
        ref[pl.ds(c, m, stride=ROW_CHUNKS), :] = val[:, c * LANES:(c + 1) * LANES]


def _from_chunks(ref, m):
    return jnp.concatenate([ref[pl.ds(c, m, stride=ROW_CHUNKS), :] for c in range(ROW_CHUNKS)], axis=1)


def _row_copy(src_ref, src_row, dst_ref, dst_row, sem):
    return pltpu.make_async_copy(src_ref.at[_chunk_rows(src_row), :], dst_ref.at[_chunk_rows(dst_row), :], sem)


def _dispatch_kernel(pad_ref, dest_ref, x_ref, xs_ref, rows_ref, zero_ref, sem):
    step = pl.program_id(0)
    tm = x_ref.shape[0]

    @pl.when(step == 0)
    def _():
        zero_ref[...] = jnp.zeros_like(zero_ref)

        def fill(e):
            return pltpu.make_async_copy(zero_ref, xs_ref.at[_chunk_rows(pad_ref[e], EXPERT_ROWS), :], sem)

        def start(e, c):
            fill(e).start()
            return c

        def wait(e, c):
            fill(e).wait()
            return c

        lax.fori_loop(0, N_EXPERTS, start, 0)
        lax.fori_loop(0, N_EXPERTS, wait, 0)

    _to_chunks(rows_ref, x_ref[...])

    def start_rows(t, c):
        for k in range(TOP_K):
            _row_copy(rows_ref, t, xs_ref, dest_ref[k, t], sem).start()
        return c

    def wait_rows(t, c):
        for k in range(TOP_K):
            _row_copy(rows_ref, t, xs_ref, dest_ref[k, t], sem).wait()
        return c

    lax.fori_loop(0, tm, start_rows, 0)
    lax.fori_loop(0, tm, wait_rows, 0)


def _dispatch(pad_start, dest, x, n_rows):
    n_tok, w = x.shape
    assert w == ROW_CHUNKS * LANES
    tm = TM_DISPATCH
    return pl.pallas_call(
        _dispatch_kernel,
        grid_spec=pltpu.PrefetchScalarGridSpec(
            num_scalar_prefetch=1,
            grid=(n_tok // tm,),
            in_specs=[pl.BlockSpec((TOP_K, tm), lambda i, pad: (0, i), memory_space=pltpu.SMEM),
                      pl.BlockSpec((tm, w), lambda i, pad: (i, 0))],
            out_specs=pl.BlockSpec(memory_space=pl.ANY),
            scratch_shapes=[pltpu.VMEM((tm * ROW_CHUNKS, LANES), F32),
                            pltpu.VMEM((EXPERT_ROWS * ROW_CHUNKS, LANES), F32),
                            pltpu.SemaphoreType.DMA(())]),
        out_shape=jax.ShapeDtypeStruct(((n_rows + EXPERT_ROWS) * ROW_CHUNKS, LANES), F32),
        compiler_params=_params("arbitrary"),
        name="dispatch",
    )(pad_start, dest, x)


def _expert_kernel(bexp_ref, nused_ref, xs_ref, wg_ref, wu_ref, wd_ref, ys_ref, wg_s, wu_s, wd_s):
    i = pl.program_id(0)

    @pl.when(i < nused_ref[0])
    def _():
        e = bexp_ref[i]
        e_prev = bexp_ref[jnp.maximum(i - 1, 0)]

        @pl.when((i == 0) | (e != e_prev))
        def _():
            wg_s[...] = wg_ref[...].astype(BF16)
            wu_s[...] = wu_ref[...].astype(BF16)
            wd_s[...] = wd_ref[...].astype(BF16)

        xb = _from_chunks(xs_ref, EXPERT_ROWS).astype(BF16)
        hmid = (_silu(_dot(xb, wg_s[...])) * _dot(xb, wu_s[...])).astype(BF16)
        _to_chunks(ys_ref, _dot(hmid, wd_s[...]))


def _experts(block_exp, n_used, xs, w_gate, w_up, w_down, n_blocks):
    _, d, ff = w_gate.shape
    assert d == ROW_CHUNKS * LANES
    block = (EXPERT_ROWS * ROW_CHUNKS, LANES)
    rows = lambda i, bexp, nused: (jnp.minimum(i, nused[0] - 1), 0)
    wsel = lambda i, bexp, nused: (bexp[jnp.minimum(i, nused[0] - 1)], 0, 0)
    return pl.pallas_call(
        _expert_kernel,
        grid_spec=pltpu.PrefetchScalarGridSpec(
            num_scalar_prefetch=2,
            grid=(n_blocks,),
            in_specs=[pl.BlockSpec(block, rows),
                      pl.BlockSpec((None, d, ff), wsel),
                      pl.BlockSpec((None, d, ff), wsel),
                      pl.BlockSpec((None, ff, d), wsel)],
            out_specs=pl.BlockSpec(block, rows),
            scratch_shapes=[pltpu.VMEM((d, ff), BF16), pltpu.VMEM((d, ff), BF16),
                            pltpu.VMEM((ff, d), BF16)]),
        out_shape=jax.ShapeDtypeStruct((n_blocks * EXPERT_ROWS * ROW_CHUNKS, LANES), F32),
        compiler_params=_params("arbitrary"),
        name="experts",
    )(block_exp, n_used, xs, w_gate, w_up, w_down)


def _combine_kernel(dest_ref, gate_ref, x_ref, ys_ref, wsg_ref, wsu_ref, wsd_ref, lg_ref, lb_ref,
                    o_ref, buf, sem):
    tm = x_ref.shape[0]

    def start_rows(t, c):
        for k in range(TOP_K):
            _row_copy(ys_ref, dest_ref[k, t], buf.at[k], t, sem).start()
        return c

    def wait_rows(t, c):
        for k in range(TOP_K):
            _row_copy(ys_ref, dest_ref[k, t], buf.at[k], t, sem).wait()
        return c

    lax.fori_loop(0, tm, start_rows, 0)

    x = x_ref[...]
    xb = x.astype(BF16)
    hmid = (_silu(_dot(xb, wsg_ref[...])) * _dot(xb, wsu_ref[...])).astype(BF16)
    acc = ALPHA * x + _dot(hmid, wsd_ref[...])

    lax.fori_loop(0, tm, wait_rows, 0)
    gates = gate_ref[...]
    routed = _from_chunks(buf.at[0], tm) * gates[:, 0:1]
    for k in range(1, TOP_K):
        routed += _from_chunks(buf.at[k], tm) * gates[:, k:k + 1]
    o_ref[...] = _layer_norm(acc + routed, lg_ref[...], lb_ref[...])


def _combine(dest, gates_t, x, ys, ws_gate, ws_up, ws_down, ln_g, ln_b):
    n_tok, d = x.shape
    tm = TM_COMBINE
    row = lambda i: (i, 0)
    const = lambda i: (0, 0)
    return pl.pallas_call(
        _combine_kernel,
        grid=(n_tok // tm,),
        in_specs=[pl.BlockSpec((TOP_K, tm), lambda i: (0, i), memory_space=pltpu.SMEM),
                  pl.BlockSpec((tm, TOP_K), row),
                  pl.BlockSpec((tm, d), row),
                  pl.BlockSpec(memory_space=pl.ANY),
                  pl.BlockSpec(ws_gate.shape, const),
                  pl.BlockSpec(ws_up.shape, const),
                  pl.BlockSpec(ws_down.shape, const),
                  pl.BlockSpec(ln_g.shape, const),
                  pl.BlockSpec(ln_b.shape, const)],
        out_specs=pl.BlockSpec((tm, d), row),
        out_shape=jax.ShapeDtypeStruct((n_tok, d), F32),
        scratch_shapes=[pltpu.VMEM((TOP_K, tm * ROW_CHUNKS, LANES), F32), pltpu.SemaphoreType.DMA(())],
        compiler_params=_params("arbitrary"),
        name="combine_shared_ln3",
    )(dest, gates_t, x, ys, ws_gate, ws_up, ws_down, ln_g, ln_b)


def _layer(x, mem, w_in, conv_w, g_attn_out, g_conv_out, w_out, ln1_g, ln1_b, w_q_mem, w_kv_mem,
           w_o_mem, ln2_g, ln2_b, w_router, router_bias, w_gate, w_up, w_down, ws_gate, ws_up,
           ws_down, ln3_g, ln3_b):
    batch, seq, d = x.shape
    mem_len = mem.shape[1]
    n_tok = batch * seq
    xf = x.reshape(n_tok, d)
    row = lambda v: v.reshape(1, -1)

    proj = _matmul(xf.astype(BF16), w_in.astype(BF16), F32, TM_PROJ, TN_PROJ)
    branches = [_attn_branch(proj, batch, seq, dil) for _, dil in DILATED_PATTERNS]
    x1 = _mix_out(*branches, proj, xf, w_out.astype(BF16), conv_w, row(g_attn_out), row(g_conv_out),
                  row(ln1_g), row(ln1_b), seq)

    kv = _matmul(mem.reshape(batch * mem_len, d).astype(BF16), w_kv_mem.astype(BF16), BF16,
                 mem_len, 1024)
    x2 = _xattn(x1, w_q_mem.astype(BF16), kv, w_o_mem.astype(BF16), row(ln2_g), row(ln2_b), seq,
                mem_len)

    idx, gates, rank, cnt = _router(x2, w_router.T, router_bias.reshape(N_EXPERTS, 1))
    counts = cnt[:, 0].astype(I32)
    padded = (counts + EXPERT_ROWS - 1) // EXPERT_ROWS * EXPERT_ROWS
    seg_end = jnp.cumsum(padded)
    seg_start = seg_end - padded
    dest = seg_start[idx] + rank
    n_blocks = -(-(n_tok * TOP_K) // EXPERT_ROWS) + N_EXPERTS
    block_first = jnp.arange(n_blocks, dtype=I32) * EXPERT_ROWS
    block_exp = jnp.minimum(jnp.searchsorted(seg_end, block_first, side='right'),
                            N_EXPERTS - 1).astype(I32)
    n_used = (seg_end[-1:] // EXPERT_ROWS).astype(I32)

    xs = _dispatch(seg_start + counts, dest, x2, n_blocks * EXPERT_ROWS)
    ys = _experts(block_exp, n_used, xs, w_gate, w_up, w_down, n_blocks)
    out = _combine(dest, gates.T, x2, ys, ws_gate.astype(BF16), ws_up.astype(BF16),
                   ws_down.astype(BF16), row(ln3_g), row(ln3_b))
    return out.reshape(batch, seq, d)


def kernel(x, mem, w_in, conv_w, g_attn_out, g_conv_out, w_out, ln1_g, ln1_b, w_q_mem, w_kv_mem, w_o_mem, ln2_g, ln2_b, w_router, router_bias, w_gate, w_up, w_down, ws_gate, ws_up, ws_down, ln3_g, ln3_b):
    for l in range(DEPTH):
        x = _layer(x, mem, w_in[l], conv_w[l], g_attn_out[l], g_conv_out[l], w_out[l], ln1_g[l],
                   ln1_b[l], w_q_mem[l], w_kv_mem[l], w_o_mem[l], ln2_g[l], ln2_b[l], w_router[l],
                   router_bias[l], w_gate[l], w_up[l], w_down[l], ws_gate[l], ws_up[l], ws_down[l],
                   ln3_g[l], ln3_b[l])
    return x
```

```python
import functools

import jax
import jax.numpy as jnp
from jax import lax
from jax.experimental import pallas as pl
from jax.experimental.pallas import tpu as pltpu

F32 = jnp.float32
BF16 = jnp.bfloat16
I32 = jnp.int32

HEAD_DIM = 128
N_ATTN_HEADS = 8
ATTN_WIDTH = N_ATTN_HEADS * HEAD_DIM
CONV_WIDTH = 1024
PROJ_PARTS = 6
DILATED_PATTERNS = ((128, 1), (512, 4), (2048, 16))
ATTN_BLOCK = 128
N_MEM_HEADS = 4
N_EXPERTS = 64
TOP_K = 8
N_GROUPS = 8
GROUP_SIZE = N_EXPERTS // N_GROUPS
TOPK_GROUPS = 4
ROUTED_SCALE = 2.5
LN_EPS = 1e-5
RMS_EPS = 1e-6
DEPTH = 1
ALPHA = (2.0 * DEPTH) ** 0.25

LANES = 128
SUBLANES = 8
ROW_CHUNKS = 16
VMEM_LIMIT = 56 * 1024 * 1024
ATTN_OUT_COLS = ATTN_WIDTH + LANES
EXPERT_ROWS = 256
TM_PROJ, TN_PROJ = 1024, 512
TM_MIX = 256
TM_XATTN = 256
TM_ROUTER = 512
TM_DISPATCH = 256
TM_COMBINE = 128


def _params(*semantics):
    return pltpu.CompilerParams(dimension_semantics=semantics, vmem_limit_bytes=VMEM_LIMIT)


def _layer_norm(r, g, b):
    mu = jnp.mean(r, axis=-1, keepdims=True)
    c = r - mu
    var = jnp.mean(c * c, axis=-1, keepdims=True)
    return c * lax.rsqrt(var + LN_EPS) * g + b


def _rms_norm(v, g):
    return v * lax.rsqrt(jnp.mean(v * v, axis=-1, keepdims=True) + RMS_EPS) * g


def _dot(a, b):
    return jnp.dot(a, b, preferred_element_type=F32)


def _dot_nt(a, b):
    return lax.dot_general(a, b, (((1,), (1,)), ((), ())), preferred_element_type=F32)


def _silu(v):
    return v / (1.0 + jnp.exp(-v))


def _matmul_kernel(a_ref, w_ref, o_ref):
    o_ref[...] = _dot(a_ref[...], w_ref[...]).astype(o_ref.dtype)


def _matmul(a, w, out_dtype, tm, tn):
    m, k = a.shape
    n = w.shape[1]
    return pl.pallas_call(
        _matmul_kernel,
        grid=(m // tm, n // tn),
        in_specs=[pl.BlockSpec((tm, k), lambda i, j: (i, 0)),
                  pl.BlockSpec((k, tn), lambda i, j: (0, j))],
        out_specs=pl.BlockSpec((tm, tn), lambda i, j: (i, j)),
        out_shape=jax.ShapeDtypeStruct((m, n), out_dtype),
        compiler_params=_params("parallel", "arbitrary"),
        name="matmul",
    )(a, w)


def _attn_kernel(q_ref, kp_ref, kc_ref, vp_ref, vc_ref, o_ref):
    n = pl.program_id(2)
    blk = ATTN_BLOCK
    row = lax.broadcasted_iota(I32, (blk, 2 * blk), 0)
    col = lax.broadcasted_iota(I32, (blk, 2 * blk), 1)
    dist = row + blk - col
    valid = (dist >= 0) & (dist <= blk) & ((col >= blk) | (n > 0))
    lane = lax.broadcasted_iota(I32, (blk, LANES), 1)
    lse_tile = jnp.zeros((blk, LANES), F32)
    scale = HEAD_DIM ** -0.5
    for h in range(N_ATTN_HEADS):
        sl = slice(h * HEAD_DIM, (h + 1) * HEAD_DIM)
        q = q_ref[:, sl].astype(BF16)
        k = jnp.concatenate([kp_ref[:, sl], kc_ref[:, sl]], axis=0).astype(BF16)
        v = jnp.concatenate([vp_ref[:, sl], vc_ref[:, sl]], axis=0).astype(BF16)
        s = _dot_nt(q, k) * scale
        s = jnp.where(valid, s, -jnp.inf)
        m = jnp.max(s, axis=-1, keepdims=True)
        p = jnp.exp(s - m)
        l = jnp.sum(p, axis=-1, keepdims=True)
        o_ref[:, sl] = _dot((p / l).astype(BF16), v)
        lse_tile = jnp.where(lane == h, m + jnp.log(l), lse_tile)
    o_ref[:, ATTN_WIDTH:] = lse_tile


def _attn_branch(proj, batch, seq, dilation):
    n_tok, cols = proj.shape
    sub_len = seq // dilation
    nb = sub_len // ATTN_BLOCK
    view = proj.reshape(n_tok // dilation, dilation * cols)
    w = ATTN_WIDTH

    def spec(part, prev):
        def index(b, r, n):
            blk = jnp.maximum(n - 1, 0) if prev else n
            return (b * nb + blk, r * PROJ_PARTS + part)
        return pl.BlockSpec((ATTN_BLOCK, w), index)

    out = pl.pallas_call(
        _attn_kernel,
        grid=(batch, dilation, nb),
        in_specs=[spec(0, False), spec(1, True), spec(1, False), spec(2, True), spec(2, False)],
        out_specs=pl.BlockSpec((ATTN_BLOCK, ATTN_OUT_COLS), lambda b, r, n: (b * nb + n, r)),
        out_shape=jax.ShapeDtypeStruct((n_tok // dilation, dilation * ATTN_OUT_COLS), F32),
        compiler_params=_params("parallel", "parallel", "arbitrary"),
        name=f"dilated_attn_d{dilation}",
    )(view, view, view, view, view)
    return out.reshape(n_tok, ATTN_OUT_COLS)


def _mix_out_kernel(a1_ref, a2_ref, a3_ref, b_ref, c_ref, h_ref, cp_ref, hp_ref, x_ref,
                    w_ref, cw_ref, ga_ref, gc_ref, lg_ref, lb_ref, o_ref, *, tiles_per_seq):
    i = pl.program_id(0)
    tm = x_ref.shape[0]
    branches = (a1_ref, a2_ref, a3_ref)
    lses = [a[:, ATTN_WIDTH:] for a in branches]
    top = jnp.maximum(jnp.maximum(lses[0], lses[1]), lses[2])
    es = [jnp.exp(v - top) for v in lses]
    den = es[0] + es[1] + es[2]
    wts = [e / den for e in es]
    heads = []
    for h in range(N_ATTN_HEADS):
        sl = slice(h * HEAD_DIM, (h + 1) * HEAD_DIM)
        acc = wts[0][:, h:h + 1] * a1_ref[:, sl]
        acc += wts[1][:, h:h + 1] * a2_ref[:, sl]
        acc += wts[2][:, h:h + 1] * a3_ref[:, sl]
        heads.append(acc)
    attn = jnp.concatenate(heads, axis=1)

    z = c_ref[...] * h_ref[...]
    zp = jnp.where(i % tiles_per_seq == 0, 0.0, cp_ref[...] * hp_ref[...])
    zz = jnp.concatenate([zp, z], axis=0)
    z1 = pltpu.roll(zz, 1, axis=0)[SUBLANES:]
    z2 = pltpu.roll(zz, 2, axis=0)[SUBLANES:]
    conv = b_ref[...] * (cw_ref[0:1, :] * z2 + cw_ref[1:2, :] * z1 + cw_ref[2:3, :] * z)

    mixed = jnp.concatenate([_rms_norm(attn, ga_ref[...]), _rms_norm(conv, gc_ref[...])], axis=1)
    y = _dot(mixed.astype(BF16), w_ref[...])
    o_ref[...] = _layer_norm(ALPHA * x_ref[...] + y, lg_ref[...], lb_ref[...])


def _mix_out(a1, a2, a3, proj, x, w_out, conv_w, g_attn, g_conv, ln_g, ln_b, seq):
    n_tok, d = x.shape
    tm = TM_MIX
    rows8 = tm // SUBLANES
    row = lambda i: (i, 0)
    const = lambda i: (0, 0)
    prev = lambda part: (lambda i: (jnp.maximum(i * rows8 - 1, 0), part))
    return pl.pallas_call(
        functools.partial(_mix_out_kernel, tiles_per_seq=seq // tm),
        grid=(n_tok // tm,),
        in_specs=[pl.BlockSpec((tm, ATTN_OUT_COLS), row)] * 3 + [
            pl.BlockSpec((tm, CONV_WIDTH), lambda i: (i, 3)),
            pl.BlockSpec((tm, CONV_WIDTH), lambda i: (i, 4)),
            pl.BlockSpec((tm, CONV_WIDTH), lambda i: (i, 5)),
            pl.BlockSpec((SUBLANES, CONV_WIDTH), prev(4)),
            pl.BlockSpec((SUBLANES, CONV_WIDTH), prev(5)),
            pl.BlockSpec((tm, d), row),
            pl.BlockSpec(w_out.shape, const),
            pl.BlockSpec(conv_w.shape, const),
            pl.BlockSpec(g_attn.shape, const),
            pl.BlockSpec(g_conv.shape, const),
            pl.BlockSpec(ln_g.shape, const),
            pl.BlockSpec(ln_b.shape, const)],
        out_specs=pl.BlockSpec((tm, d), row),
        out_shape=jax.ShapeDtypeStruct((n_tok, d), F32),
        compiler_params=_params("parallel"),
        name="mix_out_ln1",
    )(a1, a2, a3, proj, proj, proj, proj, proj, x, w_out, conv_w, g_attn, g_conv, ln_g, ln_b)


def _xattn_kernel(x_ref, wq_ref, k_ref, v_ref, wo_ref, lg_ref, lb_ref, o_ref):
    x = x_ref[...]
    d = x.shape[1]
    hd = d // N_MEM_HEADS
    q = _dot(x.astype(BF16), wq_ref[...]).astype(BF16)
    scale = hd ** -0.5
    outs = []
    for h in range(N_MEM_HEADS):
        sl = slice(h * hd, (h + 1) * hd)
        s = _dot_nt(q[:, sl], k_ref[:, sl]) * scale
        m = jnp.max(s, axis=-1, keepdims=True)
        p = jnp.exp(s - m)
        p = p / jnp.sum(p, axis=-1, keepdims=True)
        outs.append(_dot(p.astype(BF16), v_ref[:, sl]).astype(BF16))
    y = _dot(jnp.concatenate(outs, axis=1), wo_ref[...])
    o_ref[...] = _layer_norm(ALPHA * x + y, lg_ref[...], lb_ref[...])


def _xattn(x, w_q, kv, w_o, ln_g, ln_b, seq, mem_len):
    n_tok, d = x.shape
    tm = TM_XATTN
    row = lambda i: (i, 0)
    const = lambda i: (0, 0)
    tiles_per_seq = seq // tm
    return pl.pallas_call(
        _xattn_kernel,
        grid=(n_tok // tm,),
        in_specs=[pl.BlockSpec((tm, d), row),
                  pl.BlockSpec(w_q.shape, const),
                  pl.BlockSpec((mem_len, d), lambda i: (i // tiles_per_seq, 0)),
                  pl.BlockSpec((mem_len, d), lambda i: (i // tiles_per_seq, 1)),
                  pl.BlockSpec(w_o.shape, const),
                  pl.BlockSpec(ln_g.shape, const),
                  pl.BlockSpec(ln_b.shape, const)],
        out_specs=pl.BlockSpec((tm, d), row),
        out_shape=jax.ShapeDtypeStruct((n_tok, d), F32),
        compiler_params=_params("parallel"),
        name="xattn_ln2",
    )(x, w_q, kv, kv, w_o, ln_g, ln_b)


def _first_argmax(vals, index, sentinel):
    m = jnp.max(vals, axis=0, keepdims=True)
    i = jnp.min(jnp.where(vals == m, index, sentinel), axis=0, keepdims=True)
    return m, i


def _router_kernel(x_ref, wt_ref, bias_ref, idx_ref, gate_ref, rank_ref, cnt_ref):
    step = pl.program_id(0)
    tm = x_ref.shape[0]

    @pl.when(step == 0)
    def _():
        cnt_ref[...] = jnp.zeros_like(cnt_ref)

    logits = lax.dot_general(wt_ref[...], x_ref[...], (((1,), (1,)), ((), ())),
                             precision=lax.Precision.HIGHEST, preferred_element_type=F32)
    scores = 1.0 / (1.0 + jnp.exp(-logits))
    choice = scores + bias_ref[...]

    sub = lax.broadcasted_iota(I32, (GROUP_SIZE, tm), 0)
    group_scores = []
    for g in range(N_GROUPS):
        c = choice[g * GROUP_SIZE:(g + 1) * GROUP_SIZE, :]
        m1, i1 = _first_argmax(c, sub, GROUP_SIZE)
        m2 = jnp.max(jnp.where(sub == i1, -jnp.inf, c), axis=0, keepdims=True)
        group_scores.append(m1 + m2)
    gs = jnp.concatenate(group_scores, axis=0)
    gidx = lax.broadcasted_iota(I32, (N_GROUPS, tm), 0)
    gsel = jnp.zeros((N_GROUPS, tm), F32)
    for _ in range(TOPK_GROUPS):
        _, gi = _first_argmax(gs, gidx, N_GROUPS)
        hit = gidx == gi
        gsel = jnp.where(hit, 1.0, gsel)
        gs = jnp.where(hit, -jnp.inf, gs)
    masked = jnp.concatenate(
        [jnp.where(gsel[g:g + 1, :] > 0.0, choice[g * GROUP_SIZE:(g + 1) * GROUP_SIZE, :], -jnp.inf)
         for g in range(N_GROUPS)], axis=0)

    eidx = lax.broadcasted_iota(I32, (N_EXPERTS, tm), 0)
    hits, idxs, ws = [], [], []
    for _ in range(TOP_K):
        _, ei = _first_argmax(masked, eidx, N_EXPERTS)
        hit = eidx == ei
        hits.append(hit)
        idxs.append(ei)
        ws.append(jnp.sum(jnp.where(hit, scores, 0.0), axis=0, keepdims=True))
        masked = jnp.where(hit, -jnp.inf, masked)
    wsum = ws[0]
    for w in ws[1:]:
        wsum = wsum + w
    gate_ref[...] = jnp.concatenate([w / wsum * ROUTED_SCALE for w in ws], axis=0)
    idx_ref[...] = jnp.concatenate(idxs, axis=0)

    sel = hits[0]
    for hit in hits[1:]:
        sel = sel | hit
    self32 = jnp.where(sel, 1.0, 0.0)
    t_row = lax.broadcasted_iota(I32, (tm, tm), 0)
    t_col = lax.broadcasted_iota(I32, (tm, tm), 1)
    tri = jnp.where(t_row < t_col, 1.0, 0.0).astype(BF16)
    before = _dot(self32.astype(BF16), tri) + cnt_ref[:, 0:1]
    rank_ref[...] = jnp.concatenate(
        [jnp.sum(jnp.where(hit, before, 0.0), axis=0, keepdims=True) for hit in hits],
        axis=0).astype(I32)
    cnt_ref[...] += jnp.sum(self32, axis=1, keepdims=True)


def _router(x, w_router_t, bias_col):
    n_tok, d = x.shape
    tm = TM_ROUTER
    tok = lambda i: (0, i)
    return pl.pallas_call(
        _router_kernel,
        grid=(n_tok // tm,),
        in_specs=[pl.BlockSpec((tm, d), lambda i: (i, 0)),
                  pl.BlockSpec(w_router_t.shape, lambda i: (0, 0)),
                  pl.BlockSpec(bias_col.shape, lambda i: (0, 0))],
        out_specs=[pl.BlockSpec((TOP_K, tm), tok), pl.BlockSpec((TOP_K, tm), tok),
                   pl.BlockSpec((TOP_K, tm), tok),
                   pl.BlockSpec((N_EXPERTS, LANES), lambda i: (0, 0))],
        out_shape=[jax.ShapeDtypeStruct((TOP_K, n_tok), I32),
                   jax.ShapeDtypeStruct((TOP_K, n_tok), F32),
                   jax.ShapeDtypeStruct((TOP_K, n_tok), I32),
                   jax.ShapeDtypeStruct((N_EXPERTS, LANES), F32)],
        compiler_params=_params("arbitrary"),
        name="router",
    )(x, w_router_t, bias_col)


def _chunk_rows(row, n=1):
    return pl.ds(pl.multiple_of(row * ROW_CHUNKS, ROW_CHUNKS), n * ROW_CHUNKS)


def _to_chunks(ref, val):
    m = val.shape[0]
    for c in range(ROW_CHUNKS):
        ref[pl.ds(c, m, stride=ROW_CHUNKS), :] = val[:, c * LANES:(c + 1) * LANES]


def _from_chunks(ref, m):
    return jnp.concatenate([ref[pl.ds(c, m, stride=ROW_CHUNKS), :] for c in range(ROW_CHUNKS)], axis=1)


def _row_copy(src_ref, src_row, dst_ref, dst_row, sem):
    return pltpu.make_async_copy(src_ref.at[_chunk_rows(src_row), :], dst_ref.at[_chunk_rows(dst_row), :], sem)


def _dispatch_kernel(pad_ref, dest_ref, x_ref, xs_ref, rows_ref, zero_ref, sem):
    step = pl.program_id(0)
    tm = x_ref.shape[0]

    @pl.when(step == 0)
    def _():
        zero_ref[...] = jnp.zeros_like(zero_ref)

        def fill(e):
            return pltpu.make_async_copy(zero_ref, xs_ref.at[_chunk_rows(pad_ref[e], EXPERT_ROWS), :], sem)

        def start(e, c):
            fill(e).start()
            return c

        def wait(e, c):
            fill(e).wait()
            return c

        lax.fori_loop(0, N_EXPERTS, start, 0)
        lax.fori_loop(0, N_EXPERTS, wait, 0)

    _to_chunks(rows_ref, x_ref[...])

    def start_rows(t, c):
        for k in range(TOP_K):
            _row_copy(rows_ref, t, xs_ref, dest_ref[k, t], sem).start()
        return c

    def wait_rows(t, c):
        for k in range(TOP_K):
            _row_copy(rows_ref, t, xs_ref, dest_ref[k, t], sem).wait()
        return c

    lax.fori_loop(0, tm, start_rows, 0)
    lax.fori_loop(0, tm, wait_rows, 0)


def _dispatch(pad_start, dest, x, n_rows):
    n_tok, w = x.shape
    assert w == ROW_CHUNKS * LANES
    tm = TM_DISPATCH
    return pl.pallas_call(
        _dispatch_kernel,
        grid_spec=pltpu.PrefetchScalarGridSpec(
            num_scalar_prefetch=1,
            grid=(n_tok // tm,),
            in_specs=[pl.BlockSpec((TOP_K, tm), lambda i, pad: (0, i), memory_space=pltpu.SMEM),
                      pl.BlockSpec((tm, w), lambda i, pad: (i, 0))],
            out_specs=pl.BlockSpec(memory_space=pl.ANY),
            scratch_shapes=[pltpu.VMEM((tm * ROW_CHUNKS, LANES), F32),
                            pltpu.VMEM((EXPERT_ROWS * ROW_CHUNKS, LANES), F32),
                            pltpu.SemaphoreType.DMA(())]),
        out_shape=jax.ShapeDtypeStruct(((n_rows + EXPERT_ROWS) * ROW_CHUNKS, LANES), F32),
        compiler_params=_params("arbitrary"),
        name="dispatch",
    )(pad_start, dest, x)


def _expert_kernel(bexp_ref, nused_ref, xs_ref, wg_ref, wu_ref, wd_ref, ys_ref, wg_s, wu_s, wd_s):
    i = pl.program_id(0)

    @pl.when(i < nused_ref[0])
    def _():
        e = bexp_ref[i]
        e_prev = bexp_ref[jnp.maximum(i - 1, 0)]

        @pl.when((i == 0) | (e != e_prev))
        def _():
            wg_s[...] = wg_ref[...].astype(BF16)
            wu_s[...] = wu_ref[...].astype(BF16)
            wd_s[...] = wd_ref[...].astype(BF16)

        xb = _from_chunks(xs_ref, EXPERT_ROWS).astype(BF16)
        hmid = (_silu(_dot(xb, wg_s[...])) * _dot(xb, wu_s[...])).astype(BF16)
        _to_chunks(ys_ref, _dot(hmid, wd_s[...]))


def _experts(block_exp, n_used, xs, w_gate, w_up, w_down, n_blocks):
    _, d, ff = w_gate.shape
    assert d == ROW_CHUNKS * LANES
    block = (EXPERT_ROWS * ROW_CHUNKS, LANES)
    rows = lambda i, bexp, nused: (jnp.minimum(i, nused[0] - 1), 0)
    wsel = lambda i, bexp, nused: (bexp[jnp.minimum(i, nused[0] - 1)], 0, 0)
    return pl.pallas_call(
        _expert_kernel,
        grid_spec=pltpu.PrefetchScalarGridSpec(
            num_scalar_prefetch=2,
            grid=(n_blocks,),
            in_specs=[pl.BlockSpec(block, rows),
                      pl.BlockSpec((None, d, ff), wsel),
                      pl.BlockSpec((None, d, ff), wsel),
                      pl.BlockSpec((None, ff, d), wsel)],
            out_specs=pl.BlockSpec(block, rows),
            scratch_shapes=[pltpu.VMEM((d, ff), BF16), pltpu.VMEM((d, ff), BF16),
                            pltpu.VMEM((ff, d), BF16)]),
        out_shape=jax.ShapeDtypeStruct((n_blocks * EXPERT_ROWS * ROW_CHUNKS, LANES), F32),
        compiler_params=_params("arbitrary"),
        name="experts",
    )(block_exp, n_used, xs, w_gate, w_up, w_down)


def _combine_kernel(dest_ref, gate_ref, x_ref, ys_ref, wsg_ref, wsu_ref, wsd_ref, lg_ref, lb_ref,
                    o_ref, buf, sem):
    tm = x_ref.shape[0]

    def start_rows(t, c):
        for k in range(TOP_K):
            _row_copy(ys_ref, dest_ref[k, t], buf.at[k], t, sem).start()
        return c

    def wait_rows(t, c):
        for k in range(TOP_K):
            _row_copy(ys_ref, dest_ref[k, t], buf.at[k], t, sem).wait()
        return c

    lax.fori_loop(0, tm, start_rows, 0)

    x = x_ref[...]
    xb = x.astype(BF16)
    hmid = (_silu(_dot(xb, wsg_ref[...])) * _dot(xb, wsu_ref[...])).astype(BF16)
    acc = ALPHA * x + _dot(hmid, wsd_ref[...])

    lax.fori_loop(0, tm, wait_rows, 0)
    gates = gate_ref[...]
    routed = _from_chunks(buf.at[0], tm) * gates[:, 0:1]
    for k in range(1, TOP_K):
        routed += _from_chunks(buf.at[k], tm) * gates[:, k:k + 1]
    o_ref[...] = _layer_norm(acc + routed, lg_ref[...], lb_ref[...])


def _combine(dest, gates_t, x, ys, ws_gate, ws_up, ws_down, ln_g, ln_b):
    n_tok, d = x.shape
    tm = TM_COMBINE
    row = lambda i: (i, 0)
    const = lambda i: (0, 0)
    return pl.pallas_call(
        _combine_kernel,
        grid=(n_tok // tm,),
        in_specs=[pl.BlockSpec((TOP_K, tm), lambda i: (0, i), memory_space=pltpu.SMEM),
                  pl.BlockSpec((tm, TOP_K), row),
                  pl.BlockSpec((tm, d), row),
                  pl.BlockSpec(memory_space=pl.ANY),
                  pl.BlockSpec(ws_gate.shape, const),
                  pl.BlockSpec(ws_up.shape, const),
                  pl.BlockSpec(ws_down.shape, const),
                  pl.BlockSpec(ln_g.shape, const),
                  pl.BlockSpec(ln_b.shape, const)],
        out_specs=pl.BlockSpec((tm, d), row),
        out_shape=jax.ShapeDtypeStruct((n_tok, d), F32),
        scratch_shapes=[pltpu.VMEM((TOP_K, tm * ROW_CHUNKS, LANES), F32), pltpu.SemaphoreType.DMA(())],
        compiler_params=_params("arbitrary"),
        name="combine_shared_ln3",
    )(dest, gates_t, x, ys, ws_gate, ws_up, ws_down, ln_g, ln_b)


def _layer(x, mem, w_in, conv_w, g_attn_out, g_conv_out, w_out, ln1_g, ln1_b, w_q_mem, w_kv_mem,
           w_o_mem, ln2_g, ln2_b, w_router, router_bias, w_gate, w_up, w_down, ws_gate, ws_up,
           ws_down, ln3_g, ln3_b):
    batch, seq, d = x.shape
    mem_len = mem.shape[1]
    n_tok = batch * seq
    xf = x.reshape(n_tok, d)
    row = lambda v: v.reshape(1, -1)

    proj = _matmul(xf.astype(BF16), w_in.astype(BF16), F32, TM_PROJ, TN_PROJ)
    branches = [_attn_branch(proj, batch, seq, dil) for _, dil in DILATED_PATTERNS]
    x1 = _mix_out(*branches, proj, xf, w_out.astype(BF16), conv_w, row(g_attn_out), row(g_conv_out),
                  row(ln1_g), row(ln1_b), seq)

    kv = _matmul(mem.reshape(batch * mem_len, d).astype(BF16), w_kv_mem.astype(BF16), BF16,
                 mem_len, 1024)
    x2 = _xattn(x1, w_q_mem.astype(BF16), kv, w_o_mem.astype(BF16), row(ln2_g), row(ln2_b), seq,
                mem_len)

    idx, gates, rank, cnt = _router(x2, w_router.T, router_bias.reshape(N_EXPERTS, 1))
    counts = cnt[:, 0].astype(I32)
    padded = (counts + EXPERT_ROWS - 1) // EXPERT_ROWS * EXPERT_ROWS
    seg_end = jnp.cumsum(padded)
    seg_start = seg_end - padded
    experts = jnp.arange(N_EXPERTS, dtype=I32)
    dest = rank + jnp.sum(jnp.where(idx[..., None] == experts, seg_start, 0), axis=-1)
    n_blocks = -(-(n_tok * TOP_K) // EXPERT_ROWS) + N_EXPERTS
    block_first = jnp.arange(n_blocks, dtype=I32) * EXPERT_ROWS
    block_exp = jnp.minimum(jnp.sum(block_first[:, None] >= seg_end[None, :], axis=1),
                            N_EXPERTS - 1).astype(I32)
    n_used = (seg_end[-1:] // EXPERT_ROWS).astype(I32)

    xs = _dispatch(seg_start + counts, dest, x2, n_blocks * EXPERT_ROWS)
    ys = _experts(block_exp, n_used, xs, w_gate, w_up, w_down, n_blocks)
    out = _combine(dest, gates.T, x2, ys, ws_gate.astype(BF16), ws_up.astype(BF16),
                   ws_down.astype(BF16), row(ln3_g), row(ln3_b))
    return out.reshape(batch, seq, d)


def kernel(x, mem, w_in, conv_w, g_attn_out, g_conv_out, w_out, ln1_g, ln1_b, w_q_mem, w_kv_mem, w_o_mem, ln2_g, ln2_b, w_router, router_bias, w_gate, w_up, w_down, ws_gate, ws_up, ws_down, ln3_g, ln3_b):
    for l in range(DEPTH):
        x = _layer(x, mem, w_in[l], conv_w[l], g_attn_out[l], g_conv_out[l], w_out[l], ln1_g[l],
                   ln1_b[l], w_q_mem[l], w_kv_mem[l], w_o_mem[l], ln2_g[l], ln2_b[l], w_router[l],
                   router_bias[l], w_gate[l], w_up[l], w_down[l], ws_gate[l], ws_up[l], ws_down[l],
                   ln3_g[l], ln3_b[l])
    return x
```

```python
import functools

import jax
import jax.numpy as jnp
from jax import lax
from jax.experimental import pallas as pl
from jax.experimental.pallas import tpu as pltpu

F32 = jnp.float32
BF16 = jnp.bfloat16
I32 = jnp.int32

HEAD_DIM = 128
N_ATTN_HEADS = 8
ATTN_WIDTH = N_ATTN_HEADS * HEAD_DIM
CONV_WIDTH = 1024
PROJ_PARTS = 6
DILATED_PATTERNS = ((128, 1), (512, 4), (2048, 16))
ATTN_BLOCK = 128
N_MEM_HEADS = 4
N_EXPERTS = 64
TOP_K = 8
N_GROUPS = 8
GROUP_SIZE = N_EXPERTS // N_GROUPS
TOPK_GROUPS = 4
ROUTED_SCALE = 2.5
LN_EPS = 1e-5
RMS_EPS = 1e-6
DEPTH = 1
ALPHA = (2.0 * DEPTH) ** 0.25

LANES = 128
SUBLANES = 8
VMEM_LIMIT = 56 * 1024 * 1024
ATTN_OUT_COLS = ATTN_WIDTH + LANES
EXPERT_ROWS = 256
ZERO_GROUPS = EXPERT_ROWS // SUBLANES + 1
TM_PROJ, TN_PROJ = 1024, 512
TM_MIX = 256
TM_XATTN = 256
TM_ROUTER = 512
TM_DISPATCH = 256
TM_COMBINE = 128


def _params(*semantics):
    return pltpu.CompilerParams(dimension_semantics=semantics, vmem_limit_bytes=VMEM_LIMIT)


def _layer_norm(r, g, b):
    mu = jnp.mean(r, axis=-1, keepdims=True)
    c = r - mu
    var = jnp.mean(c * c, axis=-1, keepdims=True)
    return c * lax.rsqrt(var + LN_EPS) * g + b


def _rms_norm(v, g):
    return v * lax.rsqrt(jnp.mean(v * v, axis=-1, keepdims=True) + RMS_EPS) * g


def _dot(a, b):
    return jnp.dot(a, b, preferred_element_type=F32)


def _dot_nt(a, b):
    return lax.dot_general(a, b, (((1,), (1,)), ((), ())), preferred_element_type=F32)


def _silu(v):
    return v / (1.0 + jnp.exp(-v))


def _matmul_kernel(a_ref, w_ref, o_ref):
    o_ref[...] = _dot(a_ref[...], w_ref[...]).astype(o_ref.dtype)


def _matmul(a, w, out_dtype, tm, tn):
    m, k = a.shape
    n = w.shape[1]
    return pl.pallas_call(
        _matmul_kernel,
        grid=(m // tm, n // tn),
        in_specs=[pl.BlockSpec((tm, k), lambda i, j: (i, 0)),
                  pl.BlockSpec((k, tn), lambda i, j: (0, j))],
        out_specs=pl.BlockSpec((tm, tn), lambda i, j: (i, j)),
        out_shape=jax.ShapeDtypeStruct((m, n), out_dtype),
        compiler_params=_params("parallel", "arbitrary"),
        name="matmul",
    )(a, w)


def _attn_kernel(q_ref, kp_ref, kc_ref, vp_ref, vc_ref, o_ref):
    n = pl.program_id(2)
    blk = ATTN_BLOCK
    row = lax.broadcasted_iota(I32, (blk, 2 * blk), 0)
    col = lax.broadcasted_iota(I32, (blk, 2 * blk), 1)
    dist = row + blk - col
    valid = (dist >= 0) & (dist <= blk) & ((col >= blk) | (n > 0))
    lane = lax.broadcasted_iota(I32, (blk, LANES), 1)
    lse_tile = jnp.zeros((blk, LANES), F32)
    scale = HEAD_DIM ** -0.5
    for h in range(N_ATTN_HEADS):
        sl = slice(h * HEAD_DIM, (h + 1) * HEAD_DIM)
        q = q_ref[:, sl].astype(BF16)
        k = jnp.concatenate([kp_ref[:, sl], kc_ref[:, sl]], axis=0).astype(BF16)
        v = jnp.concatenate([vp_ref[:, sl], vc_ref[:, sl]], axis=0).astype(BF16)
        s = _dot_nt(q, k) * scale
        s = jnp.where(valid, s, -jnp.inf)
        m = jnp.max(s, axis=-1, keepdims=True)
        p = jnp.exp(s - m)
        l = jnp.sum(p, axis=-1, keepdims=True)
        o_ref[:, sl] = _dot((p / l).astype(BF16), v)
        lse_tile = jnp.where(lane == h, m + jnp.log(l), lse_tile)
    o_ref[:, ATTN_WIDTH:] = lse_tile


def _attn_branch(proj, batch, seq, dilation):
    n_tok, cols = proj.shape
    sub_len = seq // dilation
    nb = sub_len // ATTN_BLOCK
    view = proj.reshape(n_tok // dilation, dilation * cols)
    w = ATTN_WIDTH

    def spec(part, prev):
        def index(b, r, n):
            blk = jnp.maximum(n - 1, 0) if prev else n
            return (b * nb + blk, r * PROJ_PARTS + part)
        return pl.BlockSpec((ATTN_BLOCK, w), index)

    out = pl.pallas_call(
        _attn_kernel,
        grid=(batch, dilation, nb),
        in_specs=[spec(0, False), spec(1, True), spec(1, False), spec(2, True), spec(2, False)],
        out_specs=pl.BlockSpec((ATTN_BLOCK, ATTN_OUT_COLS), lambda b, r, n: (b * nb + n, r)),
        out_shape=jax.ShapeDtypeStruct((n_tok // dilation, dilation * ATTN_OUT_COLS), F32),
        compiler_params=_params("parallel", "parallel", "arbitrary"),
        name=f"dilated_attn_d{dilation}",
    )(view, view, view, view, view)
    return out.reshape(n_tok, ATTN_OUT_COLS)


def _mix_out_kernel(a1_ref, a2_ref, a3_ref, b_ref, c_ref, h_ref, cp_ref, hp_ref, x_ref,
                    w_ref, cw_ref, ga_ref, gc_ref, lg_ref, lb_ref, o_ref, *, tiles_per_seq):
    i = pl.program_id(0)
    tm = x_ref.shape[0]
    branches = (a1_ref, a2_ref, a3_ref)
    lses = [a[:, ATTN_WIDTH:] for a in branches]
    top = jnp.maximum(jnp.maximum(lses[0], lses[1]), lses[2])
    es = [jnp.exp(v - top) for v in lses]
    den = es[0] + es[1] + es[2]
    wts = [e / den for e in es]
    heads = []
    for h in range(N_ATTN_HEADS):
        sl = slice(h * HEAD_DIM, (h + 1) * HEAD_DIM)
        acc = wts[0][:, h:h + 1] * a1_ref[:, sl]
        acc += wts[1][:, h:h + 1] * a2_ref[:, sl]
        acc += wts[2][:, h:h + 1] * a3_ref[:, sl]
        heads.append(acc)
    attn = jnp.concatenate(heads, axis=1)

    z = c_ref[...] * h_ref[...]
    zp = jnp.where(i % tiles_per_seq == 0, 0.0, cp_ref[...] * hp_ref[...])
    zz = jnp.concatenate([zp, z], axis=0)
    z1 = pltpu.roll(zz, 1, axis=0)[SUBLANES:]
    z2 = pltpu.roll(zz, 2, axis=0)[SUBLANES:]
    conv = b_ref[...] * (cw_ref[0:1, :] * z2 + cw_ref[1:2, :] * z1 + cw_ref[2:3, :] * z)

    mixed = jnp.concatenate([_rms_norm(attn, ga_ref[...]), _rms_norm(conv, gc_ref[...])], axis=1)
    y = _dot(mixed.astype(BF16), w_ref[...])
    o_ref[...] = _layer_norm(ALPHA * x_ref[...] + y, lg_ref[...], lb_ref[...])


def _mix_out(a1, a2, a3, proj, x, w_out, conv_w, g_attn, g_conv, ln_g, ln_b, seq):
    n_tok, d = x.shape
    tm = TM_MIX
    rows8 = tm // SUBLANES
    row = lambda i: (i, 0)
    const = lambda i: (0, 0)
    prev = lambda part: (lambda i: (jnp.maximum(i * rows8 - 1, 0), part))
    return pl.pallas_call(
        functools.partial(_mix_out_kernel, tiles_per_seq=seq // tm),
        grid=(n_tok // tm,),
        in_specs=[pl.BlockSpec((tm, ATTN_OUT_COLS), row)] * 3 + [
            pl.BlockSpec((tm, CONV_WIDTH), lambda i: (i, 3)),
            pl.BlockSpec((tm, CONV_WIDTH), lambda i: (i, 4)),
            pl.BlockSpec((tm, CONV_WIDTH), lambda i: (i, 5)),
            pl.BlockSpec((SUBLANES, CONV_WIDTH), prev(4)),
            pl.BlockSpec((SUBLANES, CONV_WIDTH), prev(5)),
            pl.BlockSpec((tm, d), row),
            pl.BlockSpec(w_out.shape, const),
            pl.BlockSpec(conv_w.shape, const),
            pl.BlockSpec(g_attn.shape, const),
            pl.BlockSpec(g_conv.shape, const),
            pl.BlockSpec(ln_g.shape, const),
            pl.BlockSpec(ln_b.shape, const)],
        out_specs=pl.BlockSpec((tm, d), row),
        out_shape=jax.ShapeDtypeStruct((n_tok, d), F32),
        compiler_params=_params("parallel"),
        name="mix_out_ln1",
    )(a1, a2, a3, proj, proj, proj, proj, proj, x, w_out, conv_w, g_attn, g_conv, ln_g, ln_b)


def _xattn_kernel(x_ref, wq_ref, k_ref, v_ref, wo_ref, lg_ref, lb_ref, o_ref):
    x = x_ref[...]
    d = x.shape[1]
    hd = d // N_MEM_HEADS
    q = _dot(x.astype(BF16), wq_ref[...]).astype(BF16)
    scale = hd ** -0.5
    outs = []
    for h in range(N_MEM_HEADS):
        sl = slice(h * hd, (h + 1) * hd)
        s = _dot_nt(q[:, sl], k_ref[:, sl]) * scale
        m = jnp.max(s, axis=-1, keepdims=True)
        p = jnp.exp(s - m)
        p = p / jnp.sum(p, axis=-1, keepdims=True)
        outs.append(_dot(p.astype(BF16), v_ref[:, sl]).astype(BF16))
    y = _dot(jnp.concatenate(outs, axis=1), wo_ref[...])
    o_ref[...] = _layer_norm(ALPHA * x + y, lg_ref[...], lb_ref[...])


def _xattn(x, w_q, kv, w_o, ln_g, ln_b, seq, mem_len):
    n_tok, d = x.shape
    tm = TM_XATTN
    row = lambda i: (i, 0)
    const = lambda i: (0, 0)
    tiles_per_seq = seq // tm
    return pl.pallas_call(
        _xattn_kernel,
        grid=(n_tok // tm,),
        in_specs=[pl.BlockSpec((tm, d), row),
                  pl.BlockSpec(w_q.shape, const),
                  pl.BlockSpec((mem_len, d), lambda i: (i // tiles_per_seq, 0)),
                  pl.BlockSpec((mem_len, d), lambda i: (i // tiles_per_seq, 1)),
                  pl.BlockSpec(w_o.shape, const),
                  pl.BlockSpec(ln_g.shape, const),
                  pl.BlockSpec(ln_b.shape, const)],
        out_specs=pl.BlockSpec((tm, d), row),
        out_shape=jax.ShapeDtypeStruct((n_tok, d), F32),
        compiler_params=_params("parallel"),
        name="xattn_ln2",
    )(x, w_q, kv, kv, w_o, ln_g, ln_b)


def _first_argmax(vals, index, sentinel):
    m = jnp.max(vals, axis=0, keepdims=True)
    i = jnp.min(jnp.where(vals == m, index, sentinel), axis=0, keepdims=True)
    return m, i


def _router_kernel(x_ref, wt_ref, bias_ref, idx_ref, gate_ref, rank_ref, cnt_ref):
    step = pl.program_id(0)
    tm = x_ref.shape[0]

    @pl.when(step == 0)
    def _():
        cnt_ref[...] = jnp.zeros_like(cnt_ref)

    logits = lax.dot_general(wt_ref[...], x_ref[...], (((1,), (1,)), ((), ())),
                             precision=lax.Precision.HIGHEST, preferred_element_type=F32)
    scores = 1.0 / (1.0 + jnp.exp(-logits))
    choice = scores + bias_ref[...]

    sub = lax.broadcasted_iota(I32, (GROUP_SIZE, tm), 0)
    group_scores = []
    for g in range(N_GROUPS):
        c = choice[g * GROUP_SIZE:(g + 1) * GROUP_SIZE, :]
        m1, i1 = _first_argmax(c, sub, GROUP_SIZE)
        m2 = jnp.max(jnp.where(sub == i1, -jnp.inf, c), axis=0, keepdims=True)
        group_scores.append(m1 + m2)
    gs = jnp.concatenate(group_scores, axis=0)
    gidx = lax.broadcasted_iota(I32, (N_GROUPS, tm), 0)
    gsel = jnp.zeros((N_GROUPS, tm), F32)
    for _ in range(TOPK_GROUPS):
        _, gi = _first_argmax(gs, gidx, N_GROUPS)
        hit = gidx == gi
        gsel = jnp.where(hit, 1.0, gsel)
        gs = jnp.where(hit, -jnp.inf, gs)
    masked = jnp.concatenate(
        [jnp.where(gsel[g:g + 1, :] > 0.0, choice[g * GROUP_SIZE:(g + 1) * GROUP_SIZE, :], -jnp.inf)
         for g in range(N_GROUPS)], axis=0)

    eidx = lax.broadcasted_iota(I32, (N_EXPERTS, tm), 0)
    hits, idxs, ws = [], [], []
    for _ in range(TOP_K):
        _, ei = _first_argmax(masked, eidx, N_EXPERTS)
        hit = eidx == ei
        hits.append(hit)
        idxs.append(ei)
        ws.append(jnp.sum(jnp.where(hit, scores, 0.0), axis=0, keepdims=True))
        masked = jnp.where(hit, -jnp.inf, masked)
    wsum = ws[0]
    for w in ws[1:]:
        wsum = wsum + w
    gate_ref[...] = jnp.concatenate([w / wsum * ROUTED_SCALE for w in ws], axis=0)
    idx_ref[...] = jnp.concatenate(idxs, axis=0)

    sel = hits[0]
    for hit in hits[1:]:
        sel = sel | hit
    self32 = jnp.where(sel, 1.0, 0.0)
    t_row = lax.broadcasted_iota(I32, (tm, tm), 0)
    t_col = lax.broadcasted_iota(I32, (tm, tm), 1)
    tri = jnp.where(t_row < t_col, 1.0, 0.0).astype(BF16)
    before = _dot(self32.astype(BF16), tri) + cnt_ref[:, 0:1]
    rank_ref[...] = jnp.concatenate(
        [jnp.sum(jnp.where(hit, before, 0.0), axis=0, keepdims=True) for hit in hits],
        axis=0).astype(I32)
    cnt_ref[...] += jnp.sum(self32, axis=1, keepdims=True)


def _router(x, w_router_t, bias_col):
    n_tok, d = x.shape
    tm = TM_ROUTER
    tok = lambda i: (0, i)
    return pl.pallas_call(
        _router_kernel,
        grid=(n_tok // tm,),
        in_specs=[pl.BlockSpec((tm, d), lambda i: (i, 0)),
                  pl.BlockSpec(w_router_t.shape, lambda i: (0, 0)),
                  pl.BlockSpec(bias_col.shape, lambda i: (0, 0))],
        out_specs=[pl.BlockSpec((TOP_K, tm), tok), pl.BlockSpec((TOP_K, tm), tok),
                   pl.BlockSpec((TOP_K, tm), tok),
                   pl.BlockSpec((N_EXPERTS, LANES), lambda i: (0, 0))],
        out_shape=[jax.ShapeDtypeStruct((TOP_K, n_tok), I32),
                   jax.ShapeDtypeStruct((TOP_K, n_tok), F32),
                   jax.ShapeDtypeStruct((TOP_K, n_tok), I32),
                   jax.ShapeDtypeStruct((N_EXPERTS, LANES), F32)],
        compiler_params=_params("arbitrary"),
        name="router",
    )(x, w_router_t, bias_col)


def _tile_view(x):
    r, d = x.shape
    return (x.reshape(r // SUBLANES, SUBLANES, d // LANES, LANES).transpose(0, 2, 1, 3)
            .reshape(r // SUBLANES, d // LANES, SUBLANES, 1, LANES))


def _row_view(v):
    g, c, s, _, l = v.shape
    return v.reshape(g, c, s, l).transpose(0, 2, 1, 3).reshape(g * s, c * l)


def _tile_row(view_ref, row):
    return view_ref.at[row >> 3, :, row & (SUBLANES - 1)]


def _dispatch_kernel(pad_ref, dest_ref, x_ref, xs_ref, zero_ref, sem):
    step = pl.program_id(0)
    groups = x_ref.shape[0]

    @pl.when(step == 0)
    def _():
        zero_ref[...] = jnp.zeros_like(zero_ref)

        def fill(e):
            return pltpu.make_async_copy(zero_ref, xs_ref.at[pl.ds(pad_ref[e] >> 3, ZERO_GROUPS)], sem)

        def start(e, c):
            fill(e).start()
            return c

        def wait(e, c):
            fill(e).wait()
            return c

        lax.fori_loop(0, N_EXPERTS, start, 0)
        lax.fori_loop(0, N_EXPERTS, wait, 0)

    def copies(g):
        return [pltpu.make_async_copy(x_ref.at[g, :, s], _tile_row(xs_ref, dest_ref[k, g * SUBLANES + s]), sem)
                for s in range(SUBLANES) for k in range(TOP_K)]

    def start_rows(g, c):
        for cp in copies(g):
            cp.start()
        return c

    def wait_rows(g, c):
        for cp in copies(g):
            cp.wait()
        return c

    lax.fori_loop(0, groups, start_rows, 0)
    lax.fori_loop(0, groups, wait_rows, 0)


def _dispatch(pad_start, dest, x, n_rows):
    n_tok, w = x.shape
    tm = TM_DISPATCH
    chunks = w // LANES
    block = (tm // SUBLANES, chunks, SUBLANES, 1, LANES)
    total_groups = n_rows // SUBLANES + ZERO_GROUPS
    out = pl.pallas_call(
        _dispatch_kernel,
        grid_spec=pltpu.PrefetchScalarGridSpec(
            num_scalar_prefetch=1,
            grid=(n_tok // tm,),
            in_specs=[pl.BlockSpec((TOP_K, tm), lambda i, pad: (0, i), memory_space=pltpu.SMEM),
                      pl.BlockSpec(block, lambda i, pad: (i, 0, 0, 0, 0))],
            out_specs=pl.BlockSpec(memory_space=pl.ANY),
            scratch_shapes=[pltpu.VMEM((ZERO_GROUPS, chunks, SUBLANES, 1, LANES), F32),
                            pltpu.SemaphoreType.DMA(())]),
        out_shape=jax.ShapeDtypeStruct((total_groups, chunks, SUBLANES, 1, LANES), F32),
        compiler_params=_params("arbitrary"),
        name="dispatch",
    )(pad_start, dest, _tile_view(x))
    return _row_view(out)


def _expert_kernel(bexp_ref, nused_ref, xs_ref, wg_ref, wu_ref, wd_ref, ys_ref, wg_s, wu_s, wd_s):
    i = pl.program_id(0)

    @pl.when(i < nused_ref[0])
    def _():
        e = bexp_ref[i]
        e_prev = bexp_ref[jnp.maximum(i - 1, 0)]

        @pl.when((i == 0) | (e != e_prev))
        def _():
            wg_s[...] = wg_ref[...].astype(BF16)
            wu_s[...] = wu_ref[...].astype(BF16)
            wd_s[...] = wd_ref[...].astype(BF16)

        xb = xs_ref[...].astype(BF16)
        hmid = (_silu(_dot(xb, wg_s[...])) * _dot(xb, wu_s[...])).astype(BF16)
        ys_ref[...] = _dot(hmid, wd_s[...])


def _experts(block_exp, n_used, xs, w_gate, w_up, w_down, n_blocks):
    _, d, ff = w_gate.shape
    rows = lambda i, bexp, nused: (jnp.minimum(i, nused[0] - 1), 0)
    wsel = lambda i, bexp, nused: (bexp[jnp.minimum(i, nused[0] - 1)], 0, 0)
    return pl.pallas_call(
        _expert_kernel,
        grid_spec=pltpu.PrefetchScalarGridSpec(
            num_scalar_prefetch=2,
            grid=(n_blocks,),
            in_specs=[pl.BlockSpec((EXPERT_ROWS, d), rows),
                      pl.BlockSpec((None, d, ff), wsel),
                      pl.BlockSpec((None, d, ff), wsel),
                      pl.BlockSpec((None, ff, d), wsel)],
            out_specs=pl.BlockSpec((EXPERT_ROWS, d), rows),
            scratch_shapes=[pltpu.VMEM((d, ff), BF16), pltpu.VMEM((d, ff), BF16),
                            pltpu.VMEM((ff, d), BF16)]),
        out_shape=jax.ShapeDtypeStruct((n_blocks * EXPERT_ROWS, d), F32),
        compiler_params=_params("arbitrary"),
        name="experts",
    )(block_exp, n_used, xs, w_gate, w_up, w_down)


def _combine_kernel(dest_ref, gate_ref, x_ref, ys_ref, wsg_ref, wsu_ref, wsd_ref, lg_ref, lb_ref,
                    o_ref, buf, sem):
    tm, d = x_ref.shape
    groups, chunks = tm // SUBLANES, d // LANES
    dma_views = [buf.at[k].reshape(groups, chunks, SUBLANES, 1, LANES) for k in range(TOP_K)]
    load_views = [buf.at[k].reshape(groups, chunks, SUBLANES, LANES) for k in range(TOP_K)]

    def copies(g):
        return [pltpu.make_async_copy(_tile_row(ys_ref, dest_ref[k, g * SUBLANES + s]),
                                      dma_views[k].at[g, :, s], sem)
                for s in range(SUBLANES) for k in range(TOP_K)]

    def start_rows(g, c):
        for cp in copies(g):
            cp.start()
        return c

    def wait_rows(g, c):
        for cp in copies(g):
            cp.wait()
        return c

    lax.fori_loop(0, groups, start_rows, 0)

    x = x_ref[...]
    xb = x.astype(BF16)
    hmid = (_silu(_dot(xb, wsg_ref[...])) * _dot(xb, wsu_ref[...])).astype(BF16)
    acc = ALPHA * x + _dot(hmid, wsd_ref[...])

    lax.fori_loop(0, groups, wait_rows, 0)
    gates = gate_ref[...]

    def rows_of(k):
        return jnp.concatenate([load_views[k][:, c].reshape(tm, LANES) for c in range(chunks)], axis=1)

    routed = rows_of(0) * gates[:, 0:1]
    for k in range(1, TOP_K):
        routed += rows_of(k) * gates[:, k:k + 1]
    o_ref[...] = _layer_norm(acc + routed, lg_ref[...], lb_ref[...])


def _combine(dest, gates_t, x, ys, ws_gate, ws_up, ws_down, ln_g, ln_b):
    n_tok, d = x.shape
    tm = TM_COMBINE
    row = lambda i: (i, 0)
    const = lambda i: (0, 0)
    return pl.pallas_call(
        _combine_kernel,
        grid=(n_tok // tm,),
        in_specs=[pl.BlockSpec((TOP_K, tm), lambda i: (0, i), memory_space=pltpu.SMEM),
                  pl.BlockSpec((tm, TOP_K), row),
                  pl.BlockSpec((tm, d), row),
                  pl.BlockSpec(memory_space=pl.ANY),
                  pl.BlockSpec(ws_gate.shape, const),
                  pl.BlockSpec(ws_up.shape, const),
                  pl.BlockSpec(ws_down.shape, const),
                  pl.BlockSpec(ln_g.shape, const),
                  pl.BlockSpec(ln_b.shape, const)],
        out_specs=pl.BlockSpec((tm, d), row),
        out_shape=jax.ShapeDtypeStruct((n_tok, d), F32),
        scratch_shapes=[pltpu.VMEM((TOP_K, tm * (d // LANES), LANES), F32), pltpu.SemaphoreType.DMA(())],
        compiler_params=_params("arbitrary"),
        name="combine_shared_ln3",
    )(dest, gates_t, x, _tile_view(ys), ws_gate, ws_up, ws_down, ln_g, ln_b)


def _layer(x, mem, w_in, conv_w, g_attn_out, g_conv_out, w_out, ln1_g, ln1_b, w_q_mem, w_kv_mem,
           w_o_mem, ln2_g, ln2_b, w_router, router_bias, w_gate, w_up, w_down, ws_gate, ws_up,
           ws_down, ln3_g, ln3_b):
    batch, seq, d = x.shape
    mem_len = mem.shape[1]
    n_tok = batch * seq
    xf = x.reshape(n_tok, d)
    row = lambda v: v.reshape(1, -1)

    proj = _matmul(xf.astype(BF16), w_in.astype(BF16), F32, TM_PROJ, TN_PROJ)
    branches = [_attn_branch(proj, batch, seq, dil) for _, dil in DILATED_PATTERNS]
    x1 = _mix_out(*branches, proj, xf, w_out.astype(BF16), conv_w, row(g_attn_out), row(g_conv_out),
                  row(ln1_g), row(ln1_b), seq)

    kv = _matmul(mem.reshape(batch * mem_len, d).astype(BF16), w_kv_mem.astype(BF16), BF16,
                 mem_len, 1024)
    x2 = _xattn(x1, w_q_mem.astype(BF16), kv, w_o_mem.astype(BF16), row(ln2_g), row(ln2_b), seq,
                mem_len)

    idx, gates, rank, cnt = _router(x2, w_router.T, router_bias.reshape(N_EXPERTS, 1))
    counts = cnt[:, 0].astype(I32)
    padded = (counts + EXPERT_ROWS - 1) // EXPERT_ROWS * EXPERT_ROWS
    seg_end = jnp.cumsum(padded)
    seg_start = seg_end - padded
    experts = jnp.arange(N_EXPERTS, dtype=I32)
    dest = rank + jnp.sum(jnp.where(idx[..., None] == experts, seg_start, 0), axis=-1)
    n_blocks = -(-(n_tok * TOP_K) // EXPERT_ROWS) + N_EXPERTS
    block_first = jnp.arange(n_blocks, dtype=I32) * EXPERT_ROWS
    block_exp = jnp.minimum(jnp.sum(block_first[:, None] >= seg_end[None, :], axis=1),
                            N_EXPERTS - 1).astype(I32)
    n_used = (seg_end[-1:] // EXPERT_ROWS).astype(I32)

    xs = _dispatch(seg_start + counts, dest, x2, n_blocks * EXPERT_ROWS)
    ys = _experts(block_exp, n_used, xs, w_gate, w_up, w_down, n_blocks)
    out = _combine(dest, gates.T, x2, ys, ws_gate.astype(BF16), ws_up.astype(BF16),
                   ws_down.astype(BF16), row(ln3_g), row(ln3_b))
    return out.reshape(batch, seq, d)


def kernel(x, mem, w_in, conv_w, g_attn_out, g_conv_out, w_out, ln1_g, ln1_b, w_q_mem, w_kv_mem, w_o_mem, ln2_g, ln2_b, w_router, router_bias, w_gate, w_up, w_down, ws_gate, ws_up, ws_down, ln3_g, ln3_b):
    for l in range(DEPTH):
        x = _layer(x, mem, w_in[l], conv_w[l], g_attn_out[l], g_conv_out[l], w_out[l], ln1_g[l],
                   ln1_b[l], w_q_mem[l], w_kv_mem[l], w_o_mem[l], ln2_g[l], ln2_b[l], w_router[l],
                   router_bias[l], w_gate[l], w_up[l], w_down[l], ws_gate[l], ws_up[l], ws_down[l],
                   ln3_g[l], ln3_b[l])
    return x
```

```python
import functools

import jax
import jax.numpy as jnp
from jax import lax
from jax.experimental import pallas as pl
from jax.experimental.pallas import tpu as pltpu

F32 = jnp.float32
BF16 = jnp.bfloat16
I32 = jnp.int32

HEAD_DIM = 128
N_ATTN_HEADS = 8
ATTN_WIDTH = N_ATTN_HEADS * HEAD_DIM
CONV_WIDTH = 1024
PROJ_PARTS = 6
DILATED_PATTERNS = ((128, 1), (512, 4), (2048, 16))
ATTN_BLOCK = 128
ATTN_SPAN = 16 * ATTN_BLOCK
N_MEM_HEADS = 4
N_EXPERTS = 64
TOP_K = 8
N_GROUPS = 8
GROUP_SIZE = N_EXPERTS // N_GROUPS
TOPK_GROUPS = 4
ROUTED_SCALE = 2.5
LN_EPS = 1e-5
RMS_EPS = 1e-6
DEPTH = 1
ALPHA = (2.0 * DEPTH) ** 0.25

LANES = 128
SUBLANES = 8
VMEM_LIMIT = 56 * 1024 * 1024
EXPERT_ROWS = 256
ZERO_GROUPS = EXPERT_ROWS // SUBLANES + 1
TM_PROJ, TN_PROJ = 1024, 512
TM_MIX = 256
TM_XATTN = 256
TM_ROUTER = 512
TM_DISPATCH = 256
TM_COMBINE = 128


def _params(*semantics):
    return pltpu.CompilerParams(dimension_semantics=semantics, vmem_limit_bytes=VMEM_LIMIT)


def _layer_norm(r, g, b):
    mu = jnp.mean(r, axis=-1, keepdims=True)
    c = r - mu
    var = jnp.mean(c * c, axis=-1, keepdims=True)
    return c * lax.rsqrt(var + LN_EPS) * g + b


def _rms_norm(v, g):
    return v * lax.rsqrt(jnp.mean(v * v, axis=-1, keepdims=True) + RMS_EPS) * g


def _dot(a, b):
    return jnp.dot(a, b, preferred_element_type=F32)


def _dot_nt(a, b):
    return lax.dot_general(a, b, (((1,), (1,)), ((), ())), preferred_element_type=F32)


def _silu(v):
    return v / (1.0 + jnp.exp(-v))


def _matmul_kernel(a_ref, w_ref, o_ref):
    o_ref[...] = _dot(a_ref[...], w_ref[...]).astype(o_ref.dtype)


def _matmul(a, w, out_dtype, tm, tn):
    m, k = a.shape
    n = w.shape[1]
    return pl.pallas_call(
        _matmul_kernel,
        grid=(m // tm, n // tn),
        in_specs=[pl.BlockSpec((tm, k), lambda i, j: (i, 0)),
                  pl.BlockSpec((k, tn), lambda i, j: (0, j))],
        out_specs=pl.BlockSpec((tm, tn), lambda i, j: (i, j)),
        out_shape=jax.ShapeDtypeStruct((m, n), out_dtype),
        compiler_params=_params("parallel", "arbitrary"),
        name="matmul",
    )(a, w)


def _attn_kernel(q_ref, kp_ref, kc_ref, vp_ref, vc_ref, o_ref, kk, vv, o_acc, l_acc):
    j = pl.program_id(1)
    blk, span = ATTN_BLOCK, ATTN_SPAN
    kk[0:span, :] = kp_ref[...]
    kk[span:, :] = kc_ref[...]
    vv[0:span, :] = vp_ref[...]
    vv[span:, :] = vc_ref[...]

    row = lax.broadcasted_iota(I32, (blk, 2 * blk), 0)
    col = lax.broadcasted_iota(I32, (blk, 2 * blk), 1)
    dist = row + blk - col
    in_window = (dist >= 0) & (dist <= blk)
    own_block = col >= blk
    scale = HEAD_DIM ** -0.5

    order = sorted(range(len(DILATED_PATTERNS)), key=lambda i: -DILATED_PATTERNS[i][1])
    assert DILATED_PATTERNS[order[-1]][1] == 1
    sub_blocks = span // blk
    for bi in order:
        dil = DILATED_PATTERNS[bi][1]
        shift = dil.bit_length() - 1

        def body(t, carry, bi=bi, dil=dil, shift=shift):
            r = t & (dil - 1)
            n = t >> shift
            base = n * (blk * dil) + r
            if dil == 1:
                base = pl.multiple_of(base, blk)
                q_rows, kv_rows = pl.ds(base, blk), pl.ds(span + base - blk, 2 * blk)
            else:
                q_rows = pl.ds(base, blk, stride=dil)
                kv_rows = pl.ds(span + base - blk * dil, 2 * blk, stride=dil)
            q = q_ref[q_rows, :].astype(BF16)
            k = kk[kv_rows, :].astype(BF16)
            v = vv[kv_rows, :].astype(BF16)
            s = _dot_nt(q, k) * scale
            valid = in_window & (own_block | (j > 0) | (n > 0))
            s = jnp.where(valid, s, -jnp.inf)
            m = jnp.max(s, axis=-1, keepdims=True)
            p = jnp.exp(s - m)
            l = jnp.sum(p, axis=-1, keepdims=True)
            o = _dot((p / l).astype(BF16), v)
            lse = jnp.broadcast_to(m + jnp.log(l), (blk, LANES))
            if dil != 1:
                o_acc[bi, q_rows, :] = o
                l_acc[bi, q_rows, :] = lse
                return carry
            others = [i for i in order if i != bi]
            lses = [lse] + [l_acc[i, q_rows, :] for i in others]
            outs = [o] + [o_acc[i, q_rows, :] for i in others]
            top = jnp.maximum(jnp.maximum(lses[0], lses[1]), lses[2])
            es = [jnp.exp(v_ - top) for v_ in lses]
            den = es[0] + es[1] + es[2]
            o_ref[q_rows, :] = ((es[0] / den) * outs[0] + (es[1] / den) * outs[1] + (es[2] / den) * outs[2])
            return carry

        lax.fori_loop(0, sub_blocks, body, 0, unroll=8)


def _dilated_attention(proj, batch, seq):
    n_tok, _ = proj.shape
    span = ATTN_SPAN
    spans = seq // span
    heads = N_ATTN_HEADS

    def spec(part, prev):
        def index(b, j, h):
            return (b * spans + (jnp.maximum(j - 1, 0) if prev else j), part * heads + h)
        return pl.BlockSpec((span, HEAD_DIM), index)

    return pl.pallas_call(
        _attn_kernel,
        grid=(batch, spans, heads),
        in_specs=[spec(0, False), spec(1, True), spec(1, False), spec(2, True), spec(2, False)],
        out_specs=pl.BlockSpec((span, HEAD_DIM), lambda b, j, h: (b * spans + j, h)),
        out_shape=jax.ShapeDtypeStruct((n_tok, ATTN_WIDTH), F32),
        scratch_shapes=[pltpu.VMEM((2 * span, HEAD_DIM), F32), pltpu.VMEM((2 * span, HEAD_DIM), F32),
                        pltpu.VMEM((len(DILATED_PATTERNS), span, HEAD_DIM), F32),
                        pltpu.VMEM((len(DILATED_PATTERNS), span, HEAD_DIM), F32)],
        compiler_params=_params("parallel", "parallel", "parallel"),
        name="dilated_attn",
    )(proj, proj, proj, proj, proj)


def _mix_out_kernel(a_ref, b_ref, c_ref, h_ref, cp_ref, hp_ref, x_ref,
                    w_ref, cw_ref, ga_ref, gc_ref, lg_ref, lb_ref, o_ref, *, tiles_per_seq):
    i = pl.program_id(0)
    z = c_ref[...] * h_ref[...]
    zp = jnp.where(i % tiles_per_seq == 0, 0.0, cp_ref[...] * hp_ref[...])
    zz = jnp.concatenate([zp, z], axis=0)
    z1 = pltpu.roll(zz, 1, axis=0)[SUBLANES:]
    z2 = pltpu.roll(zz, 2, axis=0)[SUBLANES:]
    conv = b_ref[...] * (cw_ref[0:1, :] * z2 + cw_ref[1:2, :] * z1 + cw_ref[2:3, :] * z)

    mixed = jnp.concatenate([_rms_norm(a_ref[...], ga_ref[...]), _rms_norm(conv, gc_ref[...])], axis=1)
    y = _dot(mixed.astype(BF16), w_ref[...])
    o_ref[...] = _layer_norm(ALPHA * x_ref[...] + y, lg_ref[...], lb_ref[...])


def _mix_out(attn, proj, x, w_out, conv_w, g_attn, g_conv, ln_g, ln_b, seq):
    n_tok, d = x.shape
    tm = TM_MIX
    rows8 = tm // SUBLANES
    row = lambda i: (i, 0)
    const = lambda i: (0, 0)
    prev = lambda part: (lambda i: (jnp.maximum(i * rows8 - 1, 0), part))
    return pl.pallas_call(
        functools.partial(_mix_out_kernel, tiles_per_seq=seq // tm),
        grid=(n_tok // tm,),
        in_specs=[
            pl.BlockSpec((tm, ATTN_WIDTH), row),
            pl.BlockSpec((tm, CONV_WIDTH), lambda i: (i, 3)),
            pl.BlockSpec((tm, CONV_WIDTH), lambda i: (i, 4)),
            pl.BlockSpec((tm, CONV_WIDTH), lambda i: (i, 5)),
            pl.BlockSpec((SUBLANES, CONV_WIDTH), prev(4)),
            pl.BlockSpec((SUBLANES, CONV_WIDTH), prev(5)),
            pl.BlockSpec((tm, d), row),
            pl.BlockSpec(w_out.shape, const),
            pl.BlockSpec(conv_w.shape, const),
            pl.BlockSpec(g_attn.shape, const),
            pl.BlockSpec(g_conv.shape, const),
            pl.BlockSpec(ln_g.shape, const),
            pl.BlockSpec(ln_b.shape, const)],
        out_specs=pl.BlockSpec((tm, d), row),
        out_shape=jax.ShapeDtypeStruct((n_tok, d), F32),
        compiler_params=_params("parallel"),
        name="mix_out_ln1",
    )(attn, proj, proj, proj, proj, proj, x, w_out, conv_w, g_attn, g_conv, ln_g, ln_b)


def _xattn_kernel(x_ref, wq_ref, k_ref, v_ref, wo_ref, lg_ref, lb_ref, o_ref):
    x = x_ref[...]
    d = x.shape[1]
    hd = d // N_MEM_HEADS
    q = _dot(x.astype(BF16), wq_ref[...]).astype(BF16)
    scale = hd ** -0.5
    outs = []
    for h in range(N_MEM_HEADS):
        sl = slice(h * hd, (h + 1) * hd)
        s = _dot_nt(q[:, sl], k_ref[:, sl]) * scale
        m = jnp.max(s, axis=-1, keepdims=True)
        p = jnp.exp(s - m)
        p = p / jnp.sum(p, axis=-1, keepdims=True)
        outs.append(_dot(p.astype(BF16), v_ref[:, sl]).astype(BF16))
    y = _dot(jnp.concatenate(outs, axis=1), wo_ref[...])
    o_ref[...] = _layer_norm(ALPHA * x + y, lg_ref[...], lb_ref[...])


def _xattn(x, w_q, kv, w_o, ln_g, ln_b, seq, mem_len):
    n_tok, d = x.shape
    tm = TM_XATTN
    row = lambda i: (i, 0)
    const = lambda i: (0, 0)
    tiles_per_seq = seq // tm
    return pl.pallas_call(
        _xattn_kernel,
        grid=(n_tok // tm,),
        in_specs=[pl.BlockSpec((tm, d), row),
                  pl.BlockSpec(w_q.shape, const),
                  pl.BlockSpec((mem_len, d), lambda i: (i // tiles_per_seq, 0)),
                  pl.BlockSpec((mem_len, d), lambda i: (i // tiles_per_seq, 1)),
                  pl.BlockSpec(w_o.shape, const),
                  pl.BlockSpec(ln_g.shape, const),
                  pl.BlockSpec(ln_b.shape, const)],
        out_specs=pl.BlockSpec((tm, d), row),
        out_shape=jax.ShapeDtypeStruct((n_tok, d), F32),
        compiler_params=_params("parallel"),
        name="xattn_ln2",
    )(x, w_q, kv, kv, w_o, ln_g, ln_b)


def _first_argmax(vals, index, sentinel):
    m = jnp.max(vals, axis=0, keepdims=True)
    i = jnp.min(jnp.where(vals == m, index, sentinel), axis=0, keepdims=True)
    return m, i


def _router_kernel(x_ref, wt_ref, bias_ref, idx_ref, gate_ref, rank_ref, cnt_ref):
    step = pl.program_id(0)
    tm = x_ref.shape[0]

    @pl.when(step == 0)
    def _():
        cnt_ref[...] = jnp.zeros_like(cnt_ref)

    logits = lax.dot_general(wt_ref[...], x_ref[...], (((1,), (1,)), ((), ())),
                             precision=lax.Precision.HIGHEST, preferred_element_type=F32)
    scores = 1.0 / (1.0 + jnp.exp(-logits))
    choice = scores + bias_ref[...]

    sub = lax.broadcasted_iota(I32, (GROUP_SIZE, tm), 0)
    group_scores = []
    for g in range(N_GROUPS):
        c = choice[g * GROUP_SIZE:(g + 1) * GROUP_SIZE, :]
        m1, i1 = _first_argmax(c, sub, GROUP_SIZE)
        m2 = jnp.max(jnp.where(sub == i1, -jnp.inf, c), axis=0, keepdims=True)
        group_scores.append(m1 + m2)
    gs = jnp.concatenate(group_scores, axis=0)
    gidx = lax.broadcasted_iota(I32, (N_GROUPS, tm), 0)
    gsel = jnp.zeros((N_GROUPS, tm), F32)
    for _ in range(TOPK_GROUPS):
        _, gi = _first_argmax(gs, gidx, N_GROUPS)
        hit = gidx == gi
        gsel = jnp.where(hit, 1.0, gsel)
        gs = jnp.where(hit, -jnp.inf, gs)
    masked = jnp.concatenate(
        [jnp.where(gsel[g:g + 1, :] > 0.0, choice[g * GROUP_SIZE:(g + 1) * GROUP_SIZE, :], -jnp.inf)
         for g in range(N_GROUPS)], axis=0)

    eidx = lax.broadcasted_iota(I32, (N_EXPERTS, tm), 0)
    hits, idxs, ws = [], [], []
    for _ in range(TOP_K):
        _, ei = _first_argmax(masked, eidx, N_EXPERTS)
        hit = eidx == ei
        hits.append(hit)
        idxs.append(ei)
        ws.append(jnp.sum(jnp.where(hit, scores, 0.0), axis=0, keepdims=True))
        masked = jnp.where(hit, -jnp.inf, masked)
    wsum = ws[0]
    for w in ws[1:]:
        wsum = wsum + w
    gate_ref[...] = jnp.concatenate([w / wsum * ROUTED_SCALE for w in ws], axis=0)
    idx_ref[...] = jnp.concatenate(idxs, axis=0)

    sel = hits[0]
    for hit in hits[1:]:
        sel = sel | hit
    self32 = jnp.where(sel, 1.0, 0.0)
    t_row = lax.broadcasted_iota(I32, (tm, tm), 0)
    t_col = lax.broadcasted_iota(I32, (tm, tm), 1)
    tri = jnp.where(t_row < t_col, 1.0, 0.0).astype(BF16)
    before = _dot(self32.astype(BF16), tri) + cnt_ref[:, 0:1]
    rank_ref[...] = jnp.concatenate(
        [jnp.sum(jnp.where(hit, before, 0.0), axis=0, keepdims=True) for hit in hits],
        axis=0).astype(I32)
    cnt_ref[...] += jnp.sum(self32, axis=1, keepdims=True)


def _router(x, w_router_t, bias_col):
    n_tok, d = x.shape
    tm = TM_ROUTER
    tok = lambda i: (0, i)
    return pl.pallas_call(
        _router_kernel,
        grid=(n_tok // tm,),
        in_specs=[pl.BlockSpec((tm, d), lambda i: (i, 0)),
                  pl.BlockSpec(w_router_t.shape, lambda i: (0, 0)),
                  pl.BlockSpec(bias_col.shape, lambda i: (0, 0))],
        out_specs=[pl.BlockSpec((TOP_K, tm), tok), pl.BlockSpec((TOP_K, tm), tok),
                   pl.BlockSpec((TOP_K, tm), tok),
                   pl.BlockSpec((N_EXPERTS, LANES), lambda i: (0, 0))],
        out_shape=[jax.ShapeDtypeStruct((TOP_K, n_tok), I32),
                   jax.ShapeDtypeStruct((TOP_K, n_tok), F32),
                   jax.ShapeDtypeStruct((TOP_K, n_tok), I32),
                   jax.ShapeDtypeStruct((N_EXPERTS, LANES), F32)],
        compiler_params=_params("arbitrary"),
        name="router",
    )(x, w_router_t, bias_col)


def _tile_view(x):
    r, d = x.shape
    return (x.reshape(r // SUBLANES, SUBLANES, d // LANES, LANES).transpose(0, 2, 1, 3)
            .reshape(r // SUBLANES, d // LANES, SUBLANES, 1, LANES))


def _row_view(v):
    g, c, s, _, l = v.shape
    return v.reshape(g, c, s, l).transpose(0, 2, 1, 3).reshape(g * s, c * l)


def _tile_row(view_ref, row):
    return view_ref.at[row >> 3, :, row & (SUBLANES - 1)]


def _dispatch_kernel(pad_ref, dest_ref, x_ref, xs_ref, zero_ref, sem):
    step = pl.program_id(0)
    groups = x_ref.shape[0]

    @pl.when(step == 0)
    def _():
        zero_ref[...] = jnp.zeros_like(zero_ref)

        def fill(e):
            return pltpu.make_async_copy(zero_ref, xs_ref.at[pl.ds(pad_ref[e] >> 3, ZERO_GROUPS)], sem)

        def start(e, c):
            fill(e).start()
            return c

        def wait(e, c):
            fill(e).wait()
            return c

        lax.fori_loop(0, N_EXPERTS, start, 0)
        lax.fori_loop(0, N_EXPERTS, wait, 0)

    def copies(g):
        return [pltpu.make_async_copy(x_ref.at[g, :, s], _tile_row(xs_ref, dest_ref[k, g * SUBLANES + s]), sem)
                for s in range(SUBLANES) for k in range(TOP_K)]

    def start_rows(g, c):
        for cp in copies(g):
            cp.start()
        return c

    def wait_rows(g, c):
        for cp in copies(g):
            cp.wait()
        return c

    lax.fori_loop(0, groups, start_rows, 0)
    lax.fori_loop(0, groups, wait_rows, 0)


def _dispatch(pad_start, dest, x, n_rows):
    n_tok, w = x.shape
    tm = TM_DISPATCH
    chunks = w // LANES
    block = (tm // SUBLANES, chunks, SUBLANES, 1, LANES)
    total_groups = n_rows // SUBLANES + ZERO_GROUPS
    out = pl.pallas_call(
        _dispatch_kernel,
        grid_spec=pltpu.PrefetchScalarGridSpec(
            num_scalar_prefetch=1,
            grid=(n_tok // tm,),
            in_specs=[pl.BlockSpec((TOP_K, tm), lambda i, pad: (0, i), memory_space=pltpu.SMEM),
                      pl.BlockSpec(block, lambda i, pad: (i, 0, 0, 0, 0))],
            out_specs=pl.BlockSpec(memory_space=pl.ANY),
            scratch_shapes=[pltpu.VMEM((ZERO_GROUPS, chunks, SUBLANES, 1, LANES), F32),
                            pltpu.SemaphoreType.DMA(())]),
        out_shape=jax.ShapeDtypeStruct((total_groups, chunks, SUBLANES, 1, LANES), F32),
        compiler_params=_params("arbitrary"),
        name="dispatch",
    )(pad_start, dest, _tile_view(x))
    return _row_view(out)


def _expert_kernel(bexp_ref, nused_ref, xs_ref, wg_ref, wu_ref, wd_ref, ys_ref, wg_s, wu_s, wd_s):
    i = pl.program_id(0)

    @pl.when(i < nused_ref[0])
    def _():
        e = bexp_ref[i]
        e_prev = bexp_ref[jnp.maximum(i - 1, 0)]

        @pl.when((i == 0) | (e != e_prev))
        def _():
            wg_s[...] = wg_ref[...].astype(BF16)
            wu_s[...] = wu_ref[...].astype(BF16)
            wd_s[...] = wd_ref[...].astype(BF16)

        xb = xs_ref[...].astype(BF16)
        hmid = (_silu(_dot(xb, wg_s[...])) * _dot(xb, wu_s[...])).astype(BF16)
        ys_ref[...] = _dot(hmid, wd_s[...])


def _experts(block_exp, n_used, xs, w_gate, w_up, w_down, n_blocks):
    _, d, ff = w_gate.shape
    rows = lambda i, bexp, nused: (jnp.minimum(i, nused[0] - 1), 0)
    wsel = lambda i, bexp, nused: (bexp[jnp.minimum(i, nused[0] - 1)], 0, 0)
    return pl.pallas_call(
        _expert_kernel,
        grid_spec=pltpu.PrefetchScalarGridSpec(
            num_scalar_prefetch=2,
            grid=(n_blocks,),
            in_specs=[pl.BlockSpec((EXPERT_ROWS, d), rows),
                      pl.BlockSpec((None, d, ff), wsel),
                      pl.BlockSpec((None, d, ff), wsel),
                      pl.BlockSpec((None, ff, d), wsel)],
            out_specs=pl.BlockSpec((EXPERT_ROWS, d), rows),
            scratch_shapes=[pltpu.VMEM((d, ff), BF16), pltpu.VMEM((d, ff), BF16),
                            pltpu.VMEM((ff, d), BF16)]),
        out_shape=jax.ShapeDtypeStruct((n_blocks * EXPERT_ROWS, d), F32),
        compiler_params=_params("arbitrary"),
        name="experts",
    )(block_exp, n_used, xs, w_gate, w_up, w_down)


def _combine_kernel(dest_ref, gate_ref, x_ref, ys_ref, wsg_ref, wsu_ref, wsd_ref, lg_ref, lb_ref,
                    o_ref, buf, sem):
    tm, d = x_ref.shape
    groups, chunks = tm // SUBLANES, d // LANES
    dma_views = [buf.at[k].reshape(groups, chunks, SUBLANES, 1, LANES) for k in range(TOP_K)]
    load_views = [buf.at[k].reshape(groups, chunks, SUBLANES, LANES) for k in range(TOP_K)]

    def copies(g):
        return [pltpu.make_async_copy(_tile_row(ys_ref, dest_ref[k, g * SUBLANES + s]),
                                      dma_views[k].at[g, :, s], sem)
                for s in range(SUBLANES) for k in range(TOP_K)]

    def start_rows(g, c):
        for cp in copies(g):
            cp.start()
        return c

    def wait_rows(g, c):
        for cp in copies(g):
            cp.wait()
        return c

    lax.fori_loop(0, groups, start_rows, 0)

    x = x_ref[...]
    xb = x.astype(BF16)
    hmid = (_silu(_dot(xb, wsg_ref[...])) * _dot(xb, wsu_ref[...])).astype(BF16)
    acc = ALPHA * x + _dot(hmid, wsd_ref[...])

    lax.fori_loop(0, groups, wait_rows, 0)
    gates = gate_ref[...]

    def rows_of(k):
        return jnp.concatenate([load_views[k][:, c].reshape(tm, LANES) for c in range(chunks)], axis=1)

    routed = rows_of(0) * gates[:, 0:1]
    for k in range(1, TOP_K):
        routed += rows_of(k) * gates[:, k:k + 1]
    o_ref[...] = _layer_norm(acc + routed, lg_ref[...], lb_ref[...])


def _combine(dest, gates_t, x, ys, ws_gate, ws_up, ws_down, ln_g, ln_b):
    n_tok, d = x.shape
    tm = TM_COMBINE
    row = lambda i: (i, 0)
    const = lambda i: (0, 0)
    return pl.pallas_call(
        _combine_kernel,
        grid=(n_tok // tm,),
        in_specs=[pl.BlockSpec((TOP_K, tm), lambda i: (0, i), memory_space=pltpu.SMEM),
                  pl.BlockSpec((tm, TOP_K), row),
                  pl.BlockSpec((tm, d), row),
                  pl.BlockSpec(memory_space=pl.ANY),
                  pl.BlockSpec(ws_gate.shape, const),
                  pl.BlockSpec(ws_up.shape, const),
                  pl.BlockSpec(ws_down.shape, const),
                  pl.BlockSpec(ln_g.shape, const),
                  pl.BlockSpec(ln_b.shape, const)],
        out_specs=pl.BlockSpec((tm, d), row),
        out_shape=jax.ShapeDtypeStruct((n_tok, d), F32),
        scratch_shapes=[pltpu.VMEM((TOP_K, tm * (d // LANES), LANES), F32), pltpu.SemaphoreType.DMA(())],
        compiler_params=_params("arbitrary"),
        name="combine_shared_ln3",
    )(dest, gates_t, x, _tile_view(ys), ws_gate, ws_up, ws_down, ln_g, ln_b)


def _layer(x, mem, w_in, conv_w, g_attn_out, g_conv_out, w_out, ln1_g, ln1_b, w_q_mem, w_kv_mem,
           w_o_mem, ln2_g, ln2_b, w_router, router_bias, w_gate, w_up, w_down, ws_gate, ws_up,
           ws_down, ln3_g, ln3_b):
    batch, seq, d = x.shape
    mem_len = mem.shape[1]
    n_tok = batch * seq
    xf = x.reshape(n_tok, d)
    row = lambda v: v.reshape(1, -1)

    proj = _matmul(xf.astype(BF16), w_in.astype(BF16), F32, TM_PROJ, TN_PROJ)
    attn = _dilated_attention(proj, batch, seq)
    x1 = _mix_out(attn, proj, xf, w_out.astype(BF16), conv_w, row(g_attn_out), row(g_conv_out),
                  row(ln1_g), row(ln1_b), seq)

    kv = _matmul(mem.reshape(batch * mem_len, d).astype(BF16), w_kv_mem.astype(BF16), BF16,
                 mem_len, 1024)
    x2 = _xattn(x1, w_q_mem.astype(BF16), kv, w_o_mem.astype(BF16), row(ln2_g), row(ln2_b), seq,
                mem_len)

    idx, gates, rank, cnt = _router(x2, w_router.T, router_bias.reshape(N_EXPERTS, 1))
    counts = cnt[:, 0].astype(I32)
    padded = (counts + EXPERT_ROWS - 1) // EXPERT_ROWS * EXPERT_ROWS
    seg_end = jnp.cumsum(padded)
    seg_start = seg_end - padded
    experts = jnp.arange(N_EXPERTS, dtype=I32)
    dest = rank + jnp.sum(jnp.where(idx[..., None] == experts, seg_start, 0), axis=-1)
    n_blocks = -(-(n_tok * TOP_K) // EXPERT_ROWS) + N_EXPERTS
    block_first = jnp.arange(n_blocks, dtype=I32) * EXPERT_ROWS
    block_exp = jnp.minimum(jnp.sum(block_first[:, None] >= seg_end[None, :], axis=1),
                            N_EXPERTS - 1).astype(I32)
    n_used = (seg_end[-1:] // EXPERT_ROWS).astype(I32)

    xs = _dispatch(seg_start + counts, dest, x2, n_blocks * EXPERT_ROWS)
    ys = _experts(block_exp, n_used, xs, w_gate, w_up, w_down, n_blocks)
    out = _combine(dest, gates.T, x2, ys, ws_gate.astype(BF16), ws_up.astype(BF16),
                   ws_down.astype(BF16), row(ln3_g), row(ln3_b))
    return out.reshape(batch, seq, d)


def kernel(x, mem, w_in, conv_w, g_attn_out, g_conv_out, w_out, ln1_g, ln1_b, w_q_mem, w_kv_mem, w_o_mem, ln2_g, ln2_b, w_router, router_bias, w_gate, w_up, w_down, ws_gate, ws_up, ws_down, ln3_g, ln3_b):
    for l in range(DEPTH):
        x = _layer(x, mem, w_in[l], conv_w[l], g_attn_out[l], g_conv_out[l], w_out[l], ln1_g[l],
                   ln1_b[l], w_q_mem[l], w_kv_mem[l], w_o_mem[l], ln2_g[l], ln2_b[l], w_router[l],
                   router_bias[l], w_gate[l], w_up[l], w_down[l], ws_gate[l], ws_up[l], ws_down[l],
                   ln3_g[l], ln3_b[l])
    return x
```

```python
import functools

import jax
import jax.numpy as jnp
from jax import lax
from jax.experimental import pallas as pl
from jax.experimental.pallas import tpu as pltpu

F32 = jnp.float32
BF16 = jnp.bfloat16
I32 = jnp.int32

HEAD_DIM = 128
N_ATTN_HEADS = 8
ATTN_WIDTH = N_ATTN_HEADS * HEAD_DIM
CONV_WIDTH = 1024
PROJ_PARTS = 6
DILATED_PATTERNS = ((128, 1), (512, 4), (2048, 16))
ATTN_BLOCK = 128
ATTN_SPAN = 16 * ATTN_BLOCK
N_MEM_HEADS = 4
N_EXPERTS = 64
TOP_K = 8
N_GROUPS = 8
GROUP_SIZE = N_EXPERTS // N_GROUPS
TOPK_GROUPS = 4
ROUTED_SCALE = 2.5
LN_EPS = 1e-5
RMS_EPS = 1e-6
DEPTH = 1
ALPHA = (2.0 * DEPTH) ** 0.25

LANES = 128
SUBLANES = 8
VMEM_LIMIT = 56 * 1024 * 1024
EXPERT_ROWS = 256
ZERO_GROUPS = EXPERT_ROWS // SUBLANES + 1
TM_PROJ, TN_PROJ = 1024, 512
TM_MIX = 256
TM_XATTN = 256
TM_ROUTER = 512
TM_DISPATCH = 256
TM_COMBINE = 128


def _params(*semantics):
    return pltpu.CompilerParams(dimension_semantics=semantics, vmem_limit_bytes=VMEM_LIMIT)


def _layer_norm(r, g, b):
    mu = jnp.mean(r, axis=-1, keepdims=True)
    c = r - mu
    var = jnp.mean(c * c, axis=-1, keepdims=True)
    return c * lax.rsqrt(var + LN_EPS) * g + b


def _rms_norm(v, g):
    return v * lax.rsqrt(jnp.mean(v * v, axis=-1, keepdims=True) + RMS_EPS) * g


def _dot(a, b):
    return jnp.dot(a, b, preferred_element_type=F32)


def _dot_nt(a, b):
    return lax.dot_general(a, b, (((1,), (1,)), ((), ())), preferred_element_type=F32)


def _silu(v):
    return v / (1.0 + jnp.exp(-v))


def _matmul_kernel(a_ref, w_ref, o_ref, a_bf):
    @pl.when(pl.program_id(1) == 0)
    def _():
        a_bf[...] = a_ref[...].astype(BF16)

    o_ref[...] = _dot(a_bf[...], w_ref[...]).astype(o_ref.dtype)


def _matmul(a, w, out_dtype, tm, tn):
    m, k = a.shape
    n = w.shape[1]
    return pl.pallas_call(
        _matmul_kernel,
        grid=(m // tm, n // tn),
        in_specs=[pl.BlockSpec((tm, k), lambda i, j: (i, 0)),
                  pl.BlockSpec((k, tn), lambda i, j: (0, j))],
        out_specs=pl.BlockSpec((tm, tn), lambda i, j: (i, j)),
        out_shape=jax.ShapeDtypeStruct((m, n), out_dtype),
        scratch_shapes=[pltpu.VMEM((tm, k), BF16)],
        compiler_params=_params("parallel", "arbitrary"),
        name="matmul",
    )(a, w)


def _attn_kernel(q_ref, kp_ref, kc_ref, vp_ref, vc_ref, o_ref, kk, vv, o_acc, l_acc):
    j = pl.program_id(1)
    blk, span = ATTN_BLOCK, ATTN_SPAN
    kk[0:span, :] = kp_ref[...]
    kk[span:, :] = kc_ref[...]
    vv[0:span, :] = vp_ref[...]
    vv[span:, :] = vc_ref[...]

    row = lax.broadcasted_iota(I32, (blk, 2 * blk), 0)
    col = lax.broadcasted_iota(I32, (blk, 2 * blk), 1)
    dist = row + blk - col
    in_window = (dist >= 0) & (dist <= blk)
    own_block = col >= blk
    scale = HEAD_DIM ** -0.5

    order = sorted(range(len(DILATED_PATTERNS)), key=lambda i: -DILATED_PATTERNS[i][1])
    assert DILATED_PATTERNS[order[-1]][1] == 1
    sub_blocks = span // blk
    for bi in order:
        dil = DILATED_PATTERNS[bi][1]
        shift = dil.bit_length() - 1

        def body(t, carry, bi=bi, dil=dil, shift=shift):
            r = t & (dil - 1)
            n = t >> shift
            base = n * (blk * dil) + r
            if dil == 1:
                base = pl.multiple_of(base, blk)
                q_rows, kv_rows = pl.ds(base, blk), pl.ds(span + base - blk, 2 * blk)
            else:
                q_rows = pl.ds(base, blk, stride=dil)
                kv_rows = pl.ds(span + base - blk * dil, 2 * blk, stride=dil)
            q = q_ref[q_rows, :].astype(BF16)
            k = kk[kv_rows, :].astype(BF16)
            v = vv[kv_rows, :].astype(BF16)
            s = _dot_nt(q, k) * scale
            valid = in_window & (own_block | (j > 0) | (n > 0))
            s = jnp.where(valid, s, -jnp.inf)
            m = jnp.max(s, axis=-1, keepdims=True)
            p = jnp.exp(s - m)
            l = jnp.sum(p, axis=-1, keepdims=True)
            o = _dot((p / l).astype(BF16), v)
            lse = jnp.broadcast_to(m + jnp.log(l), (blk, LANES))
            if dil != 1:
                o_acc[bi, q_rows, :] = o
                l_acc[bi, q_rows, :] = lse
                return carry
            others = [i for i in order if i != bi]
            lses = [lse] + [l_acc[i, q_rows, :] for i in others]
            outs = [o] + [o_acc[i, q_rows, :] for i in others]
            top = jnp.maximum(jnp.maximum(lses[0], lses[1]), lses[2])
            es = [jnp.exp(v_ - top) for v_ in lses]
            den = es[0] + es[1] + es[2]
            o_ref[q_rows, :] = ((es[0] / den) * outs[0] + (es[1] / den) * outs[1] + (es[2] / den) * outs[2])
            return carry

        lax.fori_loop(0, sub_blocks, body, 0, unroll=8)


def _dilated_attention(proj, batch, seq):
    n_tok, _ = proj.shape
    span = ATTN_SPAN
    spans = seq // span
    heads = N_ATTN_HEADS

    def spec(part, prev):
        def index(b, j, h):
            return (b * spans + (jnp.maximum(j - 1, 0) if prev else j), part * heads + h)
        return pl.BlockSpec((span, HEAD_DIM), index)

    return pl.pallas_call(
        _attn_kernel,
        grid=(batch, spans, heads),
        in_specs=[spec(0, False), spec(1, True), spec(1, False), spec(2, True), spec(2, False)],
        out_specs=pl.BlockSpec((span, HEAD_DIM), lambda b, j, h: (b * spans + j, h)),
        out_shape=jax.ShapeDtypeStruct((n_tok, ATTN_WIDTH), F32),
        scratch_shapes=[pltpu.VMEM((2 * span, HEAD_DIM), F32), pltpu.VMEM((2 * span, HEAD_DIM), F32),
                        pltpu.VMEM((len(DILATED_PATTERNS), span, HEAD_DIM), F32),
                        pltpu.VMEM((len(DILATED_PATTERNS), span, HEAD_DIM), F32)],
        compiler_params=_params("parallel", "parallel", "parallel"),
        name="dilated_attn",
    )(proj, proj, proj, proj, proj)


def _mix_out_kernel(a_ref, b_ref, c_ref, h_ref, cp_ref, hp_ref, x_ref,
                    w_ref, cw_ref, ga_ref, gc_ref, lg_ref, lb_ref, o_ref, *, tiles_per_seq):
    i = pl.program_id(0)
    z = c_ref[...] * h_ref[...]
    zp = jnp.where(i % tiles_per_seq == 0, 0.0, cp_ref[...] * hp_ref[...])
    zz = jnp.concatenate([zp, z], axis=0)
    z1 = pltpu.roll(zz, 1, axis=0)[SUBLANES:]
    z2 = pltpu.roll(zz, 2, axis=0)[SUBLANES:]
    conv = b_ref[...] * (cw_ref[0:1, :] * z2 + cw_ref[1:2, :] * z1 + cw_ref[2:3, :] * z)

    mixed = jnp.concatenate([_rms_norm(a_ref[...], ga_ref[...]), _rms_norm(conv, gc_ref[...])], axis=1)
    y = _dot(mixed.astype(BF16), w_ref[...])
    o_ref[...] = _layer_norm(ALPHA * x_ref[...] + y, lg_ref[...], lb_ref[...])


def _mix_out(attn, proj, x, w_out, conv_w, g_attn, g_conv, ln_g, ln_b, seq):
    n_tok, d = x.shape
    tm = TM_MIX
    rows8 = tm // SUBLANES
    row = lambda i: (i, 0)
    const = lambda i: (0, 0)
    prev = lambda part: (lambda i: (jnp.maximum(i * rows8 - 1, 0), part))
    return pl.pallas_call(
        functools.partial(_mix_out_kernel, tiles_per_seq=seq // tm),
        grid=(n_tok // tm,),
        in_specs=[
            pl.BlockSpec((tm, ATTN_WIDTH), row),
            pl.BlockSpec((tm, CONV_WIDTH), lambda i: (i, 3)),
            pl.BlockSpec((tm, CONV_WIDTH), lambda i: (i, 4)),
            pl.BlockSpec((tm, CONV_WIDTH), lambda i: (i, 5)),
            pl.BlockSpec((SUBLANES, CONV_WIDTH), prev(4)),
            pl.BlockSpec((SUBLANES, CONV_WIDTH), prev(5)),
            pl.BlockSpec((tm, d), row),
            pl.BlockSpec(w_out.shape, const),
            pl.BlockSpec(conv_w.shape, const),
            pl.BlockSpec(g_attn.shape, const),
            pl.BlockSpec(g_conv.shape, const),
            pl.BlockSpec(ln_g.shape, const),
            pl.BlockSpec(ln_b.shape, const)],
        out_specs=pl.BlockSpec((tm, d), row),
        out_shape=jax.ShapeDtypeStruct((n_tok, d), F32),
        compiler_params=_params("parallel"),
        name="mix_out_ln1",
    )(attn, proj, proj, proj, proj, proj, x, w_out, conv_w, g_attn, g_conv, ln_g, ln_b)


def _xattn_kernel(x_ref, wq_ref, k_ref, v_ref, wo_ref, lg_ref, lb_ref, o_ref):
    x = x_ref[...]
    d = x.shape[1]
    hd = d // N_MEM_HEADS
    q = _dot(x.astype(BF16), wq_ref[...]).astype(BF16)
    scale = hd ** -0.5
    outs = []
    for h in range(N_MEM_HEADS):
        sl = slice(h * hd, (h + 1) * hd)
        s = _dot_nt(q[:, sl], k_ref[:, sl]) * scale
        m = jnp.max(s, axis=-1, keepdims=True)
        p = jnp.exp(s - m)
        p = p / jnp.sum(p, axis=-1, keepdims=True)
        outs.append(_dot(p.astype(BF16), v_ref[:, sl]).astype(BF16))
    y = _dot(jnp.concatenate(outs, axis=1), wo_ref[...])
    o_ref[...] = _layer_norm(ALPHA * x + y, lg_ref[...], lb_ref[...])


def _xattn(x, w_q, kv, w_o, ln_g, ln_b, seq, mem_len):
    n_tok, d = x.shape
    tm = TM_XATTN
    row = lambda i: (i, 0)
    const = lambda i: (0, 0)
    tiles_per_seq = seq // tm
    return pl.pallas_call(
        _xattn_kernel,
        grid=(n_tok // tm,),
        in_specs=[pl.BlockSpec((tm, d), row),
                  pl.BlockSpec(w_q.shape, const),
                  pl.BlockSpec((mem_len, d), lambda i: (i // tiles_per_seq, 0)),
                  pl.BlockSpec((mem_len, d), lambda i: (i // tiles_per_seq, 1)),
                  pl.BlockSpec(w_o.shape, const),
                  pl.BlockSpec(ln_g.shape, const),
                  pl.BlockSpec(ln_b.shape, const)],
        out_specs=pl.BlockSpec((tm, d), row),
        out_shape=jax.ShapeDtypeStruct((n_tok, d), F32),
        compiler_params=_params("parallel"),
        name="xattn_ln2",
    )(x, w_q, kv, kv, w_o, ln_g, ln_b)


def _first_argmax(vals, index, sentinel):
    m = jnp.max(vals, axis=0, keepdims=True)
    i = jnp.min(jnp.where(vals == m, index, sentinel), axis=0, keepdims=True)
    return m, i


def _router_kernel(x_ref, wt_ref, bias_ref, idx_ref, gate_ref, rank_ref, cnt_ref):
    step = pl.program_id(0)
    tm = x_ref.shape[0]

    @pl.when(step == 0)
    def _():
        cnt_ref[...] = jnp.zeros_like(cnt_ref)

    logits = lax.dot_general(wt_ref[...], x_ref[...], (((1,), (1,)), ((), ())),
                             precision=lax.Precision.HIGHEST, preferred_element_type=F32)
    scores = 1.0 / (1.0 + jnp.exp(-logits))
    choice = scores + bias_ref[...]

    sub = lax.broadcasted_iota(I32, (GROUP_SIZE, tm), 0)
    group_scores = []
    for g in range(N_GROUPS):
        c = choice[g * GROUP_SIZE:(g + 1) * GROUP_SIZE, :]
        m1, i1 = _first_argmax(c, sub, GROUP_SIZE)
        m2 = jnp.max(jnp.where(sub == i1, -jnp.inf, c), axis=0, keepdims=True)
        group_scores.append(m1 + m2)
    gs = jnp.concatenate(group_scores, axis=0)
    gidx = lax.broadcasted_iota(I32, (N_GROUPS, tm), 0)
    gsel = jnp.zeros((N_GROUPS, tm), F32)
    for _ in range(TOPK_GROUPS):
        _, gi = _first_argmax(gs, gidx, N_GROUPS)
        hit = gidx == gi
        gsel = jnp.where(hit, 1.0, gsel)
        gs = jnp.where(hit, -jnp.inf, gs)
    masked = jnp.concatenate(
        [jnp.where(gsel[g:g + 1, :] > 0.0, choice[g * GROUP_SIZE:(g + 1) * GROUP_SIZE, :], -jnp.inf)
         for g in range(N_GROUPS)], axis=0)

    eidx = lax.broadcasted_iota(I32, (N_EXPERTS, tm), 0)
    hits, idxs, ws = [], [], []
    for _ in range(TOP_K):
        _, ei = _first_argmax(masked, eidx, N_EXPERTS)
        hit = eidx == ei
        hits.append(hit)
        idxs.append(ei)
        ws.append(jnp.sum(jnp.where(hit, scores, 0.0), axis=0, keepdims=True))
        masked = jnp.where(hit, -jnp.inf, masked)
    wsum = ws[0]
    for w in ws[1:]:
        wsum = wsum + w
    gate_ref[...] = jnp.concatenate([w / wsum * ROUTED_SCALE for w in ws], axis=0)
    idx_ref[...] = jnp.concatenate(idxs, axis=0)

    sel = hits[0]
    for hit in hits[1:]:
        sel = sel | hit
    self32 = jnp.where(sel, 1.0, 0.0)
    t_row = lax.broadcasted_iota(I32, (tm, tm), 0)
    t_col = lax.broadcasted_iota(I32, (tm, tm), 1)
    tri = jnp.where(t_row < t_col, 1.0, 0.0).astype(BF16)
    before = _dot(self32.astype(BF16), tri) + cnt_ref[:, 0:1]
    rank_ref[...] = jnp.concatenate(
        [jnp.sum(jnp.where(hit, before, 0.0), axis=0, keepdims=True) for hit in hits],
        axis=0).astype(I32)
    cnt_ref[...] += jnp.sum(self32, axis=1, keepdims=True)


def _router(x, w_router_t, bias_col):
    n_tok, d = x.shape
    tm = TM_ROUTER
    tok = lambda i: (0, i)
    return pl.pallas_call(
        _router_kernel,
        grid=(n_tok // tm,),
        in_specs=[pl.BlockSpec((tm, d), lambda i: (i, 0)),
                  pl.BlockSpec(w_router_t.shape, lambda i: (0, 0)),
                  pl.BlockSpec(bias_col.shape, lambda i: (0, 0))],
        out_specs=[pl.BlockSpec((TOP_K, tm), tok), pl.BlockSpec((TOP_K, tm), tok),
                   pl.BlockSpec((TOP_K, tm), tok),
                   pl.BlockSpec((N_EXPERTS, LANES), lambda i: (0, 0))],
        out_shape=[jax.ShapeDtypeStruct((TOP_K, n_tok), I32),
                   jax.ShapeDtypeStruct((TOP_K, n_tok), F32),
                   jax.ShapeDtypeStruct((TOP_K, n_tok), I32),
                   jax.ShapeDtypeStruct((N_EXPERTS, LANES), F32)],
        compiler_params=_params("arbitrary"),
        name="router",
    )(x, w_router_t, bias_col)


def _tile_view(x):
    r, d = x.shape
    return (x.reshape(r // SUBLANES, SUBLANES, d // LANES, LANES).transpose(0, 2, 1, 3)
            .reshape(r // SUBLANES, d // LANES, SUBLANES, 1, LANES))


def _row_view(v):
    g, c, s, _, l = v.shape
    return v.reshape(g, c, s, l).transpose(0, 2, 1, 3).reshape(g * s, c * l)


def _tile_row(view_ref, row):
    return view_ref.at[row >> 3, :, row & (SUBLANES - 1)]


def _dispatch_kernel(pad_ref, dest_ref, x_ref, xs_ref, zero_ref, sem):
    step = pl.program_id(0)
    groups = x_ref.shape[0]

    @pl.when(step == 0)
    def _():
        zero_ref[...] = jnp.zeros_like(zero_ref)

        def fill(e):
            return pltpu.make_async_copy(zero_ref, xs_ref.at[pl.ds(pad_ref[e] >> 3, ZERO_GROUPS)], sem)

        def start(e, c):
            fill(e).start()
            return c

        def wait(e, c):
            fill(e).wait()
            return c

        lax.fori_loop(0, N_EXPERTS, start, 0)
        lax.fori_loop(0, N_EXPERTS, wait, 0)

    def copies(g):
        return [pltpu.make_async_copy(x_ref.at[g, :, s], _tile_row(xs_ref, dest_ref[k, g * SUBLANES + s]), sem)
                for s in range(SUBLANES) for k in range(TOP_K)]

    def start_rows(g, c):
        for cp in copies(g):
            cp.start()
        return c

    def wait_rows(g, c):
        for cp in copies(g):
            cp.wait()
        return c

    lax.fori_loop(0, groups, start_rows, 0)
    lax.fori_loop(0, groups, wait_rows, 0)


def _dispatch(pad_start, dest, x, n_rows):
    n_tok, w = x.shape
    tm = TM_DISPATCH
    chunks = w // LANES
    block = (tm // SUBLANES, chunks, SUBLANES, 1, LANES)
    total_groups = n_rows // SUBLANES + ZERO_GROUPS
    out = pl.pallas_call(
        _dispatch_kernel,
        grid_spec=pltpu.PrefetchScalarGridSpec(
            num_scalar_prefetch=1,
            grid=(n_tok // tm,),
            in_specs=[pl.BlockSpec((TOP_K, tm), lambda i, pad: (0, i), memory_space=pltpu.SMEM),
                      pl.BlockSpec(block, lambda i, pad: (i, 0, 0, 0, 0))],
            out_specs=pl.BlockSpec(memory_space=pl.ANY),
            scratch_shapes=[pltpu.VMEM((ZERO_GROUPS, chunks, SUBLANES, 1, LANES), F32),
                            pltpu.SemaphoreType.DMA(())]),
        out_shape=jax.ShapeDtypeStruct((total_groups, chunks, SUBLANES, 1, LANES), F32),
        compiler_params=_params("arbitrary"),
        name="dispatch",
    )(pad_start, dest, _tile_view(x))
    return _row_view(out)


def _expert_kernel(bexp_ref, nused_ref, next_ref, slot_ref, xs_ref, wg_hbm, wu_hbm, wd_hbm, ys_ref,
                   wg_f, wu_f, wd_f, wg_s, wu_s, wd_s, sem):
    i = pl.program_id(0)

    def fetch(e, slot):
        return [pltpu.make_async_copy(src.at[e], dst.at[slot], sem.at[slot])
                for src, dst in ((wg_hbm, wg_f), (wu_hbm, wu_f), (wd_hbm, wd_f))]

    @pl.when(i < nused_ref[0])
    def _():
        e = bexp_ref[i]
        e_prev = bexp_ref[jnp.maximum(i - 1, 0)]
        slot = slot_ref[e]

        @pl.when(i == 0)
        def _():
            for cp in fetch(e, slot):
                cp.start()

        @pl.when((i == 0) | (e != e_prev))
        def _():
            for cp in fetch(e, slot):
                cp.wait()
            wg_s[...] = wg_f[slot].astype(BF16)
            wu_s[...] = wu_f[slot].astype(BF16)
            wd_s[...] = wd_f[slot].astype(BF16)
            e_next = next_ref[e]

            @pl.when(e_next < N_EXPERTS)
            def _():
                for cp in fetch(e_next, 1 - slot):
                    cp.start()

        xb = xs_ref[...].astype(BF16)
        hmid = (_silu(_dot(xb, wg_s[...])) * _dot(xb, wu_s[...])).astype(BF16)
        ys_ref[...] = _dot(hmid, wd_s[...])


def _experts(block_exp, n_used, next_expert, slot, xs, w_gate, w_up, w_down, n_blocks):
    _, d, ff = w_gate.shape
    rows = lambda i, bexp, nused, nxt, slt: (jnp.minimum(i, nused[0] - 1), 0)
    hbm = pl.BlockSpec(memory_space=pl.ANY)
    return pl.pallas_call(
        _expert_kernel,
        grid_spec=pltpu.PrefetchScalarGridSpec(
            num_scalar_prefetch=4,
            grid=(n_blocks,),
            in_specs=[pl.BlockSpec((EXPERT_ROWS, d), rows), hbm, hbm, hbm],
            out_specs=pl.BlockSpec((EXPERT_ROWS, d), rows),
            scratch_shapes=[pltpu.VMEM((2, d, ff), F32), pltpu.VMEM((2, d, ff), F32),
                            pltpu.VMEM((2, ff, d), F32),
                            pltpu.VMEM((d, ff), BF16), pltpu.VMEM((d, ff), BF16),
                            pltpu.VMEM((ff, d), BF16),
                            pltpu.SemaphoreType.DMA((2,))]),
        out_shape=jax.ShapeDtypeStruct((n_blocks * EXPERT_ROWS, d), F32),
        compiler_params=_params("arbitrary"),
        name="experts",
    )(block_exp, n_used, next_expert, slot, xs, w_gate, w_up, w_down)


def _combine_kernel(dest_ref, gate_ref, x_ref, ys_ref, wsg_ref, wsu_ref, wsd_ref, lg_ref, lb_ref,
                    o_ref, buf, sem):
    tm, d = x_ref.shape
    groups, chunks = tm // SUBLANES, d // LANES
    dma_views = [buf.at[k].reshape(groups, chunks, SUBLANES, 1, LANES) for k in range(TOP_K)]
    load_views = [buf.at[k].reshape(groups, chunks, SUBLANES, LANES) for k in range(TOP_K)]

    def copies(g):
        return [pltpu.make_async_copy(_tile_row(ys_ref, dest_ref[k, g * SUBLANES + s]),
                                      dma_views[k].at[g, :, s], sem)
                for s in range(SUBLANES) for k in range(TOP_K)]

    def start_rows(g, c):
        for cp in copies(g):
            cp.start()
        return c

    def wait_rows(g, c):
        for cp in copies(g):
            cp.wait()
        return c

    lax.fori_loop(0, groups, start_rows, 0)

    x = x_ref[...]
    xb = x.astype(BF16)
    hmid = (_silu(_dot(xb, wsg_ref[...])) * _dot(xb, wsu_ref[...])).astype(BF16)
    acc = ALPHA * x + _dot(hmid, wsd_ref[...])

    lax.fori_loop(0, groups, wait_rows, 0)
    gates = gate_ref[...]

    def rows_of(k):
        return jnp.concatenate([load_views[k][:, c].reshape(tm, LANES) for c in range(chunks)], axis=1)

    routed = rows_of(0) * gates[:, 0:1]
    for k in range(1, TOP_K):
        routed += rows_of(k) * gates[:, k:k + 1]
    o_ref[...] = _layer_norm(acc + routed, lg_ref[...], lb_ref[...])


def _combine(dest, gates_t, x, ys, ws_gate, ws_up, ws_down, ln_g, ln_b):
    n_tok, d = x.shape
    tm = TM_COMBINE
    row = lambda i: (i, 0)
    const = lambda i: (0, 0)
    return pl.pallas_call(
        _combine_kernel,
        grid=(n_tok // tm,),
        in_specs=[pl.BlockSpec((TOP_K, tm), lambda i: (0, i), memory_space=pltpu.SMEM),
                  pl.BlockSpec((tm, TOP_K), row),
                  pl.BlockSpec((tm, d), row),
                  pl.BlockSpec(memory_space=pl.ANY),
                  pl.BlockSpec(ws_gate.shape, const),
                  pl.BlockSpec(ws_up.shape, const),
                  pl.BlockSpec(ws_down.shape, const),
                  pl.BlockSpec(ln_g.shape, const),
                  pl.BlockSpec(ln_b.shape, const)],
        out_specs=pl.BlockSpec((tm, d), row),
        out_shape=jax.ShapeDtypeStruct((n_tok, d), F32),
        scratch_shapes=[pltpu.VMEM((TOP_K, tm * (d // LANES), LANES), F32), pltpu.SemaphoreType.DMA(())],
        compiler_params=_params("arbitrary"),
        name="combine_shared_ln3",
    )(dest, gates_t, x, _tile_view(ys), ws_gate, ws_up, ws_down, ln_g, ln_b)


def _layer(x, mem, w_in, conv_w, g_attn_out, g_conv_out, w_out, ln1_g, ln1_b, w_q_mem, w_kv_mem,
           w_o_mem, ln2_g, ln2_b, w_router, router_bias, w_gate, w_up, w_down, ws_gate, ws_up,
           ws_down, ln3_g, ln3_b):
    batch, seq, d = x.shape
    mem_len = mem.shape[1]
    n_tok = batch * seq
    xf = x.reshape(n_tok, d)
    row = lambda v: v.reshape(1, -1)

    proj = _matmul(xf, w_in.astype(BF16), F32, TM_PROJ, TN_PROJ)
    attn = _dilated_attention(proj, batch, seq)
    x1 = _mix_out(attn, proj, xf, w_out.astype(BF16), conv_w, row(g_attn_out), row(g_conv_out),
                  row(ln1_g), row(ln1_b), seq)

    kv = _matmul(mem.reshape(batch * mem_len, d), w_kv_mem.astype(BF16), BF16,
                 mem_len, 1024)
    x2 = _xattn(x1, w_q_mem.astype(BF16), kv, w_o_mem.astype(BF16), row(ln2_g), row(ln2_b), seq,
                mem_len)

    idx, gates, rank, cnt = _router(x2, w_router.T, router_bias.reshape(N_EXPERTS, 1))
    counts = cnt[:, 0].astype(I32)
    padded = (counts + EXPERT_ROWS - 1) // EXPERT_ROWS * EXPERT_ROWS
    seg_end = jnp.cumsum(padded)
    seg_start = seg_end - padded
    experts = jnp.arange(N_EXPERTS, dtype=I32)
    dest = rank + jnp.sum(jnp.where(idx[..., None] == experts, seg_start, 0), axis=-1)
    n_blocks = -(-(n_tok * TOP_K) // EXPERT_ROWS) + N_EXPERTS
    block_first = jnp.arange(n_blocks, dtype=I32) * EXPERT_ROWS
    block_exp = jnp.minimum(jnp.sum(block_first[:, None] >= seg_end[None, :], axis=1),
                            N_EXPERTS - 1).astype(I32)
    n_used = (seg_end[-1:] // EXPERT_ROWS).astype(I32)

    xs = _dispatch(seg_start + counts, dest, x2, n_blocks * EXPERT_ROWS)
    nonempty = counts > 0
    first_from = lax.cummin(jnp.where(nonempty, experts, N_EXPERTS), reverse=True)
    next_expert = jnp.concatenate([first_from[1:], jnp.full((1,), N_EXPERTS, I32)])
    slot = (jnp.cumsum(nonempty.astype(I32)) - 1) & 1
    ys = _experts(block_exp, n_used, next_expert, slot, xs, w_gate, w_up, w_down, n_blocks)
    out = _combine(dest, gates.T, x2, ys, ws_gate.astype(BF16), ws_up.astype(BF16),
                   ws_down.astype(BF16), row(ln3_g), row(ln3_b))
    return out.reshape(batch, seq, d)


def kernel(x, mem, w_in, conv_w, g_attn_out, g_conv_out, w_out, ln1_g, ln1_b, w_q_mem, w_kv_mem, w_o_mem, ln2_g, ln2_b, w_router, router_bias, w_gate, w_up, w_down, ws_gate, ws_up, ws_down, ln3_g, ln3_b):
    for l in range(DEPTH):
        x = _layer(x, mem, w_in[l], conv_w[l], g_attn_out[l], g_conv_out[l], w_out[l], ln1_g[l],
                   ln1_b[l], w_q_mem[l], w_kv_mem[l], w_o_mem[l], ln2_g[l], ln2_b[l], w_router[l],
                   router_bias[l], w_gate[l], w_up[l], w_down[l], ws_gate[l], ws_up[l], ws_down[l],
                   ln3_g[l], ln3_b[l])
    return x
```

```python
import functools

import jax
import jax.numpy as jnp
from jax import lax
from jax.experimental import pallas as pl
from jax.experimental.pallas import tpu as pltpu

F32 = jnp.float32
BF16 = jnp.bfloat16
I32 = jnp.int32

HEAD_DIM = 128
N_ATTN_HEADS = 8
ATTN_WIDTH = N_ATTN_HEADS * HEAD_DIM
CONV_WIDTH = 1024
PROJ_PARTS = 6
DILATED_PATTERNS = ((128, 1), (512, 4), (2048, 16))
ATTN_BLOCK = 128
ATTN_SPAN = 16 * ATTN_BLOCK
N_MEM_HEADS = 4
N_EXPERTS = 64
TOP_K = 8
N_GROUPS = 8
GROUP_SIZE = N_EXPERTS // N_GROUPS
TOPK_GROUPS = 4
ROUTED_SCALE = 2.5
LN_EPS = 1e-5
RMS_EPS = 1e-6
DEPTH = 1
ALPHA = (2.0 * DEPTH) ** 0.25

LANES = 128
SUBLANES = 8
VMEM_LIMIT = 56 * 1024 * 1024
DMA_QUEUES = 2
EXPERT_ROWS = 256
ZERO_GROUPS = EXPERT_ROWS // SUBLANES + 1
TM_PROJ, TN_PROJ = 1024, 512
TM_MIX = 256
TM_XATTN = 256
TM_ROUTER = 512
TM_DISPATCH = 256
TM_COMBINE = 128


def _params(*semantics):
    return pltpu.CompilerParams(dimension_semantics=semantics, vmem_limit_bytes=VMEM_LIMIT)


def _layer_norm(r, g, b):
    mu = jnp.mean(r, axis=-1, keepdims=True)
    c = r - mu
    var = jnp.mean(c * c, axis=-1, keepdims=True)
    return c * lax.rsqrt(var + LN_EPS) * g + b


def _rms_norm(v, g):
    return v * lax.rsqrt(jnp.mean(v * v, axis=-1, keepdims=True) + RMS_EPS) * g


def _dot(a, b):
    return jnp.dot(a, b, preferred_element_type=F32)


def _dot_nt(a, b):
    return lax.dot_general(a, b, (((1,), (1,)), ((), ())), preferred_element_type=F32)


def _silu(v):
    return v / (1.0 + jnp.exp(-v))


def _matmul_kernel(a_ref, w_ref, o_ref, a_bf):
    @pl.when(pl.program_id(1) == 0)
    def _():
        a_bf[...] = a_ref[...].astype(BF16)

    o_ref[...] = _dot(a_bf[...], w_ref[...]).astype(o_ref.dtype)


def _matmul(a, w, out_dtype, tm, tn):
    m, k = a.shape
    n = w.shape[1]
    return pl.pallas_call(
        _matmul_kernel,
        grid=(m // tm, n // tn),
        in_specs=[pl.BlockSpec((tm, k), lambda i, j: (i, 0)),
                  pl.BlockSpec((k, tn), lambda i, j: (0, j))],
        out_specs=pl.BlockSpec((tm, tn), lambda i, j: (i, j)),
        out_shape=jax.ShapeDtypeStruct((m, n), out_dtype),
        scratch_shapes=[pltpu.VMEM((tm, k), BF16)],
        compiler_params=_params("parallel", "arbitrary"),
        name="matmul",
    )(a, w)


def _attn_kernel(q_ref, kp_ref, kc_ref, vp_ref, vc_ref, o_ref, kk, vv, o_acc, l_acc):
    j = pl.program_id(1)
    blk, span = ATTN_BLOCK, ATTN_SPAN
    kk[0:span, :] = kp_ref[...]
    kk[span:, :] = kc_ref[...]
    vv[0:span, :] = vp_ref[...]
    vv[span:, :] = vc_ref[...]

    row = lax.broadcasted_iota(I32, (blk, 2 * blk), 0)
    col = lax.broadcasted_iota(I32, (blk, 2 * blk), 1)
    dist = row + blk - col
    in_window = (dist >= 0) & (dist <= blk)
    own_block = col >= blk
    scale = HEAD_DIM ** -0.5

    order = sorted(range(len(DILATED_PATTERNS)), key=lambda i: -DILATED_PATTERNS[i][1])
    assert DILATED_PATTERNS[order[-1]][1] == 1
    sub_blocks = span // blk
    for bi in order:
        dil = DILATED_PATTERNS[bi][1]
        shift = dil.bit_length() - 1

        def body(t, carry, bi=bi, dil=dil, shift=shift):
            r = t & (dil - 1)
            n = t >> shift
            base = n * (blk * dil) + r
            if dil == 1:
                base = pl.multiple_of(base, blk)
                q_rows, kv_rows = pl.ds(base, blk), pl.ds(span + base - blk, 2 * blk)
            else:
                q_rows = pl.ds(base, blk, stride=dil)
                kv_rows = pl.ds(span + base - blk * dil, 2 * blk, stride=dil)
            q = q_ref[q_rows, :].astype(BF16)
            k = kk[kv_rows, :].astype(BF16)
            v = vv[kv_rows, :].astype(BF16)
            s = _dot_nt(q, k) * scale
            valid = in_window & (own_block | (j > 0) | (n > 0))
            s = jnp.where(valid, s, -jnp.inf)
            m = jnp.max(s, axis=-1, keepdims=True)
            p = jnp.exp(s - m)
            l = jnp.sum(p, axis=-1, keepdims=True)
            o = _dot((p / l).astype(BF16), v)
            lse = jnp.broadcast_to(m + jnp.log(l), (blk, LANES))
            if dil != 1:
                o_acc[bi, q_rows, :] = o
                l_acc[bi, q_rows, :] = lse
                return carry
            others = [i for i in order if i != bi]
            lses = [lse] + [l_acc[i, q_rows, :] for i in others]
            outs = [o] + [o_acc[i, q_rows, :] for i in others]
            top = jnp.maximum(jnp.maximum(lses[0], lses[1]), lses[2])
            es = [jnp.exp(v_ - top) for v_ in lses]
            den = es[0] + es[1] + es[2]
            o_ref[q_rows, :] = ((es[0] / den) * outs[0] + (es[1] / den) * outs[1] + (es[2] / den) * outs[2])
            return carry

        lax.fori_loop(0, sub_blocks, body, 0, unroll=8)


def _dilated_attention(proj, batch, seq):
    n_tok, _ = proj.shape
    span = ATTN_SPAN
    spans = seq // span
    heads = N_ATTN_HEADS

    def spec(part, prev):
        def index(b, j, h):
            return (b * spans + (jnp.maximum(j - 1, 0) if prev else j), part * heads + h)
        return pl.BlockSpec((span, HEAD_DIM), index)

    return pl.pallas_call(
        _attn_kernel,
        grid=(batch, spans, heads),
        in_specs=[spec(0, False), spec(1, True), spec(1, False), spec(2, True), spec(2, False)],
        out_specs=pl.BlockSpec((span, HEAD_DIM), lambda b, j, h: (b * spans + j, h)),
        out_shape=jax.ShapeDtypeStruct((n_tok, ATTN_WIDTH), F32),
        scratch_shapes=[pltpu.VMEM((2 * span, HEAD_DIM), F32), pltpu.VMEM((2 * span, HEAD_DIM), F32),
                        pltpu.VMEM((len(DILATED_PATTERNS), span, HEAD_DIM), F32),
                        pltpu.VMEM((len(DILATED_PATTERNS), span, HEAD_DIM), F32)],
        compiler_params=_params("parallel", "parallel", "parallel"),
        name="dilated_attn",
    )(proj, proj, proj, proj, proj)


def _mix_out_kernel(a_ref, b_ref, c_ref, h_ref, cp_ref, hp_ref, x_ref,
                    w_ref, cw_ref, ga_ref, gc_ref, lg_ref, lb_ref, o_ref, *, tiles_per_seq):
    i = pl.program_id(0)
    z = c_ref[...] * h_ref[...]
    zp = jnp.where(i % tiles_per_seq == 0, 0.0, cp_ref[...] * hp_ref[...])
    zz = jnp.concatenate([zp, z], axis=0)
    z1 = pltpu.roll(zz, 1, axis=0)[SUBLANES:]
    z2 = pltpu.roll(zz, 2, axis=0)[SUBLANES:]
    conv = b_ref[...] * (cw_ref[0:1, :] * z2 + cw_ref[1:2, :] * z1 + cw_ref[2:3, :] * z)

    mixed = jnp.concatenate([_rms_norm(a_ref[...], ga_ref[...]), _rms_norm(conv, gc_ref[...])], axis=1)
    y = _dot(mixed.astype(BF16), w_ref[...])
    o_ref[...] = _layer_norm(ALPHA * x_ref[...] + y, lg_ref[...], lb_ref[...])


def _mix_out(attn, proj, x, w_out, conv_w, g_attn, g_conv, ln_g, ln_b, seq):
    n_tok, d = x.shape
    tm = TM_MIX
    rows8 = tm // SUBLANES
    row = lambda i: (i, 0)
    const = lambda i: (0, 0)
    prev = lambda part: (lambda i: (jnp.maximum(i * rows8 - 1, 0), part))
    return pl.pallas_call(
        functools.partial(_mix_out_kernel, tiles_per_seq=seq // tm),
        grid=(n_tok // tm,),
        in_specs=[
            pl.BlockSpec((tm, ATTN_WIDTH), row),
            pl.BlockSpec((tm, CONV_WIDTH), lambda i: (i, 3)),
            pl.BlockSpec((tm, CONV_WIDTH), lambda i: (i, 4)),
            pl.BlockSpec((tm, CONV_WIDTH), lambda i: (i, 5)),
            pl.BlockSpec((SUBLANES, CONV_WIDTH), prev(4)),
            pl.BlockSpec((SUBLANES, CONV_WIDTH), prev(5)),
            pl.BlockSpec((tm, d), row),
            pl.BlockSpec(w_out.shape, const),
            pl.BlockSpec(conv_w.shape, const),
            pl.BlockSpec(g_attn.shape, const),
            pl.BlockSpec(g_conv.shape, const),
            pl.BlockSpec(ln_g.shape, const),
            pl.BlockSpec(ln_b.shape, const)],
        out_specs=pl.BlockSpec((tm, d), row),
        out_shape=jax.ShapeDtypeStruct((n_tok, d), F32),
        compiler_params=_params("parallel"),
        name="mix_out_ln1",
    )(attn, proj, proj, proj, proj, proj, x, w_out, conv_w, g_attn, g_conv, ln_g, ln_b)


def _xattn_kernel(x_ref, wq_ref, k_ref, v_ref, wo_ref, lg_ref, lb_ref, o_ref):
    x = x_ref[...]
    d = x.shape[1]
    hd = d // N_MEM_HEADS
    q = _dot(x.astype(BF16), wq_ref[...]).astype(BF16)
    scale = hd ** -0.5
    outs = []
    for h in range(N_MEM_HEADS):
        sl = slice(h * hd, (h + 1) * hd)
        s = _dot_nt(q[:, sl], k_ref[:, sl]) * scale
        m = jnp.max(s, axis=-1, keepdims=True)
        p = jnp.exp(s - m)
        p = p / jnp.sum(p, axis=-1, keepdims=True)
        outs.append(_dot(p.astype(BF16), v_ref[:, sl]).astype(BF16))
    y = _dot(jnp.concatenate(outs, axis=1), wo_ref[...])
    o_ref[...] = _layer_norm(ALPHA * x + y, lg_ref[...], lb_ref[...])


def _xattn(x, w_q, kv, w_o, ln_g, ln_b, seq, mem_len):
    n_tok, d = x.shape
    tm = TM_XATTN
    row = lambda i: (i, 0)
    const = lambda i: (0, 0)
    tiles_per_seq = seq // tm
    return pl.pallas_call(
        _xattn_kernel,
        grid=(n_tok // tm,),
        in_specs=[pl.BlockSpec((tm, d), row),
                  pl.BlockSpec(w_q.shape, const),
                  pl.BlockSpec((mem_len, d), lambda i: (i // tiles_per_seq, 0)),
                  pl.BlockSpec((mem_len, d), lambda i: (i // tiles_per_seq, 1)),
                  pl.BlockSpec(w_o.shape, const),
                  pl.BlockSpec(ln_g.shape, const),
                  pl.BlockSpec(ln_b.shape, const)],
        out_specs=pl.BlockSpec((tm, d), row),
        out_shape=jax.ShapeDtypeStruct((n_tok, d), F32),
        compiler_params=_params("parallel"),
        name="xattn_ln2",
    )(x, w_q, kv, kv, w_o, ln_g, ln_b)


def _first_argmax(vals, index, sentinel):
    m = jnp.max(vals, axis=0, keepdims=True)
    i = jnp.min(jnp.where(vals == m, index, sentinel), axis=0, keepdims=True)
    return m, i


def _router_kernel(x_ref, wt_ref, bias_ref, idx_ref, gate_ref, rank_ref, cnt_ref):
    step = pl.program_id(0)
    tm = x_ref.shape[0]

    @pl.when(step == 0)
    def _():
        cnt_ref[...] = jnp.zeros_like(cnt_ref)

    logits = lax.dot_general(wt_ref[...], x_ref[...], (((1,), (1,)), ((), ())),
                             precision=lax.Precision.HIGHEST, preferred_element_type=F32)
    scores = 1.0 / (1.0 + jnp.exp(-logits))
    choice = scores + bias_ref[...]

    sub = lax.broadcasted_iota(I32, (GROUP_SIZE, tm), 0)
    group_scores = []
    for g in range(N_GROUPS):
        c = choice[g * GROUP_SIZE:(g + 1) * GROUP_SIZE, :]
        m1, i1 = _first_argmax(c, sub, GROUP_SIZE)
        m2 = jnp.max(jnp.where(sub == i1, -jnp.inf, c), axis=0, keepdims=True)
        group_scores.append(m1 + m2)
    gs = jnp.concatenate(group_scores, axis=0)
    gidx = lax.broadcasted_iota(I32, (N_GROUPS, tm), 0)
    gsel = jnp.zeros((N_GROUPS, tm), F32)
    for _ in range(TOPK_GROUPS):
        _, gi = _first_argmax(gs, gidx, N_GROUPS)
        hit = gidx == gi
        gsel = jnp.where(hit, 1.0, gsel)
        gs = jnp.where(hit, -jnp.inf, gs)
    masked = jnp.concatenate(
        [jnp.where(gsel[g:g + 1, :] > 0.0, choice[g * GROUP_SIZE:(g + 1) * GROUP_SIZE, :], -jnp.inf)
         for g in range(N_GROUPS)], axis=0)

    eidx = lax.broadcasted_iota(I32, (N_EXPERTS, tm), 0)
    hits, idxs, ws = [], [], []
    for _ in range(TOP_K):
        _, ei = _first_argmax(masked, eidx, N_EXPERTS)
        hit = eidx == ei
        hits.append(hit)
        idxs.append(ei)
        ws.append(jnp.sum(jnp.where(hit, scores, 0.0), axis=0, keepdims=True))
        masked = jnp.where(hit, -jnp.inf, masked)
    wsum = ws[0]
    for w in ws[1:]:
        wsum = wsum + w
    gate_ref[...] = jnp.concatenate([w / wsum * ROUTED_SCALE for w in ws], axis=0)
    idx_ref[...] = jnp.concatenate(idxs, axis=0)

    sel = hits[0]
    for hit in hits[1:]:
        sel = sel | hit
    self32 = jnp.where(sel, 1.0, 0.0)
    t_row = lax.broadcasted_iota(I32, (tm, tm), 0)
    t_col = lax.broadcasted_iota(I32, (tm, tm), 1)
    tri = jnp.where(t_row < t_col, 1.0, 0.0).astype(BF16)
    before = _dot(self32.astype(BF16), tri) + cnt_ref[:, 0:1]
    rank_ref[...] = jnp.concatenate(
        [jnp.sum(jnp.where(hit, before, 0.0), axis=0, keepdims=True) for hit in hits],
        axis=0).astype(I32)
    cnt_ref[...] += jnp.sum(self32, axis=1, keepdims=True)


def _router(x, w_router_t, bias_col):
    n_tok, d = x.shape
    tm = TM_ROUTER
    tok = lambda i: (0, i)
    return pl.pallas_call(
        _router_kernel,
        grid=(n_tok // tm,),
        in_specs=[pl.BlockSpec((tm, d), lambda i: (i, 0)),
                  pl.BlockSpec(w_router_t.shape, lambda i: (0, 0)),
                  pl.BlockSpec(bias_col.shape, lambda i: (0, 0))],
        out_specs=[pl.BlockSpec((TOP_K, tm), tok), pl.BlockSpec((TOP_K, tm), tok),
                   pl.BlockSpec((TOP_K, tm), tok),
                   pl.BlockSpec((N_EXPERTS, LANES), lambda i: (0, 0))],
        out_shape=[jax.ShapeDtypeStruct((TOP_K, n_tok), I32),
                   jax.ShapeDtypeStruct((TOP_K, n_tok), F32),
                   jax.ShapeDtypeStruct((TOP_K, n_tok), I32),
                   jax.ShapeDtypeStruct((N_EXPERTS, LANES), F32)],
        compiler_params=_params("arbitrary"),
        name="router",
    )(x, w_router_t, bias_col)


def _tile_view(x):
    r, d = x.shape
    return (x.reshape(r // SUBLANES, SUBLANES, d // LANES, LANES).transpose(0, 2, 1, 3)
            .reshape(r // SUBLANES, d // LANES, SUBLANES, 1, LANES))


def _row_view(v):
    g, c, s, _, l = v.shape
    return v.reshape(g, c, s, l).transpose(0, 2, 1, 3).reshape(g * s, c * l)


def _tile_row(view_ref, row):
    return view_ref.at[row >> 3, :, row & (SUBLANES - 1)]


def _dispatch_kernel(pad_ref, dest_ref, x_ref, xs_ref, zero_ref, sem):
    step = pl.program_id(0)
    groups = x_ref.shape[0]

    @pl.when(step == 0)
    def _():
        zero_ref[...] = jnp.zeros_like(zero_ref)

        def fill(e):
            return pltpu.make_async_copy(zero_ref, xs_ref.at[pl.ds(pad_ref[e] >> 3, ZERO_GROUPS)], sem)

        def start(e, c):
            fill(e).start()
            return c

        def wait(e, c):
            fill(e).wait()
            return c

        lax.fori_loop(0, N_EXPERTS, start, 0)
        lax.fori_loop(0, N_EXPERTS, wait, 0)

    def copies(g):
        return [pltpu.make_async_copy(x_ref.at[g, :, s], _tile_row(xs_ref, dest_ref[k, g * SUBLANES + s]), sem)
                for s in range(SUBLANES) for k in range(TOP_K)]

    def start_rows(g, c):
        for n, cp in enumerate(copies(g)):
            cp.start(priority=n % DMA_QUEUES)
        return c

    def wait_rows(g, c):
        for cp in copies(g):
            cp.wait()
        return c

    lax.fori_loop(0, groups, start_rows, 0)
    lax.fori_loop(0, groups, wait_rows, 0)


def _dispatch(pad_start, dest, x, n_rows):
    n_tok, w = x.shape
    tm = TM_DISPATCH
    chunks = w // LANES
    block = (tm // SUBLANES, chunks, SUBLANES, 1, LANES)
    total_groups = n_rows // SUBLANES + ZERO_GROUPS
    out = pl.pallas_call(
        _dispatch_kernel,
        grid_spec=pltpu.PrefetchScalarGridSpec(
            num_scalar_prefetch=1,
            grid=(n_tok // tm,),
            in_specs=[pl.BlockSpec((TOP_K, tm), lambda i, pad: (0, i), memory_space=pltpu.SMEM),
                      pl.BlockSpec(block, lambda i, pad: (i, 0, 0, 0, 0))],
            out_specs=pl.BlockSpec(memory_space=pl.ANY),
            scratch_shapes=[pltpu.VMEM((ZERO_GROUPS, chunks, SUBLANES, 1, LANES), F32),
                            pltpu.SemaphoreType.DMA(())]),
        out_shape=jax.ShapeDtypeStruct((total_groups, chunks, SUBLANES, 1, LANES), F32),
        compiler_params=_params("arbitrary"),
        name="dispatch",
    )(pad_start, dest, _tile_view(x))
    return _row_view(out)


def _expert_kernel(bexp_ref, nused_ref, next_ref, slot_ref, xs_ref, wg_hbm, wu_hbm, wd_hbm, ys_ref,
                   wg_f, wu_f, wd_f, wg_s, wu_s, wd_s, sem):
    i = pl.program_id(0)

    def fetch(e, slot):
        return [pltpu.make_async_copy(src.at[e], dst.at[slot], sem.at[slot])
                for src, dst in ((wg_hbm, wg_f), (wu_hbm, wu_f), (wd_hbm, wd_f))]

    @pl.when(i < nused_ref[0])
    def _():
        e = bexp_ref[i]
        e_prev = bexp_ref[jnp.maximum(i - 1, 0)]
        slot = slot_ref[e]

        @pl.when(i == 0)
        def _():
            for cp in fetch(e, slot):
                cp.start()

        @pl.when((i == 0) | (e != e_prev))
        def _():
            for cp in fetch(e, slot):
                cp.wait()
            wg_s[...] = wg_f[slot].astype(BF16)
            wu_s[...] = wu_f[slot].astype(BF16)
            wd_s[...] = wd_f[slot].astype(BF16)
            e_next = next_ref[e]

            @pl.when(e_next < N_EXPERTS)
            def _():
                for cp in fetch(e_next, 1 - slot):
                    cp.start(priority=DMA_QUEUES - 1)

        xb = xs_ref[...].astype(BF16)
        hmid = (_silu(_dot(xb, wg_s[...])) * _dot(xb, wu_s[...])).astype(BF16)
        ys_ref[...] = _dot(hmid, wd_s[...])


def _experts(block_exp, n_used, next_expert, slot, xs, w_gate, w_up, w_down, n_blocks):
    _, d, ff = w_gate.shape
    rows = lambda i, bexp, nused, nxt, slt: (jnp.minimum(i, nused[0] - 1), 0)
    hbm = pl.BlockSpec(memory_space=pl.ANY)
    return pl.pallas_call(
        _expert_kernel,
        grid_spec=pltpu.PrefetchScalarGridSpec(
            num_scalar_prefetch=4,
            grid=(n_blocks,),
            in_specs=[pl.BlockSpec((EXPERT_ROWS, d), rows), hbm, hbm, hbm],
            out_specs=pl.BlockSpec((EXPERT_ROWS, d), rows),
            scratch_shapes=[pltpu.VMEM((2, d, ff), F32), pltpu.VMEM((2, d, ff), F32),
                            pltpu.VMEM((2, ff, d), F32),
                            pltpu.VMEM((d, ff), BF16), pltpu.VMEM((d, ff), BF16),
                            pltpu.VMEM((ff, d), BF16),
                            pltpu.SemaphoreType.DMA((2,))]),
        out_shape=jax.ShapeDtypeStruct((n_blocks * EXPERT_ROWS, d), F32),
        compiler_params=_params("arbitrary"),
        name="experts",
    )(block_exp, n_used, next_expert, slot, xs, w_gate, w_up, w_down)


def _combine_kernel(dest_ref, gate_ref, x_ref, ys_ref, wsg_ref, wsu_ref, wsd_ref, lg_ref, lb_ref,
                    o_ref, buf, sem):
    tm, d = x_ref.shape
    groups, chunks = tm // SUBLANES, d // LANES
    dma_views = [buf.at[k].reshape(groups, chunks, SUBLANES, 1, LANES) for k in range(TOP_K)]
    load_views = [buf.at[k].reshape(groups, chunks, SUBLANES, LANES) for k in range(TOP_K)]

    def copies(g):
        return [pltpu.make_async_copy(_tile_row(ys_ref, dest_ref[k, g * SUBLANES + s]),
                                      dma_views[k].at[g, :, s], sem)
                for s in range(SUBLANES) for k in range(TOP_K)]

    def start_rows(g, c):
        for n, cp in enumerate(copies(g)):
            cp.start(priority=n % DMA_QUEUES)
        return c

    def wait_rows(g, c):
        for cp in copies(g):
            cp.wait()
        return c

    lax.fori_loop(0, groups, start_rows, 0)

    x = x_ref[...]
    xb = x.astype(BF16)
    hmid = (_silu(_dot(xb, wsg_ref[...])) * _dot(xb, wsu_ref[...])).astype(BF16)
    acc = ALPHA * x + _dot(hmid, wsd_ref[...])

    lax.fori_loop(0, groups, wait_rows, 0)
    gates = gate_ref[...]

    def rows_of(k):
        return jnp.concatenate([load_views[k][:, c].reshape(tm, LANES) for c in range(chunks)], axis=1)

    routed = rows_of(0) * gates[:, 0:1]
    for k in range(1, TOP_K):
        routed += rows_of(k) * gates[:, k:k + 1]
    o_ref[...] = _layer_norm(acc + routed, lg_ref[...], lb_ref[...])


def _combine(dest, gates_t, x, ys, ws_gate, ws_up, ws_down, ln_g, ln_b):
    n_tok, d = x.shape
    tm = TM_COMBINE
    row = lambda i: (i, 0)
    const = lambda i: (0, 0)
    return pl.pallas_call(
        _combine_kernel,
        grid=(n_tok // tm,),
        in_specs=[pl.BlockSpec((TOP_K, tm), lambda i: (0, i), memory_space=pltpu.SMEM),
                  pl.BlockSpec((tm, TOP_K), row),
                  pl.BlockSpec((tm, d), row),
                  pl.BlockSpec(memory_space=pl.ANY),
                  pl.BlockSpec(ws_gate.shape, const),
                  pl.BlockSpec(ws_up.shape, const),
                  pl.BlockSpec(ws_down.shape, const),
                  pl.BlockSpec(ln_g.shape, const),
                  pl.BlockSpec(ln_b.shape, const)],
        out_specs=pl.BlockSpec((tm, d), row),
        out_shape=jax.ShapeDtypeStruct((n_tok, d), F32),
        scratch_shapes=[pltpu.VMEM((TOP_K, tm * (d // LANES), LANES), F32), pltpu.SemaphoreType.DMA(())],
        compiler_params=_params("arbitrary"),
        name="combine_shared_ln3",
    )(dest, gates_t, x, _tile_view(ys), ws_gate, ws_up, ws_down, ln_g, ln_b)


def _layer(x, mem, w_in, conv_w, g_attn_out, g_conv_out, w_out, ln1_g, ln1_b, w_q_mem, w_kv_mem,
           w_o_mem, ln2_g, ln2_b, w_router, router_bias, w_gate, w_up, w_down, ws_gate, ws_up,
           ws_down, ln3_g, ln3_b):
    batch, seq, d = x.shape
    mem_len = mem.shape[1]
    n_tok = batch * seq
    xf = x.reshape(n_tok, d)
    row = lambda v: v.reshape(1, -1)

    proj = _matmul(xf, w_in.astype(BF16), F32, TM_PROJ, TN_PROJ)
    attn = _dilated_attention(proj, batch, seq)
    x1 = _mix_out(attn, proj, xf, w_out.astype(BF16), conv_w, row(g_attn_out), row(g_conv_out),
                  row(ln1_g), row(ln1_b), seq)

    kv = _matmul(mem.reshape(batch * mem_len, d), w_kv_mem.astype(BF16), BF16,
                 mem_len, 1024)
    x2 = _xattn(x1, w_q_mem.astype(BF16), kv, w_o_mem.astype(BF16), row(ln2_g), row(ln2_b), seq,
                mem_len)

    idx, gates, rank, cnt = _router(x2, w_router.T, router_bias.reshape(N_EXPERTS, 1))
    counts = cnt[:, 0].astype(I32)
    padded = (counts + EXPERT_ROWS - 1) // EXPERT_ROWS * EXPERT_ROWS
    seg_end = jnp.cumsum(padded)
    seg_start = seg_end - padded
    experts = jnp.arange(N_EXPERTS, dtype=I32)
    dest = rank + jnp.sum(jnp.where(idx[..., None] == experts, seg_start, 0), axis=-1)
    n_blocks = -(-(n_tok * TOP_K) // EXPERT_ROWS) + N_EXPERTS
    block_first = jnp.arange(n_blocks, dtype=I32) * EXPERT_ROWS
    block_exp = jnp.minimum(jnp.sum(block_first[:, None] >= seg_end[None, :], axis=1),
                            N_EXPERTS - 1).astype(I32)
    n_used = (seg_end[-1:] // EXPERT_ROWS).astype(I32)

    xs = _dispatch(seg_start + counts, dest, x2, n_blocks * EXPERT_ROWS)
    nonempty = counts > 0
    first_from = lax.cummin(jnp.where(nonempty, experts, N_EXPERTS), reverse=True)
    next_expert = jnp.concatenate([first_from[1:], jnp.full((1,), N_EXPERTS, I32)])
    slot = (jnp.cumsum(nonempty.astype(I32)) - 1) & 1
    ys = _experts(block_exp, n_used, next_expert, slot, xs, w_gate, w_up, w_down, n_blocks)
    out = _combine(dest, gates.T, x2, ys, ws_gate.astype(BF16), ws_up.astype(BF16),
                   ws_down.astype(BF16), row(ln3_g), row(ln3_b))
    return out.reshape(batch, seq, d)


def kernel(x, mem, w_in, conv_w, g_attn_out, g_conv_out, w_out, ln1_g, ln1_b, w_q_mem, w_kv_mem, w_o_mem, ln2_g, ln2_b, w_router, router_bias, w_gate, w_up, w_down, ws_gate, ws_up, ws_down, ln3_g, ln3_b):
    for l in range(DEPTH):
        x = _layer(x, mem, w_in[l], conv_w[l], g_attn_out[l], g_conv_out[l], w_out[l], ln1_g[l],
                   ln1_b[l], w_q_mem[l], w_kv_mem[l], w_o_mem[l], ln2_g[l], ln2_b[l], w_router[l],
                   router_bias[l], w_gate[l], w_up[l], w_down[l], ws_gate[l], ws_up[l], ws_down[l],
                   ln3_g[l], ln3_b[l])
    return x
```

```python
import functools

import jax
import jax.numpy as jnp
from jax import lax
from jax.experimental import pallas as pl
from jax.experimental.pallas import tpu as pltpu

F32 = jnp.float32
BF16 = jnp.bfloat16
I32 = jnp.int32

HEAD_DIM = 128
N_ATTN_HEADS = 8
ATTN_WIDTH = N_ATTN_HEADS * HEAD_DIM
CONV_WIDTH = 1024
PROJ_PARTS = 6
DILATED_PATTERNS = ((128, 1), (512, 4), (2048, 16))
ATTN_BLOCK = 128
ATTN_SPAN = 16 * ATTN_BLOCK
N_MEM_HEADS = 4
N_EXPERTS = 64
TOP_K = 8
N_GROUPS = 8
GROUP_SIZE = N_EXPERTS // N_GROUPS
TOPK_GROUPS = 4
ROUTED_SCALE = 2.5
LN_EPS = 1e-5
RMS_EPS = 1e-6
DEPTH = 1
ALPHA = (2.0 * DEPTH) ** 0.25

LANES = 128
SUBLANES = 8
VMEM_LIMIT = 56 * 1024 * 1024
DMA_QUEUES = 2
EXPERT_ROWS = 256
ZERO_GROUPS = EXPERT_ROWS // SUBLANES + 1
TM_PROJ, TN_PROJ = 1024, 512
TM_MIX = 256
TM_XATTN = 256
TM_ROUTER = 512
TM_DISPATCH = 256
TM_COMBINE = 128


def _params(*semantics):
    return pltpu.CompilerParams(dimension_semantics=semantics, vmem_limit_bytes=VMEM_LIMIT)


def _layer_norm(r, g, b):
    mu = jnp.mean(r, axis=-1, keepdims=True)
    c = r - mu
    var = jnp.mean(c * c, axis=-1, keepdims=True)
    return c * lax.rsqrt(var + LN_EPS) * g + b


def _rms_norm(v, g):
    return v * lax.rsqrt(jnp.mean(v * v, axis=-1, keepdims=True) + RMS_EPS) * g


def _dot(a, b):
    return jnp.dot(a, b, preferred_element_type=F32)


def _dot_nt(a, b):
    return lax.dot_general(a, b, (((1,), (1,)), ((), ())), preferred_element_type=F32)


def _silu(v):
    return v / (1.0 + jnp.exp(-v))


def _matmul_kernel(a_ref, w_ref, o_ref, a_bf):
    @pl.when(pl.program_id(1) == 0)
    def _():
        a_bf[...] = a_ref[...].astype(BF16)

    o_ref[...] = _dot(a_bf[...], w_ref[...]).astype(o_ref.dtype)


def _matmul(a, w, out_dtype, tm, tn):
    m, k = a.shape
    n = w.shape[1]
    return pl.pallas_call(
        _matmul_kernel,
        grid=(m // tm, n // tn),
        in_specs=[pl.BlockSpec((tm, k), lambda i, j: (i, 0)),
                  pl.BlockSpec((k, tn), lambda i, j: (0, j))],
        out_specs=pl.BlockSpec((tm, tn), lambda i, j: (i, j)),
        out_shape=jax.ShapeDtypeStruct((m, n), out_dtype),
        scratch_shapes=[pltpu.VMEM((tm, k), BF16)],
        compiler_params=_params("parallel", "arbitrary"),
        name="matmul",
    )(a, w)


def _attn_kernel(q_ref, kp_ref, kc_ref, vp_ref, vc_ref, o_ref, kk, vv, o_acc, l_acc):
    j = pl.program_id(1)
    blk, span = ATTN_BLOCK, ATTN_SPAN
    kk[0:span, :] = kp_ref[...]
    kk[span:, :] = kc_ref[...]
    vv[0:span, :] = vp_ref[...]
    vv[span:, :] = vc_ref[...]

    row = lax.broadcasted_iota(I32, (blk, 2 * blk), 0)
    col = lax.broadcasted_iota(I32, (blk, 2 * blk), 1)
    dist = row + blk - col
    in_window = (dist >= 0) & (dist <= blk)
    own_block = col >= blk
    scale = HEAD_DIM ** -0.5

    order = sorted(range(len(DILATED_PATTERNS)), key=lambda i: -DILATED_PATTERNS[i][1])
    assert DILATED_PATTERNS[order[-1]][1] == 1
    sub_blocks = span // blk
    for bi in order:
        dil = DILATED_PATTERNS[bi][1]
        shift = dil.bit_length() - 1

        def body(t, carry, bi=bi, dil=dil, shift=shift):
            r = t & (dil - 1)
            n = t >> shift
            base = n * (blk * dil) + r
            if dil == 1:
                base = pl.multiple_of(base, blk)
                q_rows, kv_rows = pl.ds(base, blk), pl.ds(span + base - blk, 2 * blk)
            else:
                q_rows = pl.ds(base, blk, stride=dil)
                kv_rows = pl.ds(span + base - blk * dil, 2 * blk, stride=dil)
            q = q_ref[q_rows, :].astype(BF16)
            k = kk[kv_rows, :].astype(BF16)
            v = vv[kv_rows, :].astype(BF16)
            s = _dot_nt(q, k) * scale
            valid = in_window & (own_block | (j > 0) | (n > 0))
            s = jnp.where(valid, s, -jnp.inf)
            m = jnp.max(s, axis=-1, keepdims=True)
            p = jnp.exp(s - m)
            l = jnp.sum(p, axis=-1, keepdims=True)
            o = _dot((p / l).astype(BF16), v)
            lse = jnp.broadcast_to(m + jnp.log(l), (blk, LANES))
            if dil != 1:
                o_acc[bi, q_rows, :] = o
                l_acc[bi, q_rows, :] = lse
                return carry
            others = [i for i in order if i != bi]
            lses = [lse] + [l_acc[i, q_rows, :] for i in others]
            outs = [o] + [o_acc[i, q_rows, :] for i in others]
            top = jnp.maximum(jnp.maximum(lses[0], lses[1]), lses[2])
            es = [jnp.exp(v_ - top) for v_ in lses]
            den = es[0] + es[1] + es[2]
            o_ref[q_rows, :] = ((es[0] / den) * outs[0] + (es[1] / den) * outs[1] + (es[2] / den) * outs[2])
            return carry

        lax.fori_loop(0, sub_blocks, body, 0, unroll=True)


def _dilated_attention(proj, batch, seq):
    n_tok, _ = proj.shape
    span = ATTN_SPAN
    spans = seq // span
    heads = N_ATTN_HEADS

    def spec(part, prev):
        def index(b, j, h):
            return (b * spans + (jnp.maximum(j - 1, 0) if prev else j), part * heads + h)
        return pl.BlockSpec((span, HEAD_DIM), index)

    return pl.pallas_call(
        _attn_kernel,
        grid=(batch, spans, heads),
        in_specs=[spec(0, False), spec(1, True), spec(1, False), spec(2, True), spec(2, False)],
        out_specs=pl.BlockSpec((span, HEAD_DIM), lambda b, j, h: (b * spans + j, h)),
        out_shape=jax.ShapeDtypeStruct((n_tok, ATTN_WIDTH), F32),
        scratch_shapes=[pltpu.VMEM((2 * span, HEAD_DIM), F32), pltpu.VMEM((2 * span, HEAD_DIM), F32),
                        pltpu.VMEM((len(DILATED_PATTERNS), span, HEAD_DIM), F32),
                        pltpu.VMEM((len(DILATED_PATTERNS), span, HEAD_DIM), F32)],
        compiler_params=_params("parallel", "parallel", "parallel"),
        name="dilated_attn",
    )(proj, proj, proj, proj, proj)


def _mix_out_kernel(a_ref, b_ref, c_ref, h_ref, cp_ref, hp_ref, x_ref,
                    w_ref, cw_ref, ga_ref, gc_ref, lg_ref, lb_ref, o_ref, *, tiles_per_seq):
    i = pl.program_id(0)
    z = c_ref[...] * h_ref[...]
    zp = jnp.where(i % tiles_per_seq == 0, 0.0, cp_ref[...] * hp_ref[...])
    zz = jnp.concatenate([zp, z], axis=0)
    z1 = pltpu.roll(zz, 1, axis=0)[SUBLANES:]
    z2 = pltpu.roll(zz, 2, axis=0)[SUBLANES:]
    conv = b_ref[...] * (cw_ref[0:1, :] * z2 + cw_ref[1:2, :] * z1 + cw_ref[2:3, :] * z)

    mixed = jnp.concatenate([_rms_norm(a_ref[...], ga_ref[...]), _rms_norm(conv, gc_ref[...])], axis=1)
    y = _dot(mixed.astype(BF16), w_ref[...])
    o_ref[...] = _layer_norm(ALPHA * x_ref[...] + y, lg_ref[...], lb_ref[...])


def _mix_out(attn, proj, x, w_out, conv_w, g_attn, g_conv, ln_g, ln_b, seq):
    n_tok, d = x.shape
    tm = TM_MIX
    rows8 = tm // SUBLANES
    row = lambda i: (i, 0)
    const = lambda i: (0, 0)
    prev = lambda part: (lambda i: (jnp.maximum(i * rows8 - 1, 0), part))
    return pl.pallas_call(
        functools.partial(_mix_out_kernel, tiles_per_seq=seq // tm),
        grid=(n_tok // tm,),
        in_specs=[
            pl.BlockSpec((tm, ATTN_WIDTH), row),
            pl.BlockSpec((tm, CONV_WIDTH), lambda i: (i, 3)),
            pl.BlockSpec((tm, CONV_WIDTH), lambda i: (i, 4)),
            pl.BlockSpec((tm, CONV_WIDTH), lambda i: (i, 5)),
            pl.BlockSpec((SUBLANES, CONV_WIDTH), prev(4)),
            pl.BlockSpec((SUBLANES, CONV_WIDTH), prev(5)),
            pl.BlockSpec((tm, d), row),
            pl.BlockSpec(w_out.shape, const),
            pl.BlockSpec(conv_w.shape, const),
            pl.BlockSpec(g_attn.shape, const),
            pl.BlockSpec(g_conv.shape, const),
            pl.BlockSpec(ln_g.shape, const),
            pl.BlockSpec(ln_b.shape, const)],
        out_specs=pl.BlockSpec((tm, d), row),
        out_shape=jax.ShapeDtypeStruct((n_tok, d), F32),
        compiler_params=_params("parallel"),
        name="mix_out_ln1",
    )(attn, proj, proj, proj, proj, proj, x, w_out, conv_w, g_attn, g_conv, ln_g, ln_b)


def _xattn_kernel(x_ref, wq_ref, k_ref, v_ref, wo_ref, lg_ref, lb_ref, o_ref):
    x = x_ref[...]
    d = x.shape[1]
    hd = d // N_MEM_HEADS
    q = _dot(x.astype(BF16), wq_ref[...]).astype(BF16)
    scale = hd ** -0.5
    outs = []
    for h in range(N_MEM_HEADS):
        sl = slice(h * hd, (h + 1) * hd)
        s = _dot_nt(q[:, sl], k_ref[:, sl]) * scale
        m = jnp.max(s, axis=-1, keepdims=True)
        p = jnp.exp(s - m)
        p = p / jnp.sum(p, axis=-1, keepdims=True)
        outs.append(_dot(p.astype(BF16), v_ref[:, sl]).astype(BF16))
    y = _dot(jnp.concatenate(outs, axis=1), wo_ref[...])
    o_ref[...] = _layer_norm(ALPHA * x + y, lg_ref[...], lb_ref[...])


def _xattn(x, w_q, kv, w_o, ln_g, ln_b, seq, mem_len):
    n_tok, d = x.shape
    tm = TM_XATTN
    row = lambda i: (i, 0)
    const = lambda i: (0, 0)
    tiles_per_seq = seq // tm
    return pl.pallas_call(
        _xattn_kernel,
        grid=(n_tok // tm,),
        in_specs=[pl.BlockSpec((tm, d), row),
                  pl.BlockSpec(w_q.shape, const),
                  pl.BlockSpec((mem_len, d), lambda i: (i // tiles_per_seq, 0)),
                  pl.BlockSpec((mem_len, d), lambda i: (i // tiles_per_seq, 1)),
                  pl.BlockSpec(w_o.shape, const),
                  pl.BlockSpec(ln_g.shape, const),
                  pl.BlockSpec(ln_b.shape, const)],
        out_specs=pl.BlockSpec((tm, d), row),
        out_shape=jax.ShapeDtypeStruct((n_tok, d), F32),
        compiler_params=_params("parallel"),
        name="xattn_ln2",
    )(x, w_q, kv, kv, w_o, ln_g, ln_b)


def _first_argmax(vals, index, sentinel):
    m = jnp.max(vals, axis=0, keepdims=True)
    i = jnp.min(jnp.where(vals == m, index, sentinel), axis=0, keepdims=True)
    return m, i


def _router_kernel(x_ref, wt_ref, bias_ref, idx_ref, gate_ref, rank_ref, cnt_ref):
    step = pl.program_id(0)
    tm = x_ref.shape[0]

    @pl.when(step == 0)
    def _():
        cnt_ref[...] = jnp.zeros_like(cnt_ref)

    x = x_ref[...]
    xh = x.astype(BF16)
    xl = (x - xh.astype(F32)).astype(BF16)
    w = wt_ref[...]
    wh = w.astype(BF16)
    wl = (w - wh.astype(F32)).astype(BF16)
    both = _dot_nt(jnp.concatenate([wh, wl], axis=0), xh)
    logits = both[:N_EXPERTS] + both[N_EXPERTS:] + _dot_nt(wh, xl)
    scores = 1.0 / (1.0 + jnp.exp(-logits))
    choice = scores + bias_ref[...]

    sub = lax.broadcasted_iota(I32, (GROUP_SIZE, tm), 0)
    group_scores = []
    for g in range(N_GROUPS):
        c = choice[g * GROUP_SIZE:(g + 1) * GROUP_SIZE, :]
        m1, i1 = _first_argmax(c, sub, GROUP_SIZE)
        m2 = jnp.max(jnp.where(sub == i1, -jnp.inf, c), axis=0, keepdims=True)
        group_scores.append(m1 + m2)
    gs = jnp.concatenate(group_scores, axis=0)
    gidx = lax.broadcasted_iota(I32, (N_GROUPS, tm), 0)
    gsel = jnp.zeros((N_GROUPS, tm), F32)
    for _ in range(TOPK_GROUPS):
        _, gi = _first_argmax(gs, gidx, N_GROUPS)
        hit = gidx == gi
        gsel = jnp.where(hit, 1.0, gsel)
        gs = jnp.where(hit, -jnp.inf, gs)
    masked = jnp.concatenate(
        [jnp.where(gsel[g:g + 1, :] > 0.0, choice[g * GROUP_SIZE:(g + 1) * GROUP_SIZE, :], -jnp.inf)
         for g in range(N_GROUPS)], axis=0)

    eidx = lax.broadcasted_iota(I32, (N_EXPERTS, tm), 0)
    hits, idxs, ws = [], [], []
    for _ in range(TOP_K):
        _, ei = _first_argmax(masked, eidx, N_EXPERTS)
        hit = eidx == ei
        hits.append(hit)
        idxs.append(ei)
        ws.append(jnp.sum(jnp.where(hit, scores, 0.0), axis=0, keepdims=True))
        masked = jnp.where(hit, -jnp.inf, masked)
    wsum = ws[0]
    for w in ws[1:]:
        wsum = wsum + w
    gate_ref[...] = jnp.concatenate([w / wsum * ROUTED_SCALE for w in ws], axis=0)
    idx_ref[...] = jnp.concatenate(idxs, axis=0)

    sel = hits[0]
    for hit in hits[1:]:
        sel = sel | hit
    self32 = jnp.where(sel, 1.0, 0.0)
    t_row = lax.broadcasted_iota(I32, (tm, tm), 0)
    t_col = lax.broadcasted_iota(I32, (tm, tm), 1)
    tri = jnp.where(t_row < t_col, 1.0, 0.0).astype(BF16)
    before = _dot(self32.astype(BF16), tri) + cnt_ref[:, 0:1]
    rank_ref[...] = jnp.concatenate(
        [jnp.sum(jnp.where(hit, before, 0.0), axis=0, keepdims=True) for hit in hits],
        axis=0).astype(I32)
    cnt_ref[...] += jnp.sum(self32, axis=1, keepdims=True)


def _router(x, w_router_t, bias_col):
    n_tok, d = x.shape
    tm = TM_ROUTER
    tok = lambda i: (0, i)
    return pl.pallas_call(
        _router_kernel,
        grid=(n_tok // tm,),
        in_specs=[pl.BlockSpec((tm, d), lambda i: (i, 0)),
                  pl.BlockSpec(w_router_t.shape, lambda i: (0, 0)),
                  pl.BlockSpec(bias_col.shape, lambda i: (0, 0))],
        out_specs=[pl.BlockSpec((TOP_K, tm), tok), pl.BlockSpec((TOP_K, tm), tok),
                   pl.BlockSpec((TOP_K, tm), tok),
                   pl.BlockSpec((N_EXPERTS, LANES), lambda i: (0, 0))],
        out_shape=[jax.ShapeDtypeStruct((TOP_K, n_tok), I32),
                   jax.ShapeDtypeStruct((TOP_K, n_tok), F32),
                   jax.ShapeDtypeStruct((TOP_K, n_tok), I32),
                   jax.ShapeDtypeStruct((N_EXPERTS, LANES), F32)],
        compiler_params=_params("arbitrary"),
        name="router",
    )(x, w_router_t, bias_col)


def _tile_view(x):
    r, d = x.shape
    return (x.reshape(r // SUBLANES, SUBLANES, d // LANES, LANES).transpose(0, 2, 1, 3)
            .reshape(r // SUBLANES, d // LANES, SUBLANES, 1, LANES))


def _row_view(v):
    g, c, s, _, l = v.shape
    return v.reshape(g, c, s, l).transpose(0, 2, 1, 3).reshape(g * s, c * l)


def _tile_row(view_ref, row):
    return view_ref.at[row >> 3, :, row & (SUBLANES - 1)]


def _dispatch_kernel(pad_ref, dest_ref, x_ref, xs_ref, zero_ref, sem):
    step = pl.program_id(0)
    groups = x_ref.shape[0]

    @pl.when(step == 0)
    def _():
        zero_ref[...] = jnp.zeros_like(zero_ref)

        def fill(e):
            return pltpu.make_async_copy(zero_ref, xs_ref.at[pl.ds(pad_ref[e] >> 3, ZERO_GROUPS)], sem)

        def start(e, c):
            fill(e).start()
            return c

        def wait(e, c):
            fill(e).wait()
            return c

        lax.fori_loop(0, N_EXPERTS, start, 0)
        lax.fori_loop(0, N_EXPERTS, wait, 0)

    def copies(g):
        return [pltpu.make_async_copy(x_ref.at[g, :, s], _tile_row(xs_ref, dest_ref[k, g * SUBLANES + s]), sem)
                for s in range(SUBLANES) for k in range(TOP_K)]

    def start_rows(g, c):
        for n, cp in enumerate(copies(g)):
            cp.start(priority=n % DMA_QUEUES)
        return c

    def wait_rows(g, c):
        for cp in copies(g):
            cp.wait()
        return c

    lax.fori_loop(0, groups, start_rows, 0)
    lax.fori_loop(0, groups, wait_rows, 0)


def _dispatch(pad_start, dest, x, n_rows):
    n_tok, w = x.shape
    tm = TM_DISPATCH
    chunks = w // LANES
    block = (tm // SUBLANES, chunks, SUBLANES, 1, LANES)
    total_groups = n_rows // SUBLANES + ZERO_GROUPS
    out = pl.pallas_call(
        _dispatch_kernel,
        grid_spec=pltpu.PrefetchScalarGridSpec(
            num_scalar_prefetch=1,
            grid=(n_tok // tm,),
            in_specs=[pl.BlockSpec((TOP_K, tm), lambda i, pad: (0, i), memory_space=pltpu.SMEM),
                      pl.BlockSpec(block, lambda i, pad: (i, 0, 0, 0, 0))],
            out_specs=pl.BlockSpec(memory_space=pl.ANY),
            scratch_shapes=[pltpu.VMEM((ZERO_GROUPS, chunks, SUBLANES, 1, LANES), F32),
                            pltpu.SemaphoreType.DMA(())]),
        out_shape=jax.ShapeDtypeStruct((total_groups, chunks, SUBLANES, 1, LANES), F32),
        compiler_params=_params("arbitrary"),
        name="dispatch",
    )(pad_start, dest, _tile_view(x))
    return _row_view(out)


def _expert_kernel(bexp_ref, nused_ref, next_ref, slot_ref, xs_ref, wg_hbm, wu_hbm, wd_hbm, ys_ref,
                   wg_f, wu_f, wd_f, wg_s, wu_s, wd_s, sem):
    i = pl.program_id(0)

    def fetch(e, slot):
        return [pltpu.make_async_copy(src.at[e], dst.at[slot], sem.at[slot])
                for src, dst in ((wg_hbm, wg_f), (wu_hbm, wu_f), (wd_hbm, wd_f))]

    @pl.when(i < nused_ref[0])
    def _():
        e = bexp_ref[i]
        e_prev = bexp_ref[jnp.maximum(i - 1, 0)]
        slot = slot_ref[e]

        @pl.when(i == 0)
        def _():
            for cp in fetch(e, slot):
                cp.start()

        @pl.when((i == 0) | (e != e_prev))
        def _():
            for cp in fetch(e, slot):
                cp.wait()
            wg_s[...] = wg_f[slot].astype(BF16)
            wu_s[...] = wu_f[slot].astype(BF16)
            wd_s[...] = wd_f[slot].astype(BF16)
            e_next = next_ref[e]

            @pl.when(e_next < N_EXPERTS)
            def _():
                for cp in fetch(e_next, 1 - slot):
                    cp.start(priority=DMA_QUEUES - 1)

        xb = xs_ref[...].astype(BF16)
        hmid = (_silu(_dot(xb, wg_s[...])) * _dot(xb, wu_s[...])).astype(BF16)
        ys_ref[...] = _dot(hmid, wd_s[...])


def _experts(block_exp, n_used, next_expert, slot, xs, w_gate, w_up, w_down, n_blocks):
    _, d, ff = w_gate.shape
    rows = lambda i, bexp, nused, nxt, slt: (jnp.minimum(i, nused[0] - 1), 0)
    hbm = pl.BlockSpec(memory_space=pl.ANY)
    return pl.pallas_call(
        _expert_kernel,
        grid_spec=pltpu.PrefetchScalarGridSpec(
            num_scalar_prefetch=4,
            grid=(n_blocks,),
            in_specs=[pl.BlockSpec((EXPERT_ROWS, d), rows), hbm, hbm, hbm],
            out_specs=pl.BlockSpec((EXPERT_ROWS, d), rows),
            scratch_shapes=[pltpu.VMEM((2, d, ff), F32), pltpu.VMEM((2, d, ff), F32),
                            pltpu.VMEM((2, ff, d), F32),
                            pltpu.VMEM((d, ff), BF16), pltpu.VMEM((d, ff), BF16),
                            pltpu.VMEM((ff, d), BF16),
                            pltpu.SemaphoreType.DMA((2,))]),
        out_shape=jax.ShapeDtypeStruct((n_blocks * EXPERT_ROWS, d), F32),
        compiler_params=_params("arbitrary"),
        name="experts",
    )(block_exp, n_used, next_expert, slot, xs, w_gate, w_up, w_down)


def _combine_kernel(dest_ref, gate_ref, x_ref, ys_ref, wsg_ref, wsu_ref, wsd_ref, lg_ref, lb_ref,
                    o_ref, buf, sem):
    tm, d = x_ref.shape
    groups, chunks = tm // SUBLANES, d // LANES
    dma_views = [buf.at[k].reshape(groups, chunks, SUBLANES, 1, LANES) for k in range(TOP_K)]
    load_views = [buf.at[k].reshape(groups, chunks, SUBLANES, LANES) for k in range(TOP_K)]

    def copies(g):
        return [pltpu.make_async_copy(_tile_row(ys_ref, dest_ref[k, g * SUBLANES + s]),
                                      dma_views[k].at[g, :, s], sem)
                for s in range(SUBLANES) for k in range(TOP_K)]

    def start_rows(g, c):
        for n, cp in enumerate(copies(g)):
            cp.start(priority=n % DMA_QUEUES)
        return c

    def wait_rows(g, c):
        for cp in copies(g):
            cp.wait()
        return c

    lax.fori_loop(0, groups, start_rows, 0)

    x = x_ref[...]
    xb = x.astype(BF16)
    hmid = (_silu(_dot(xb, wsg_ref[...])) * _dot(xb, wsu_ref[...])).astype(BF16)
    acc = ALPHA * x + _dot(hmid, wsd_ref[...])

    lax.fori_loop(0, groups, wait_rows, 0)
    gates = gate_ref[...]

    def rows_of(k):
        return jnp.concatenate([load_views[k][:, c].reshape(tm, LANES) for c in range(chunks)], axis=1)

    routed = rows_of(0) * gates[:, 0:1]
    for k in range(1, TOP_K):
        routed += rows_of(k) * gates[:, k:k + 1]
    o_ref[...] = _layer_norm(acc + routed, lg_ref[...], lb_ref[...])


def _combine(dest, gates_t, x, ys, ws_gate, ws_up, ws_down, ln_g, ln_b):
    n_tok, d = x.shape
    tm = TM_COMBINE
    row = lambda i: (i, 0)
    const = lambda i: (0, 0)
    return pl.pallas_call(
        _combine_kernel,
        grid=(n_tok // tm,),
        in_specs=[pl.BlockSpec((TOP_K, tm), lambda i: (0, i), memory_space=pltpu.SMEM),
                  pl.BlockSpec((tm, TOP_K), row),
                  pl.BlockSpec((tm, d), row),
                  pl.BlockSpec(memory_space=pl.ANY),
                  pl.BlockSpec(ws_gate.shape, const),
                  pl.BlockSpec(ws_up.shape, const),
                  pl.BlockSpec(ws_down.shape, const),
                  pl.BlockSpec(ln_g.shape, const),
                  pl.BlockSpec(ln_b.shape, const)],
        out_specs=pl.BlockSpec((tm, d), row),
        out_shape=jax.ShapeDtypeStruct((n_tok, d), F32),
        scratch_shapes=[pltpu.VMEM((TOP_K, tm * (d // LANES), LANES), F32), pltpu.SemaphoreType.DMA(())],
        compiler_params=_params("arbitrary"),
        name="combine_shared_ln3",
    )(dest, gates_t, x, _tile_view(ys), ws_gate, ws_up, ws_down, ln_g, ln_b)


def _layer(x, mem, w_in, conv_w, g_attn_out, g_conv_out, w_out, ln1_g, ln1_b, w_q_mem, w_kv_mem,
           w_o_mem, ln2_g, ln2_b, w_router, router_bias, w_gate, w_up, w_down, ws_gate, ws_up,
           ws_down, ln3_g, ln3_b):
    batch, seq, d = x.shape
    mem_len = mem.shape[1]
    n_tok = batch * seq
    xf = x.reshape(n_tok, d)
    row = lambda v: v.reshape(1, -1)

    proj = _matmul(xf, w_in.astype(BF16), F32, TM_PROJ, TN_PROJ)
    attn = _dilated_attention(proj, batch, seq)
    x1 = _mix_out(attn, proj, xf, w_out.astype(BF16), conv_w, row(g_attn_out), row(g_conv_out),
                  row(ln1_g), row(ln1_b), seq)

    kv = _matmul(mem.reshape(batch * mem_len, d), w_kv_mem.astype(BF16), BF16,
                 mem_len, 1024)
    x2 = _xattn(x1, w_q_mem.astype(BF16), kv, w_o_mem.astype(BF16), row(ln2_g), row(ln2_b), seq,
                mem_len)

    idx, gates, rank, cnt = _router(x2, w_router.T, router_bias.reshape(N_EXPERTS, 1))
    counts = cnt[:, 0].astype(I32)
    padded = (counts + EXPERT_ROWS - 1) // EXPERT_ROWS * EXPERT_ROWS
    seg_end = jnp.cumsum(padded)
    seg_start = seg_end - padded
    experts = jnp.arange(N_EXPERTS, dtype=I32)
    dest = rank + jnp.sum(jnp.where(idx[..., None] == experts, seg_start, 0), axis=-1)
    n_blocks = -(-(n_tok * TOP_K) // EXPERT_ROWS) + N_EXPERTS
    block_first = jnp.arange(n_blocks, dtype=I32) * EXPERT_ROWS
    block_exp = jnp.minimum(jnp.sum(block_first[:, None] >= seg_end[None, :], axis=1),
                            N_EXPERTS - 1).astype(I32)
    n_used = (seg_end[-1:] // EXPERT_ROWS).astype(I32)

    xs = _dispatch(seg_start + counts, dest, x2, n_blocks * EXPERT_ROWS)
    nonempty = counts > 0
    first_from = lax.cummin(jnp.where(nonempty, experts, N_EXPERTS), reverse=True)
    next_expert = jnp.concatenate([first_from[1:], jnp.full((1,), N_EXPERTS, I32)])
    slot = (jnp.cumsum(nonempty.astype(I32)) - 1) & 1
    ys = _experts(block_exp, n_used, next_expert, slot, xs, w_gate, w_up, w_down, n_blocks)
    out = _combine(dest, gates.T, x2, ys, ws_gate.astype(BF16), ws_up.astype(BF16),
                   ws_down.astype(BF16), row(ln3_g), row(ln3_b))
    return out.reshape(batch, seq, d)


def kernel(x, mem, w_in, conv_w, g_attn_out, g_conv_out, w_out, ln1_g, ln1_b, w_q_mem, w_kv_mem, w_o_mem, ln2_g, ln2_b, w_router, router_bias, w_gate, w_up, w_down, ws_gate, ws_up, ws_down, ln3_g, ln3_b):
    for l in range(DEPTH):
        x = _layer(x, mem, w_in[l], conv_w[l], g_attn_out[l], g_conv_out[l], w_out[l], ln1_g[l],
                   ln1_b[l], w_q_mem[l], w_kv_mem[l], w_o_mem[l], ln2_g[l], ln2_b[l], w_router[l],
                   router_bias[l], w_gate[l], w_up[l], w_down[l], ws_gate[l], ws_up[l], ws_down[l],
                   ln3_g[l], ln3_b[l])
    return x
```

```python
import functools

import jax
import jax.numpy as jnp
from jax import lax
from jax.experimental import pallas as pl
from jax.experimental.pallas import tpu as pltpu

F32 = jnp.float32
BF16 = jnp.bfloat16
I32 = jnp.int32

HEAD_DIM = 128
N_ATTN_HEADS = 8
ATTN_WIDTH = N_ATTN_HEADS * HEAD_DIM
CONV_WIDTH = 1024
PROJ_PARTS = 6
DILATED_PATTERNS = ((128, 1), (512, 4), (2048, 16))
ATTN_BLOCK = 128
ATTN_SPAN = 16 * ATTN_BLOCK
N_MEM_HEADS = 4
N_EXPERTS = 64
TOP_K = 8
N_GROUPS = 8
GROUP_SIZE = N_EXPERTS // N_GROUPS
TOPK_GROUPS = 4
ROUTED_SCALE = 2.5
LN_EPS = 1e-5
RMS_EPS = 1e-6
DEPTH = 1
ALPHA = (2.0 * DEPTH) ** 0.25

LANES = 128
SUBLANES = 8
VMEM_LIMIT = 56 * 1024 * 1024
DMA_QUEUES = 2
EXPERT_ROWS = 256
ZERO_GROUPS = EXPERT_ROWS // SUBLANES + 1
TM_PROJ, TN_PROJ = 1024, 512
TM_MIX = 256
TM_XATTN = 256
TM_ROUTER = 512
TM_DISPATCH = 256
TM_COMBINE = 128


def _params(*semantics):
    return pltpu.CompilerParams(dimension_semantics=semantics, vmem_limit_bytes=VMEM_LIMIT)


def _layer_norm(r, g, b):
    mu = jnp.mean(r, axis=-1, keepdims=True)
    c = r - mu
    var = jnp.mean(c * c, axis=-1, keepdims=True)
    return c * lax.rsqrt(var + LN_EPS) * g + b


def _rms_norm(v, g):
    return v * lax.rsqrt(jnp.mean(v * v, axis=-1, keepdims=True) + RMS_EPS) * g


def _dot(a, b):
    return jnp.dot(a, b, preferred_element_type=F32)


def _dot_nt(a, b):
    return lax.dot_general(a, b, (((1,), (1,)), ((), ())), preferred_element_type=F32)


def _silu(v):
    return v / (1.0 + jnp.exp(-v))


def _matmul_kernel(a_ref, w_ref, o_ref, a_bf):
    @pl.when(pl.program_id(1) == 0)
    def _():
        a_bf[...] = a_ref[...].astype(BF16)

    o_ref[...] = _dot(a_bf[...], w_ref[...]).astype(o_ref.dtype)


def _matmul(a, w, out_dtype, tm, tn):
    m, k = a.shape
    n = w.shape[1]
    return pl.pallas_call(
        _matmul_kernel,
        grid=(m // tm, n // tn),
        in_specs=[pl.BlockSpec((tm, k), lambda i, j: (i, 0)),
                  pl.BlockSpec((k, tn), lambda i, j: (0, j))],
        out_specs=pl.BlockSpec((tm, tn), lambda i, j: (i, j)),
        out_shape=jax.ShapeDtypeStruct((m, n), out_dtype),
        scratch_shapes=[pltpu.VMEM((tm, k), BF16)],
        compiler_params=_params("parallel", "arbitrary"),
        name="matmul",
    )(a, w)


def _attn_kernel(q_ref, kp_ref, kc_ref, vp_ref, vc_ref, o_ref, kk, vv, o_acc, l_acc):
    j = pl.program_id(1)
    blk, span = ATTN_BLOCK, ATTN_SPAN
    kk[0:span, :] = kp_ref[...]
    kk[span:, :] = kc_ref[...]
    vv[0:span, :] = vp_ref[...]
    vv[span:, :] = vc_ref[...]

    row = lax.broadcasted_iota(I32, (blk, 2 * blk), 0)
    col = lax.broadcasted_iota(I32, (blk, 2 * blk), 1)
    dist = row + blk - col
    in_window = (dist >= 0) & (dist <= blk)
    own_block = col >= blk
    scale = HEAD_DIM ** -0.5

    order = sorted(range(len(DILATED_PATTERNS)), key=lambda i: -DILATED_PATTERNS[i][1])
    assert DILATED_PATTERNS[order[-1]][1] == 1
    sub_blocks = span // blk
    for bi in order:
        dil = DILATED_PATTERNS[bi][1]
        shift = dil.bit_length() - 1

        def body(t, carry, bi=bi, dil=dil, shift=shift):
            r = t & (dil - 1)
            n = t >> shift
            base = n * (blk * dil) + r
            if dil == 1:
                base = pl.multiple_of(base, blk)
                q_rows, kv_rows = pl.ds(base, blk), pl.ds(span + base - blk, 2 * blk)
            else:
                q_rows = pl.ds(base, blk, stride=dil)
                kv_rows = pl.ds(span + base - blk * dil, 2 * blk, stride=dil)
            q = q_ref[q_rows, :].astype(BF16)
            k = kk[kv_rows, :].astype(BF16)
            v = vv[kv_rows, :].astype(BF16)
            s = _dot_nt(q, k) * scale
            valid = in_window & (own_block | (j > 0) | (n > 0))
            s = jnp.where(valid, s, -jnp.inf)
            m = jnp.max(s, axis=-1, keepdims=True)
            p = jnp.exp(s - m)
            l = jnp.sum(p, axis=-1, keepdims=True)
            o = _dot((p / l).astype(BF16), v)
            lse = jnp.broadcast_to(m + jnp.log(l), (blk, LANES))
            if dil != 1:
                o_acc[bi, q_rows, :] = o
                l_acc[bi, q_rows, :] = lse
                return carry
            others = [i for i in order if i != bi]
            lses = [lse] + [l_acc[i, q_rows, :] for i in others]
            outs = [o] + [o_acc[i, q_rows, :] for i in others]
            top = jnp.maximum(jnp.maximum(lses[0], lses[1]), lses[2])
            es = [jnp.exp(v_ - top) for v_ in lses]
            den = es[0] + es[1] + es[2]
            o_ref[q_rows, :] = ((es[0] / den) * outs[0] + (es[1] / den) * outs[1] + (es[2] / den) * outs[2])
            return carry

        lax.fori_loop(0, sub_blocks, body, 0, unroll=True)


def _dilated_attention(proj, batch, seq):
    n_tok, _ = proj.shape
    span = ATTN_SPAN
    spans = seq // span
    heads = N_ATTN_HEADS

    def spec(part, prev):
        def index(b, j, h):
            return (b * spans + (jnp.maximum(j - 1, 0) if prev else j), part * heads + h)
        return pl.BlockSpec((span, HEAD_DIM), index)

    return pl.pallas_call(
        _attn_kernel,
        grid=(batch, spans, heads),
        in_specs=[spec(0, False), spec(1, True), spec(1, False), spec(2, True), spec(2, False)],
        out_specs=pl.BlockSpec((span, HEAD_DIM), lambda b, j, h: (b * spans + j, h)),
        out_shape=jax.ShapeDtypeStruct((n_tok, ATTN_WIDTH), F32),
        scratch_shapes=[pltpu.VMEM((2 * span, HEAD_DIM), F32), pltpu.VMEM((2 * span, HEAD_DIM), F32),
                        pltpu.VMEM((len(DILATED_PATTERNS), span, HEAD_DIM), F32),
                        pltpu.VMEM((len(DILATED_PATTERNS), span, HEAD_DIM), F32)],
        compiler_params=_params("parallel", "parallel", "parallel"),
        name="dilated_attn",
    )(proj, proj, proj, proj, proj)


def _mix_out_kernel(a_ref, b_ref, c_ref, h_ref, cp_ref, hp_ref, x_ref,
                    w_ref, cw_ref, ga_ref, gc_ref, lg_ref, lb_ref, o_ref, *, tiles_per_seq):
    i = pl.program_id(0)
    z = c_ref[...] * h_ref[...]
    zp = jnp.where(i % tiles_per_seq == 0, 0.0, cp_ref[...] * hp_ref[...])
    zz = jnp.concatenate([zp, z], axis=0)
    z1 = pltpu.roll(zz, 1, axis=0)[SUBLANES:]
    z2 = pltpu.roll(zz, 2, axis=0)[SUBLANES:]
    conv = b_ref[...] * (cw_ref[0:1, :] * z2 + cw_ref[1:2, :] * z1 + cw_ref[2:3, :] * z)

    mixed = jnp.concatenate([_rms_norm(a_ref[...], ga_ref[...]), _rms_norm(conv, gc_ref[...])], axis=1)
    y = _dot(mixed.astype(BF16), w_ref[...])
    o_ref[...] = _layer_norm(ALPHA * x_ref[...] + y, lg_ref[...], lb_ref[...])


def _mix_out(attn, proj, x, w_out, conv_w, g_attn, g_conv, ln_g, ln_b, seq):
    n_tok, d = x.shape
    tm = TM_MIX
    rows8 = tm // SUBLANES
    row = lambda i: (i, 0)
    const = lambda i: (0, 0)
    prev = lambda part: (lambda i: (jnp.maximum(i * rows8 - 1, 0), part))
    return pl.pallas_call(
        functools.partial(_mix_out_kernel, tiles_per_seq=seq // tm),
        grid=(n_tok // tm,),
        in_specs=[
            pl.BlockSpec((tm, ATTN_WIDTH), row),
            pl.BlockSpec((tm, CONV_WIDTH), lambda i: (i, 3)),
            pl.BlockSpec((tm, CONV_WIDTH), lambda i: (i, 4)),
            pl.BlockSpec((tm, CONV_WIDTH), lambda i: (i, 5)),
            pl.BlockSpec((SUBLANES, CONV_WIDTH), prev(4)),
            pl.BlockSpec((SUBLANES, CONV_WIDTH), prev(5)),
            pl.BlockSpec((tm, d), row),
            pl.BlockSpec(w_out.shape, const),
            pl.BlockSpec(conv_w.shape, const),
            pl.BlockSpec(g_attn.shape, const),
            pl.BlockSpec(g_conv.shape, const),
            pl.BlockSpec(ln_g.shape, const),
            pl.BlockSpec(ln_b.shape, const)],
        out_specs=pl.BlockSpec((tm, d), row),
        out_shape=jax.ShapeDtypeStruct((n_tok, d), F32),
        compiler_params=_params("parallel"),
        name="mix_out_ln1",
    )(attn, proj, proj, proj, proj, proj, x, w_out, conv_w, g_attn, g_conv, ln_g, ln_b)


def _xattn_kernel(x_ref, wq_ref, k_ref, v_ref, wo_ref, lg_ref, lb_ref, o_ref):
    x = x_ref[...]
    d = x.shape[1]
    hd = d // N_MEM_HEADS
    q = _dot(x.astype(BF16), wq_ref[...]).astype(BF16)
    scale = hd ** -0.5
    outs = []
    for h in range(N_MEM_HEADS):
        sl = slice(h * hd, (h + 1) * hd)
        s = _dot_nt(q[:, sl], k_ref[:, sl]) * scale
        m = jnp.max(s, axis=-1, keepdims=True)
        p = jnp.exp(s - m)
        p = p / jnp.sum(p, axis=-1, keepdims=True)
        outs.append(_dot(p.astype(BF16), v_ref[:, sl]).astype(BF16))
    y = _dot(jnp.concatenate(outs, axis=1), wo_ref[...])
    o_ref[...] = _layer_norm(ALPHA * x + y, lg_ref[...], lb_ref[...])


def _xattn(x, w_q, kv, w_o, ln_g, ln_b, seq, mem_len):
    n_tok, d = x.shape
    tm = TM_XATTN
    row = lambda i: (i, 0)
    const = lambda i: (0, 0)
    tiles_per_seq = seq // tm
    return pl.pallas_call(
        _xattn_kernel,
        grid=(n_tok // tm,),
        in_specs=[pl.BlockSpec((tm, d), row),
                  pl.BlockSpec(w_q.shape, const),
                  pl.BlockSpec((mem_len, d), lambda i: (i // tiles_per_seq, 0)),
                  pl.BlockSpec((mem_len, d), lambda i: (i // tiles_per_seq, 1)),
                  pl.BlockSpec(w_o.shape, const),
                  pl.BlockSpec(ln_g.shape, const),
                  pl.BlockSpec(ln_b.shape, const)],
        out_specs=pl.BlockSpec((tm, d), row),
        out_shape=jax.ShapeDtypeStruct((n_tok, d), F32),
        compiler_params=_params("parallel"),
        name="xattn_ln2",
    )(x, w_q, kv, kv, w_o, ln_g, ln_b)


def _first_argmax(vals, index, sentinel):
    m = jnp.max(vals, axis=0, keepdims=True)
    i = jnp.min(jnp.where(vals == m, index, sentinel), axis=0, keepdims=True)
    return m, i


def _router_kernel(x_ref, wt_ref, bias_ref, idx_ref, gate_ref, rank_ref, cnt_ref):
    step = pl.program_id(0)
    tm = x_ref.shape[0]

    @pl.when(step == 0)
    def _():
        cnt_ref[...] = jnp.zeros_like(cnt_ref)

    x = x_ref[...]
    xh = x.astype(BF16)
    xl = (x - xh.astype(F32)).astype(BF16)
    w = wt_ref[...]
    wh = w.astype(BF16)
    wl = (w - wh.astype(F32)).astype(BF16)
    both = _dot_nt(jnp.concatenate([wh, wl], axis=0), xh)
    logits = both[:N_EXPERTS] + both[N_EXPERTS:] + _dot_nt(wh, xl)
    scores = 1.0 / (1.0 + jnp.exp(-logits))
    choice = scores + bias_ref[...]

    sub = lax.broadcasted_iota(I32, (GROUP_SIZE, tm), 0)
    group_scores = []
    for g in range(N_GROUPS):
        c = choice[g * GROUP_SIZE:(g + 1) * GROUP_SIZE, :]
        m1, i1 = _first_argmax(c, sub, GROUP_SIZE)
        m2 = jnp.max(jnp.where(sub == i1, -jnp.inf, c), axis=0, keepdims=True)
        group_scores.append(m1 + m2)
    gs = jnp.concatenate(group_scores, axis=0)
    gidx = lax.broadcasted_iota(I32, (N_GROUPS, tm), 0)
    gsel = jnp.zeros((N_GROUPS, tm), F32)
    for _ in range(TOPK_GROUPS):
        _, gi = _first_argmax(gs, gidx, N_GROUPS)
        hit = gidx == gi
        gsel = jnp.where(hit, 1.0, gsel)
        gs = jnp.where(hit, -jnp.inf, gs)
    masked = jnp.concatenate(
        [jnp.where(gsel[g:g + 1, :] > 0.0, choice[g * GROUP_SIZE:(g + 1) * GROUP_SIZE, :], -jnp.inf)
         for g in range(N_GROUPS)], axis=0)

    eidx = lax.broadcasted_iota(I32, (N_EXPERTS, tm), 0)
    hits, idxs, ws = [], [], []
    for _ in range(TOP_K):
        _, ei = _first_argmax(masked, eidx, N_EXPERTS)
        hit = eidx == ei
        hits.append(hit)
        idxs.append(ei)
        ws.append(jnp.sum(jnp.where(hit, scores, 0.0), axis=0, keepdims=True))
        masked = jnp.where(hit, -jnp.inf, masked)
    wsum = ws[0]
    for w in ws[1:]:
        wsum = wsum + w
    gate_ref[...] = jnp.concatenate([w / wsum * ROUTED_SCALE for w in ws], axis=0)
    idx_ref[...] = jnp.concatenate(idxs, axis=0)

    sel = hits[0]
    for hit in hits[1:]:
        sel = sel | hit
    self32 = jnp.where(sel, 1.0, 0.0)
    t_row = lax.broadcasted_iota(I32, (tm, tm), 0)
    t_col = lax.broadcasted_iota(I32, (tm, tm), 1)
    tri = jnp.where(t_row < t_col, 1.0, 0.0).astype(BF16)
    before = _dot(self32.astype(BF16), tri) + cnt_ref[:, 0:1]
    rank_ref[...] = jnp.concatenate(
        [jnp.sum(jnp.where(hit, before, 0.0), axis=0, keepdims=True) for hit in hits],
        axis=0).astype(I32)
    cnt_ref[...] += jnp.sum(self32, axis=1, keepdims=True)


def _router(x, w_router_t, bias_col):
    n_tok, d = x.shape
    tm = TM_ROUTER
    tok = lambda i: (0, i)
    return pl.pallas_call(
        _router_kernel,
        grid=(n_tok // tm,),
        in_specs=[pl.BlockSpec((tm, d), lambda i: (i, 0)),
                  pl.BlockSpec(w_router_t.shape, lambda i: (0, 0)),
                  pl.BlockSpec(bias_col.shape, lambda i: (0, 0))],
        out_specs=[pl.BlockSpec((TOP_K, tm), tok), pl.BlockSpec((TOP_K, tm), tok),
                   pl.BlockSpec((TOP_K, tm), tok),
                   pl.BlockSpec((N_EXPERTS, LANES), lambda i: (0, 0))],
        out_shape=[jax.ShapeDtypeStruct((TOP_K, n_tok), I32),
                   jax.ShapeDtypeStruct((TOP_K, n_tok), F32),
                   jax.ShapeDtypeStruct((TOP_K, n_tok), I32),
                   jax.ShapeDtypeStruct((N_EXPERTS, LANES), F32)],
        compiler_params=_params("arbitrary"),
        name="router",
    )(x, w_router_t, bias_col)


def _tile_view(x):
    r, d = x.shape
    return (x.reshape(r // SUBLANES, SUBLANES, d // LANES, LANES).transpose(0, 2, 1, 3)
            .reshape(r // SUBLANES, d // LANES, SUBLANES, 1, LANES))


def _row_view(v):
    g, c, s, _, l = v.shape
    return v.reshape(g, c, s, l).transpose(0, 2, 1, 3).reshape(g * s, c * l)


def _tile_row(view_ref, row):
    return view_ref.at[row >> 3, :, row & (SUBLANES - 1)]


def _dispatch_kernel(pad_ref, dest_ref, x_ref, xs_ref, zero_ref, sem):
    step = pl.program_id(0)
    groups = x_ref.shape[0]

    @pl.when(step == 0)
    def _():
        zero_ref[...] = jnp.zeros_like(zero_ref)

        def fill(e):
            return pltpu.make_async_copy(zero_ref, xs_ref.at[pl.ds(pad_ref[e] >> 3, ZERO_GROUPS)], sem)

        def start(e, c):
            fill(e).start()
            return c

        def wait(e, c):
            fill(e).wait()
            return c

        lax.fori_loop(0, N_EXPERTS, start, 0)
        lax.fori_loop(0, N_EXPERTS, wait, 0)

    def copies(g):
        return [pltpu.make_async_copy(x_ref.at[g, :, s], _tile_row(xs_ref, dest_ref[k, g * SUBLANES + s]), sem)
                for s in range(SUBLANES) for k in range(TOP_K)]

    def start_rows(g, c):
        for n, cp in enumerate(copies(g)):
            cp.start(priority=n % DMA_QUEUES)
        return c

    def wait_rows(g, c):
        for cp in copies(g):
            cp.wait()
        return c

    lax.fori_loop(0, groups, start_rows, 0)
    lax.fori_loop(0, groups, wait_rows, 0)


def _dispatch(pad_start, dest, x, n_rows):
    n_tok, w = x.shape
    tm = TM_DISPATCH
    chunks = w // LANES
    block = (tm // SUBLANES, chunks, SUBLANES, 1, LANES)
    total_groups = n_rows // SUBLANES + ZERO_GROUPS
    out = pl.pallas_call(
        _dispatch_kernel,
        grid_spec=pltpu.PrefetchScalarGridSpec(
            num_scalar_prefetch=1,
            grid=(n_tok // tm,),
            in_specs=[pl.BlockSpec((TOP_K, tm), lambda i, pad: (0, i), memory_space=pltpu.SMEM),
                      pl.BlockSpec(block, lambda i, pad: (i, 0, 0, 0, 0))],
            out_specs=pl.BlockSpec(memory_space=pl.ANY),
            scratch_shapes=[pltpu.VMEM((ZERO_GROUPS, chunks, SUBLANES, 1, LANES), F32),
                            pltpu.SemaphoreType.DMA(())]),
        out_shape=jax.ShapeDtypeStruct((total_groups, chunks, SUBLANES, 1, LANES), F32),
        compiler_params=_params("arbitrary"),
        name="dispatch",
    )(pad_start, dest, _tile_view(x))
    return _row_view(out)


def _expert_kernel(first_ref, nblk_ref, nused_ref, wg_ref, wu_ref, wd_ref, xs_hbm, ys_hbm,
                   xbuf, ybuf, wg_s, wu_s, wd_s, xsem, ysem):
    e = pl.program_id(0)
    n_used = nused_ref[0]

    def x_copy(blk, slot):
        rows = pl.ds(pl.multiple_of(blk * EXPERT_ROWS, EXPERT_ROWS), EXPERT_ROWS)
        return pltpu.make_async_copy(xs_hbm.at[rows, :], xbuf.at[slot], xsem.at[slot])

    def y_copy(blk, slot):
        rows = pl.ds(pl.multiple_of(blk * EXPERT_ROWS, EXPERT_ROWS), EXPERT_ROWS)
        return pltpu.make_async_copy(ybuf.at[slot], ys_hbm.at[rows, :], ysem.at[slot])

    @pl.when(e == 0)
    def _():
        x_copy(0, 0).start(priority=DMA_QUEUES - 1)

    @pl.when(nblk_ref[e] > 0)
    def _():
        wg_s[...] = wg_ref[...].astype(BF16)
        wu_s[...] = wu_ref[...].astype(BF16)
        wd_s[...] = wd_ref[...].astype(BF16)

        def block(i, carry):
            blk = first_ref[e] + i
            slot = blk & 1
            x_copy(blk, slot).wait()

            @pl.when(blk + 1 < n_used)
            def _():
                x_copy(blk + 1, 1 - slot).start(priority=DMA_QUEUES - 1)

            @pl.when(blk >= 2)
            def _():
                y_copy(blk - 2, slot).wait()

            xb = xbuf[slot].astype(BF16)
            hmid = (_silu(_dot(xb, wg_s[...])) * _dot(xb, wu_s[...])).astype(BF16)
            ybuf[slot] = _dot(hmid, wd_s[...])
            y_copy(blk, slot).start(priority=DMA_QUEUES - 1)
            return carry

        lax.fori_loop(0, nblk_ref[e], block, 0)

    @pl.when(e == pl.num_programs(0) - 1)
    def _():
        @pl.when(n_used >= 2)
        def _():
            y_copy(n_used - 2, (n_used - 2) & 1).wait()

        y_copy(n_used - 1, (n_used - 1) & 1).wait()


def _experts(first_block, n_blocks_of, n_used, xs, w_gate, w_up, w_down, total_blocks):
    n_exp, d, ff = w_gate.shape
    wsel = lambda e, first, nblk, nused: (e, 0, 0)
    hbm = pl.BlockSpec(memory_space=pl.ANY)
    return pl.pallas_call(
        _expert_kernel,
        grid_spec=pltpu.PrefetchScalarGridSpec(
            num_scalar_prefetch=3,
            grid=(n_exp,),
            in_specs=[pl.BlockSpec((None, d, ff), wsel), pl.BlockSpec((None, d, ff), wsel),
                      pl.BlockSpec((None, ff, d), wsel), hbm],
            out_specs=hbm,
            scratch_shapes=[pltpu.VMEM((2, EXPERT_ROWS, d), F32), pltpu.VMEM((2, EXPERT_ROWS, d), F32),
                            pltpu.VMEM((d, ff), BF16), pltpu.VMEM((d, ff), BF16),
                            pltpu.VMEM((ff, d), BF16),
                            pltpu.SemaphoreType.DMA((2,)), pltpu.SemaphoreType.DMA((2,))]),
        out_shape=jax.ShapeDtypeStruct((total_blocks * EXPERT_ROWS, d), F32),
        compiler_params=_params("arbitrary"),
        name="experts",
    )(first_block, n_blocks_of, n_used, w_gate, w_up, w_down, xs)


def _combine_kernel(dest_ref, gate_ref, x_ref, ys_ref, wsg_ref, wsu_ref, wsd_ref, lg_ref, lb_ref,
                    o_ref, buf, sem):
    tm, d = x_ref.shape
    groups, chunks = tm // SUBLANES, d // LANES
    dma_views = [buf.at[k].reshape(groups, chunks, SUBLANES, 1, LANES) for k in range(TOP_K)]
    load_views = [buf.at[k].reshape(groups, chunks, SUBLANES, LANES) for k in range(TOP_K)]

    def copies(g):
        return [pltpu.make_async_copy(_tile_row(ys_ref, dest_ref[k, g * SUBLANES + s]),
                                      dma_views[k].at[g, :, s], sem)
                for s in range(SUBLANES) for k in range(TOP_K)]

    def start_rows(g, c):
        for n, cp in enumerate(copies(g)):
            cp.start(priority=n % DMA_QUEUES)
        return c

    def wait_rows(g, c):
        for cp in copies(g):
            cp.wait()
        return c

    lax.fori_loop(0, groups, start_rows, 0)

    x = x_ref[...]
    xb = x.astype(BF16)
    hmid = (_silu(_dot(xb, wsg_ref[...])) * _dot(xb, wsu_ref[...])).astype(BF16)
    acc = ALPHA * x + _dot(hmid, wsd_ref[...])

    lax.fori_loop(0, groups, wait_rows, 0)
    gates = gate_ref[...]

    def rows_of(k):
        return jnp.concatenate([load_views[k][:, c].reshape(tm, LANES) for c in range(chunks)], axis=1)

    routed = rows_of(0) * gates[:, 0:1]
    for k in range(1, TOP_K):
        routed += rows_of(k) * gates[:, k:k + 1]
    o_ref[...] = _layer_norm(acc + routed, lg_ref[...], lb_ref[...])


def _combine(dest, gates_t, x, ys, ws_gate, ws_up, ws_down, ln_g, ln_b):
    n_tok, d = x.shape
    tm = TM_COMBINE
    row = lambda i: (i, 0)
    const = lambda i: (0, 0)
    return pl.pallas_call(
        _combine_kernel,
        grid=(n_tok // tm,),
        in_specs=[pl.BlockSpec((TOP_K, tm), lambda i: (0, i), memory_space=pltpu.SMEM),
                  pl.BlockSpec((tm, TOP_K), row),
                  pl.BlockSpec((tm, d), row),
                  pl.BlockSpec(memory_space=pl.ANY),
                  pl.BlockSpec(ws_gate.shape, const),
                  pl.BlockSpec(ws_up.shape, const),
                  pl.BlockSpec(ws_down.shape, const),
                  pl.BlockSpec(ln_g.shape, const),
                  pl.BlockSpec(ln_b.shape, const)],
        out_specs=pl.BlockSpec((tm, d), row),
        out_shape=jax.ShapeDtypeStruct((n_tok, d), F32),
        scratch_shapes=[pltpu.VMEM((TOP_K, tm * (d // LANES), LANES), F32), pltpu.SemaphoreType.DMA(())],
        compiler_params=_params("arbitrary"),
        name="combine_shared_ln3",
    )(dest, gates_t, x, _tile_view(ys), ws_gate, ws_up, ws_down, ln_g, ln_b)


def _layer(x, mem, w_in, conv_w, g_attn_out, g_conv_out, w_out, ln1_g, ln1_b, w_q_mem, w_kv_mem,
           w_o_mem, ln2_g, ln2_b, w_router, router_bias, w_gate, w_up, w_down, ws_gate, ws_up,
           ws_down, ln3_g, ln3_b):
    batch, seq, d = x.shape
    mem_len = mem.shape[1]
    n_tok = batch * seq
    xf = x.reshape(n_tok, d)
    row = lambda v: v.reshape(1, -1)

    proj = _matmul(xf, w_in.astype(BF16), F32, TM_PROJ, TN_PROJ)
    attn = _dilated_attention(proj, batch, seq)
    x1 = _mix_out(attn, proj, xf, w_out.astype(BF16), conv_w, row(g_attn_out), row(g_conv_out),
                  row(ln1_g), row(ln1_b), seq)

    kv = _matmul(mem.reshape(batch * mem_len, d), w_kv_mem.astype(BF16), BF16,
                 mem_len, 1024)
    x2 = _xattn(x1, w_q_mem.astype(BF16), kv, w_o_mem.astype(BF16), row(ln2_g), row(ln2_b), seq,
                mem_len)

    idx, gates, rank, cnt = _router(x2, w_router.T, router_bias.reshape(N_EXPERTS, 1))
    counts = cnt[:, 0].astype(I32)
    padded = (counts + EXPERT_ROWS - 1) // EXPERT_ROWS * EXPERT_ROWS
    seg_end = jnp.cumsum(padded)
    seg_start = seg_end - padded
    experts = jnp.arange(N_EXPERTS, dtype=I32)
    dest = rank + jnp.sum(jnp.where(idx[..., None] == experts, seg_start, 0), axis=-1)
    total_blocks = -(-(n_tok * TOP_K) // EXPERT_ROWS) + N_EXPERTS
    n_used = (seg_end[-1:] // EXPERT_ROWS).astype(I32)

    xs = _dispatch(seg_start + counts, dest, x2, total_blocks * EXPERT_ROWS)
    ys = _experts(seg_start // EXPERT_ROWS, padded // EXPERT_ROWS, n_used, xs, w_gate, w_up, w_down,
                  total_blocks)
    out = _combine(dest, gates.T, x2, ys, ws_gate.astype(BF16), ws_up.astype(BF16),
                   ws_down.astype(BF16), row(ln3_g), row(ln3_b))
    return out.reshape(batch, seq, d)


def kernel(x, mem, w_in, conv_w, g_attn_out, g_conv_out, w_out, ln1_g, ln1_b, w_q_mem, w_kv_mem, w_o_mem, ln2_g, ln2_b, w_router, router_bias, w_gate, w_up, w_down, ws_gate, ws_up, ws_down, ln3_g, ln3_b):
    for l in range(DEPTH):
        x = _layer(x, mem, w_in[l], conv_w[l], g_attn_out[l], g_conv_out[l], w_out[l], ln1_g[l],
                   ln1_b[l], w_q_mem[l], w_kv_mem[l], w_o_mem[l], ln2_g[l], ln2_b[l], w_router[l],
                   router_bias[l], w_gate[l], w_up[l], w_down[l], ws_gate[l], ws_up[l], ws_down[l],
                   ln3_g[l], ln3_b[l])
    return x
```

```python
import functools

import jax
import jax.numpy as jnp
from jax import lax
from jax.experimental import pallas as pl
from jax.experimental.pallas import tpu as pltpu

F32 = jnp.float32
BF16 = jnp.bfloat16
I32 = jnp.int32

HEAD_DIM = 128
N_ATTN_HEADS = 8
ATTN_WIDTH = N_ATTN_HEADS * HEAD_DIM
CONV_WIDTH = 1024
PROJ_PARTS = 6
DILATED_PATTERNS = ((128, 1), (512, 4), (2048, 16))
ATTN_BLOCK = 128
ATTN_SPAN = 16 * ATTN_BLOCK
N_MEM_HEADS = 4
N_EXPERTS = 64
TOP_K = 8
N_GROUPS = 8
GROUP_SIZE = N_EXPERTS // N_GROUPS
TOPK_GROUPS = 4
ROUTED_SCALE = 2.5
LN_EPS = 1e-5
RMS_EPS = 1e-6
DEPTH = 1
ALPHA = (2.0 * DEPTH) ** 0.25

LANES = 128
SUBLANES = 8
VMEM_LIMIT = 56 * 1024 * 1024
DMA_QUEUES = 2
EXPERT_ROWS = 256
ZERO_GROUPS = EXPERT_ROWS // SUBLANES + 1
TM_PROJ, TN_PROJ = 1024, 512
TM_MIX = 256
TM_XATTN = 256
TM_ROUTER = 512
TM_DISPATCH = 256
TM_COMBINE = 128


def _params(*semantics):
    return pltpu.CompilerParams(dimension_semantics=semantics, vmem_limit_bytes=VMEM_LIMIT)


def _layer_norm(r, g, b):
    mu = jnp.mean(r, axis=-1, keepdims=True)
    c = r - mu
    var = jnp.mean(c * c, axis=-1, keepdims=True)
    return c * lax.rsqrt(var + LN_EPS) * g + b


def _rms_norm(v, g):
    return v * lax.rsqrt(jnp.mean(v * v, axis=-1, keepdims=True) + RMS_EPS) * g


def _dot(a, b):
    return jnp.dot(a, b, preferred_element_type=F32)


def _dot_nt(a, b):
    return lax.dot_general(a, b, (((1,), (1,)), ((), ())), preferred_element_type=F32)


def _silu(v):
    return v / (1.0 + jnp.exp(-v))


def _matmul_kernel(a_ref, w_ref, o_ref, a_bf):
    @pl.when(pl.program_id(1) == 0)
    def _():
        a_bf[...] = a_ref[...].astype(BF16)

    o_ref[...] = _dot(a_bf[...], w_ref[...]).astype(o_ref.dtype)


def _matmul(a, w, out_dtype, tm, tn):
    m, k = a.shape
    n = w.shape[1]
    return pl.pallas_call(
        _matmul_kernel,
        grid=(m // tm, n // tn),
        in_specs=[pl.BlockSpec((tm, k), lambda i, j: (i, 0)),
                  pl.BlockSpec((k, tn), lambda i, j: (0, j))],
        out_specs=pl.BlockSpec((tm, tn), lambda i, j: (i, j)),
        out_shape=jax.ShapeDtypeStruct((m, n), out_dtype),
        scratch_shapes=[pltpu.VMEM((tm, k), BF16)],
        compiler_params=_params("parallel", "arbitrary"),
        name="matmul",
    )(a, w)


def _attn_kernel(q_ref, kp_ref, kc_ref, vp_ref, vc_ref, o_ref, kk, vv, o_acc, l_acc):
    j = pl.program_id(1)
    blk, span = ATTN_BLOCK, ATTN_SPAN
    kk[0:span, :] = kp_ref[...]
    kk[span:, :] = kc_ref[...]
    vv[0:span, :] = vp_ref[...]
    vv[span:, :] = vc_ref[...]

    row = lax.broadcasted_iota(I32, (blk, 2 * blk), 0)
    col = lax.broadcasted_iota(I32, (blk, 2 * blk), 1)
    dist = row + blk - col
    in_window = (dist >= 0) & (dist <= blk)
    own_block = col >= blk
    scale = HEAD_DIM ** -0.5

    order = sorted(range(len(DILATED_PATTERNS)), key=lambda i: -DILATED_PATTERNS[i][1])
    assert DILATED_PATTERNS[order[-1]][1] == 1
    sub_blocks = span // blk
    for bi in order:
        dil = DILATED_PATTERNS[bi][1]
        shift = dil.bit_length() - 1

        def body(t, carry, bi=bi, dil=dil, shift=shift):
            r = t & (dil - 1)
            n = t >> shift
            base = n * (blk * dil) + r
            if dil == 1:
                base = pl.multiple_of(base, blk)
                q_rows, kv_rows = pl.ds(base, blk), pl.ds(span + base - blk, 2 * blk)
            else:
                q_rows = pl.ds(base, blk, stride=dil)
                kv_rows = pl.ds(span + base - blk * dil, 2 * blk, stride=dil)
            q = q_ref[q_rows, :].astype(BF16)
            k = kk[kv_rows, :].astype(BF16)
            v = vv[kv_rows, :].astype(BF16)
            s = _dot_nt(q, k) * scale
            valid = in_window & (own_block | (j > 0) | (n > 0))
            s = jnp.where(valid, s, -jnp.inf)
            m = jnp.max(s, axis=-1, keepdims=True)
            p = jnp.exp(s - m)
            l = jnp.sum(p, axis=-1, keepdims=True)
            o = _dot((p / l).astype(BF16), v)
            lse = jnp.broadcast_to(m + jnp.log(l), (blk, LANES))
            if dil != 1:
                o_acc[bi, q_rows, :] = o
                l_acc[bi, q_rows, :] = lse
                return carry
            others = [i for i in order if i != bi]
            lses = [lse] + [l_acc[i, q_rows, :] for i in others]
            outs = [o] + [o_acc[i, q_rows, :] for i in others]
            top = jnp.maximum(jnp.maximum(lses[0], lses[1]), lses[2])
            es = [jnp.exp(v_ - top) for v_ in lses]
            den = es[0] + es[1] + es[2]
            o_ref[q_rows, :] = ((es[0] / den) * outs[0] + (es[1] / den) * outs[1] + (es[2] / den) * outs[2])
            return carry

        lax.fori_loop(0, sub_blocks, body, 0, unroll=True)


def _dilated_attention(proj, batch, seq):
    n_tok, _ = proj.shape
    span = ATTN_SPAN
    spans = seq // span
    heads = N_ATTN_HEADS

    def spec(part, prev):
        def index(b, j, h):
            return (b * spans + (jnp.maximum(j - 1, 0) if prev else j), part * heads + h)
        return pl.BlockSpec((span, HEAD_DIM), index)

    return pl.pallas_call(
        _attn_kernel,
        grid=(batch, spans, heads),
        in_specs=[spec(0, False), spec(1, True), spec(1, False), spec(2, True), spec(2, False)],
        out_specs=pl.BlockSpec((span, HEAD_DIM), lambda b, j, h: (b * spans + j, h)),
        out_shape=jax.ShapeDtypeStruct((n_tok, ATTN_WIDTH), F32),
        scratch_shapes=[pltpu.VMEM((2 * span, HEAD_DIM), F32), pltpu.VMEM((2 * span, HEAD_DIM), F32),
                        pltpu.VMEM((len(DILATED_PATTERNS), span, HEAD_DIM), F32),
                        pltpu.VMEM((len(DILATED_PATTERNS), span, HEAD_DIM), F32)],
        compiler_params=_params("parallel", "parallel", "parallel"),
        name="dilated_attn",
    )(proj, proj, proj, proj, proj)


def _mix_out_kernel(a_ref, b_ref, c_ref, h_ref, cp_ref, hp_ref, x_ref,
                    w_ref, cw_ref, ga_ref, gc_ref, lg_ref, lb_ref, o_ref, *, tiles_per_seq):
    i = pl.program_id(0)
    z = c_ref[...] * h_ref[...]
    zp = jnp.where(i % tiles_per_seq == 0, 0.0, cp_ref[...] * hp_ref[...])
    zz = jnp.concatenate([zp, z], axis=0)
    z1 = pltpu.roll(zz, 1, axis=0)[SUBLANES:]
    z2 = pltpu.roll(zz, 2, axis=0)[SUBLANES:]
    conv = b_ref[...] * (cw_ref[0:1, :] * z2 + cw_ref[1:2, :] * z1 + cw_ref[2:3, :] * z)

    mixed = jnp.concatenate([_rms_norm(a_ref[...], ga_ref[...]), _rms_norm(conv, gc_ref[...])], axis=1)
    y = _dot(mixed.astype(BF16), w_ref[...])
    o_ref[...] = _layer_norm(ALPHA * x_ref[...] + y, lg_ref[...], lb_ref[...])


def _mix_out(attn, proj, x, w_out, conv_w, g_attn, g_conv, ln_g, ln_b, seq):
    n_tok, d = x.shape
    tm = TM_MIX
    rows8 = tm // SUBLANES
    row = lambda i: (i, 0)
    const = lambda i: (0, 0)
    prev = lambda part: (lambda i: (jnp.maximum(i * rows8 - 1, 0), part))
    return pl.pallas_call(
        functools.partial(_mix_out_kernel, tiles_per_seq=seq // tm),
        grid=(n_tok // tm,),
        in_specs=[
            pl.BlockSpec((tm, ATTN_WIDTH), row),
            pl.BlockSpec((tm, CONV_WIDTH), lambda i: (i, 3)),
            pl.BlockSpec((tm, CONV_WIDTH), lambda i: (i, 4)),
            pl.BlockSpec((tm, CONV_WIDTH), lambda i: (i, 5)),
            pl.BlockSpec((SUBLANES, CONV_WIDTH), prev(4)),
            pl.BlockSpec((SUBLANES, CONV_WIDTH), prev(5)),
            pl.BlockSpec((tm, d), row),
            pl.BlockSpec(w_out.shape, const),
            pl.BlockSpec(conv_w.shape, const),
            pl.BlockSpec(g_attn.shape, const),
            pl.BlockSpec(g_conv.shape, const),
            pl.BlockSpec(ln_g.shape, const),
            pl.BlockSpec(ln_b.shape, const)],
        out_specs=pl.BlockSpec((tm, d), row),
        out_shape=jax.ShapeDtypeStruct((n_tok, d), F32),
        compiler_params=_params("parallel"),
        name="mix_out_ln1",
    )(attn, proj, proj, proj, proj, proj, x, w_out, conv_w, g_attn, g_conv, ln_g, ln_b)


def _xattn_kernel(x_ref, wq_ref, k_ref, v_ref, wo_ref, lg_ref, lb_ref, o_ref):
    x = x_ref[...]
    d = x.shape[1]
    hd = d // N_MEM_HEADS
    q = _dot(x.astype(BF16), wq_ref[...]).astype(BF16)
    scale = hd ** -0.5
    outs = []
    for h in range(N_MEM_HEADS):
        sl = slice(h * hd, (h + 1) * hd)
        s = _dot_nt(q[:, sl], k_ref[:, sl]) * scale
        m = jnp.max(s, axis=-1, keepdims=True)
        p = jnp.exp(s - m)
        p = p / jnp.sum(p, axis=-1, keepdims=True)
        outs.append(_dot(p.astype(BF16), v_ref[:, sl]).astype(BF16))
    y = _dot(jnp.concatenate(outs, axis=1), wo_ref[...])
    o_ref[...] = _layer_norm(ALPHA * x + y, lg_ref[...], lb_ref[...])


def _xattn(x, w_q, kv, w_o, ln_g, ln_b, seq, mem_len):
    n_tok, d = x.shape
    tm = TM_XATTN
    row = lambda i: (i, 0)
    const = lambda i: (0, 0)
    tiles_per_seq = seq // tm
    return pl.pallas_call(
        _xattn_kernel,
        grid=(n_tok // tm,),
        in_specs=[pl.BlockSpec((tm, d), row),
                  pl.BlockSpec(w_q.shape, const),
                  pl.BlockSpec((mem_len, d), lambda i: (i // tiles_per_seq, 0)),
                  pl.BlockSpec((mem_len, d), lambda i: (i // tiles_per_seq, 1)),
                  pl.BlockSpec(w_o.shape, const),
                  pl.BlockSpec(ln_g.shape, const),
                  pl.BlockSpec(ln_b.shape, const)],
        out_specs=pl.BlockSpec((tm, d), row),
        out_shape=jax.ShapeDtypeStruct((n_tok, d), F32),
        compiler_params=_params("parallel"),
        name="xattn_ln2",
    )(x, w_q, kv, kv, w_o, ln_g, ln_b)


def _first_argmax(vals, index, sentinel):
    m = jnp.max(vals, axis=0, keepdims=True)
    i = jnp.min(jnp.where(vals == m, index, sentinel), axis=0, keepdims=True)
    return m, i


def _router_kernel(x_ref, wt_ref, bias_ref, idx_ref, gate_ref, rank_ref, cnt_ref):
    step = pl.program_id(0)
    tm = x_ref.shape[0]

    @pl.when(step == 0)
    def _():
        cnt_ref[...] = jnp.zeros_like(cnt_ref)

    x = x_ref[...]
    xh = x.astype(BF16)
    xl = (x - xh.astype(F32)).astype(BF16)
    w = wt_ref[...]
    wh = w.astype(BF16)
    wl = (w - wh.astype(F32)).astype(BF16)
    both = _dot_nt(jnp.concatenate([wh, wl], axis=0), xh)
    logits = both[:N_EXPERTS] + both[N_EXPERTS:] + _dot_nt(wh, xl)
    scores = 1.0 / (1.0 + jnp.exp(-logits))
    choice = scores + bias_ref[...]

    sub = lax.broadcasted_iota(I32, (GROUP_SIZE, tm), 0)
    group_scores = []
    for g in range(N_GROUPS):
        c = choice[g * GROUP_SIZE:(g + 1) * GROUP_SIZE, :]
        m1, i1 = _first_argmax(c, sub, GROUP_SIZE)
        m2 = jnp.max(jnp.where(sub == i1, -jnp.inf, c), axis=0, keepdims=True)
        group_scores.append(m1 + m2)
    gs = jnp.concatenate(group_scores, axis=0)
    gidx = lax.broadcasted_iota(I32, (N_GROUPS, tm), 0)
    gsel = jnp.zeros((N_GROUPS, tm), F32)
    for _ in range(TOPK_GROUPS):
        _, gi = _first_argmax(gs, gidx, N_GROUPS)
        hit = gidx == gi
        gsel = jnp.where(hit, 1.0, gsel)
        gs = jnp.where(hit, -jnp.inf, gs)
    masked = jnp.concatenate(
        [jnp.where(gsel[g:g + 1, :] > 0.0, choice[g * GROUP_SIZE:(g + 1) * GROUP_SIZE, :], -jnp.inf)
         for g in range(N_GROUPS)], axis=0)

    eidx = lax.broadcasted_iota(I32, (N_EXPERTS, tm), 0)
    hits, idxs, ws = [], [], []
    for _ in range(TOP_K):
        _, ei = _first_argmax(masked, eidx, N_EXPERTS)
        hit = eidx == ei
        hits.append(hit)
        idxs.append(ei)
        ws.append(jnp.sum(jnp.where(hit, scores, 0.0), axis=0, keepdims=True))
        masked = jnp.where(hit, -jnp.inf, masked)
    wsum = ws[0]
    for w in ws[1:]:
        wsum = wsum + w
    gate_ref[...] = jnp.concatenate([w / wsum * ROUTED_SCALE for w in ws], axis=0)
    idx_ref[...] = jnp.concatenate(idxs, axis=0)

    sel = hits[0]
    for hit in hits[1:]:
        sel = sel | hit
    self32 = jnp.where(sel, 1.0, 0.0)
    t_row = lax.broadcasted_iota(I32, (tm, tm), 0)
    t_col = lax.broadcasted_iota(I32, (tm, tm), 1)
    tri = jnp.where(t_row < t_col, 1.0, 0.0).astype(BF16)
    before = _dot(self32.astype(BF16), tri) + cnt_ref[:, 0:1]
    rank_ref[...] = jnp.concatenate(
        [jnp.sum(jnp.where(hit, before, 0.0), axis=0, keepdims=True) for hit in hits],
        axis=0).astype(I32)
    cnt_ref[...] += jnp.sum(self32, axis=1, keepdims=True)


def _router(x, w_router_t, bias_col):
    n_tok, d = x.shape
    tm = TM_ROUTER
    tok = lambda i: (0, i)
    return pl.pallas_call(
        _router_kernel,
        grid=(n_tok // tm,),
        in_specs=[pl.BlockSpec((tm, d), lambda i: (i, 0)),
                  pl.BlockSpec(w_router_t.shape, lambda i: (0, 0)),
                  pl.BlockSpec(bias_col.shape, lambda i: (0, 0))],
        out_specs=[pl.BlockSpec((TOP_K, tm), tok), pl.BlockSpec((TOP_K, tm), tok),
                   pl.BlockSpec((TOP_K, tm), tok),
                   pl.BlockSpec((N_EXPERTS, LANES), lambda i: (0, 0))],
        out_shape=[jax.ShapeDtypeStruct((TOP_K, n_tok), I32),
                   jax.ShapeDtypeStruct((TOP_K, n_tok), F32),
                   jax.ShapeDtypeStruct((TOP_K, n_tok), I32),
                   jax.ShapeDtypeStruct((N_EXPERTS, LANES), F32)],
        compiler_params=_params("arbitrary"),
        name="router",
    )(x, w_router_t, bias_col)


def _tile_view(x):
    r, d = x.shape
    return (x.reshape(r // SUBLANES, SUBLANES, d // LANES, LANES).transpose(0, 2, 1, 3)
            .reshape(r // SUBLANES, d // LANES, SUBLANES, 1, LANES))


def _row_view(v):
    g, c, s, _, l = v.shape
    return v.reshape(g, c, s, l).transpose(0, 2, 1, 3).reshape(g * s, c * l)


def _tile_row(view_ref, row):
    return view_ref.at[row >> 3, :, row & (SUBLANES - 1)]


def _dispatch_kernel(pad_ref, dest_ref, x_ref, xs_ref, zero_ref, sem):
    step = pl.program_id(0)
    groups = x_ref.shape[0]

    @pl.when(step == 0)
    def _():
        zero_ref[...] = jnp.zeros_like(zero_ref)

        def fill(e):
            return pltpu.make_async_copy(zero_ref, xs_ref.at[pl.ds(pad_ref[e] >> 3, ZERO_GROUPS)], sem)

        def start(e, c):
            fill(e).start()
            return c

        def wait(e, c):
            fill(e).wait()
            return c

        lax.fori_loop(0, N_EXPERTS, start, 0)
        lax.fori_loop(0, N_EXPERTS, wait, 0)

    def copies(g):
        return [pltpu.make_async_copy(x_ref.at[g, :, s], _tile_row(xs_ref, dest_ref[k, g * SUBLANES + s]), sem)
                for s in range(SUBLANES) for k in range(TOP_K)]

    def start_rows(g, c):
        for n, cp in enumerate(copies(g)):
            cp.start(priority=n % DMA_QUEUES)
        return c

    def wait_rows(g, c):
        for cp in copies(g):
            cp.wait()
        return c

    lax.fori_loop(0, groups, start_rows, 0)
    lax.fori_loop(0, groups, wait_rows, 0)


def _dispatch(pad_start, dest, x, n_rows):
    n_tok, w = x.shape
    tm = TM_DISPATCH
    chunks = w // LANES
    block = (tm // SUBLANES, chunks, SUBLANES, 1, LANES)
    total_groups = n_rows // SUBLANES + ZERO_GROUPS
    out = pl.pallas_call(
        _dispatch_kernel,
        grid_spec=pltpu.PrefetchScalarGridSpec(
            num_scalar_prefetch=1,
            grid=(n_tok // tm,),
            in_specs=[pl.BlockSpec((TOP_K, tm), lambda i, pad: (0, i), memory_space=pltpu.SMEM),
                      pl.BlockSpec(block, lambda i, pad: (i, 0, 0, 0, 0))],
            out_specs=pl.BlockSpec(memory_space=pl.ANY),
            scratch_shapes=[pltpu.VMEM((ZERO_GROUPS, chunks, SUBLANES, 1, LANES), F32),
                            pltpu.SemaphoreType.DMA(())]),
        out_shape=jax.ShapeDtypeStruct((total_groups, chunks, SUBLANES, 1, LANES), F32),
        compiler_params=_params("arbitrary"),
        name="dispatch",
    )(pad_start, dest, _tile_view(x))
    return _row_view(out)


def _expert_kernel(first_ref, nblk_ref, nused_ref, next_ref, wslot_ref, wg_hbm, wu_hbm, wd_hbm, xs_hbm, ys_hbm,
                   xbuf, ybuf, wg_f, wu_f, wd_f, wg_s, wu_s, wd_s, xsem, ysem, wsem):
    e = pl.program_id(0)
    n_used = nused_ref[0]

    def x_copy(blk, slot):
        rows = pl.ds(pl.multiple_of(blk * EXPERT_ROWS, EXPERT_ROWS), EXPERT_ROWS)
        return pltpu.make_async_copy(xs_hbm.at[rows, :], xbuf.at[slot], xsem.at[slot])

    def y_copy(blk, slot):
        rows = pl.ds(pl.multiple_of(blk * EXPERT_ROWS, EXPERT_ROWS), EXPERT_ROWS)
        return pltpu.make_async_copy(ybuf.at[slot], ys_hbm.at[rows, :], ysem.at[slot])

    def w_copies(expert, slot):
        return [pltpu.make_async_copy(src.at[expert], dst.at[slot], wsem.at[slot])
                for src, dst in ((wg_hbm, wg_f), (wu_hbm, wu_f), (wd_hbm, wd_f))]

    @pl.when(e == 0)
    def _():
        x_copy(0, 0).start()

    @pl.when(nblk_ref[e] > 0)
    def _():
        wslot = wslot_ref[e]

        @pl.when(first_ref[e] == 0)
        def _():
            for cp in w_copies(e, wslot):
                cp.start()

        for cp in w_copies(e, wslot):
            cp.wait()
        wg_s[...] = wg_f[wslot].astype(BF16)
        wu_s[...] = wu_f[wslot].astype(BF16)
        wd_s[...] = wd_f[wslot].astype(BF16)
        e_next = next_ref[e]

        @pl.when(e_next < N_EXPERTS)
        def _():
            for cp in w_copies(e_next, 1 - wslot):
                cp.start(priority=DMA_QUEUES - 1)

        def block(i, carry):
            blk = first_ref[e] + i
            slot = blk & 1
            x_copy(blk, slot).wait()

            @pl.when(blk + 1 < n_used)
            def _():
                x_copy(blk + 1, 1 - slot).start()

            @pl.when(blk >= 2)
            def _():
                y_copy(blk - 2, slot).wait()

            xb = xbuf[slot].astype(BF16)
            hmid = (_silu(_dot(xb, wg_s[...])) * _dot(xb, wu_s[...])).astype(BF16)
            ybuf[slot] = _dot(hmid, wd_s[...])
            y_copy(blk, slot).start()
            return carry

        lax.fori_loop(0, nblk_ref[e], block, 0)

    @pl.when(e == pl.num_programs(0) - 1)
    def _():
        @pl.when(n_used >= 2)
        def _():
            y_copy(n_used - 2, (n_used - 2) & 1).wait()

        y_copy(n_used - 1, (n_used - 1) & 1).wait()


def _experts(first_block, n_blocks_of, n_used, next_expert, wslot, xs, w_gate, w_up, w_down, total_blocks):
    n_exp, d, ff = w_gate.shape
    hbm = pl.BlockSpec(memory_space=pl.ANY)
    return pl.pallas_call(
        _expert_kernel,
        grid_spec=pltpu.PrefetchScalarGridSpec(
            num_scalar_prefetch=5,
            grid=(n_exp,),
            in_specs=[hbm, hbm, hbm, hbm],
            out_specs=hbm,
            scratch_shapes=[pltpu.VMEM((2, EXPERT_ROWS, d), F32), pltpu.VMEM((2, EXPERT_ROWS, d), F32),
                            pltpu.VMEM((2, d, ff), F32), pltpu.VMEM((2, d, ff), F32),
                            pltpu.VMEM((2, ff, d), F32),
                            pltpu.VMEM((d, ff), BF16), pltpu.VMEM((d, ff), BF16),
                            pltpu.VMEM((ff, d), BF16),
                            pltpu.SemaphoreType.DMA((2,)), pltpu.SemaphoreType.DMA((2,)),
                            pltpu.SemaphoreType.DMA((2,))]),
        out_shape=jax.ShapeDtypeStruct((total_blocks * EXPERT_ROWS, d), F32),
        compiler_params=_params("arbitrary"),
        name="experts",
    )(first_block, n_blocks_of, n_used, next_expert, wslot, w_gate, w_up, w_down, xs)


def _combine_kernel(dest_ref, gate_ref, x_ref, ys_ref, wsg_ref, wsu_ref, wsd_ref, lg_ref, lb_ref,
                    o_ref, buf, sem):
    tm, d = x_ref.shape
    groups, chunks = tm // SUBLANES, d // LANES
    dma_views = [buf.at[k].reshape(groups, chunks, SUBLANES, 1, LANES) for k in range(TOP_K)]
    load_views = [buf.at[k].reshape(groups, chunks, SUBLANES, LANES) for k in range(TOP_K)]

    def copies(g):
        return [pltpu.make_async_copy(_tile_row(ys_ref, dest_ref[k, g * SUBLANES + s]),
                                      dma_views[k].at[g, :, s], sem)
                for s in range(SUBLANES) for k in range(TOP_K)]

    def start_rows(g, c):
        for n, cp in enumerate(copies(g)):
            cp.start(priority=n % DMA_QUEUES)
        return c

    def wait_rows(g, c):
        for cp in copies(g):
            cp.wait()
        return c

    lax.fori_loop(0, groups, start_rows, 0)

    x = x_ref[...]
    xb = x.astype(BF16)
    hmid = (_silu(_dot(xb, wsg_ref[...])) * _dot(xb, wsu_ref[...])).astype(BF16)
    acc = ALPHA * x + _dot(hmid, wsd_ref[...])

    lax.fori_loop(0, groups, wait_rows, 0)
    gates = gate_ref[...]

    def rows_of(k):
        return jnp.concatenate([load_views[k][:, c].reshape(tm, LANES) for c in range(chunks)], axis=1)

    routed = rows_of(0) * gates[:, 0:1]
    for k in range(1, TOP_K):
        routed += rows_of(k) * gates[:, k:k + 1]
    o_ref[...] = _layer_norm(acc + routed, lg_ref[...], lb_ref[...])


def _combine(dest, gates_t, x, ys, ws_gate, ws_up, ws_down, ln_g, ln_b):
    n_tok, d = x.shape
    tm = TM_COMBINE
    row = lambda i: (i, 0)
    const = lambda i: (0, 0)
    return pl.pallas_call(
        _combine_kernel,
        grid=(n_tok // tm,),
        in_specs=[pl.BlockSpec((TOP_K, tm), lambda i: (0, i), memory_space=pltpu.SMEM),
                  pl.BlockSpec((tm, TOP_K), row),
                  pl.BlockSpec((tm, d), row),
                  pl.BlockSpec(memory_space=pl.ANY),
                  pl.BlockSpec(ws_gate.shape, const),
                  pl.BlockSpec(ws_up.shape, const),
                  pl.BlockSpec(ws_down.shape, const),
                  pl.BlockSpec(ln_g.shape, const),
                  pl.BlockSpec(ln_b.shape, const)],
        out_specs=pl.BlockSpec((tm, d), row),
        out_shape=jax.ShapeDtypeStruct((n_tok, d), F32),
        scratch_shapes=[pltpu.VMEM((TOP_K, tm * (d // LANES), LANES), F32), pltpu.SemaphoreType.DMA(())],
        compiler_params=_params("arbitrary"),
        name="combine_shared_ln3",
    )(dest, gates_t, x, _tile_view(ys), ws_gate, ws_up, ws_down, ln_g, ln_b)


def _layer(x, mem, w_in, conv_w, g_attn_out, g_conv_out, w_out, ln1_g, ln1_b, w_q_mem, w_kv_mem,
           w_o_mem, ln2_g, ln2_b, w_router, router_bias, w_gate, w_up, w_down, ws_gate, ws_up,
           ws_down, ln3_g, ln3_b):
    batch, seq, d = x.shape
    mem_len = mem.shape[1]
    n_tok = batch * seq
    xf = x.reshape(n_tok, d)
    row = lambda v: v.reshape(1, -1)

    proj = _matmul(xf, w_in.astype(BF16), F32, TM_PROJ, TN_PROJ)
    attn = _dilated_attention(proj, batch, seq)
    x1 = _mix_out(attn, proj, xf, w_out.astype(BF16), conv_w, row(g_attn_out), row(g_conv_out),
                  row(ln1_g), row(ln1_b), seq)

    kv = _matmul(mem.reshape(batch * mem_len, d), w_kv_mem.astype(BF16), BF16,
                 mem_len, 1024)
    x2 = _xattn(x1, w_q_mem.astype(BF16), kv, w_o_mem.astype(BF16), row(ln2_g), row(ln2_b), seq,
                mem_len)

    idx, gates, rank, cnt = _router(x2, w_router.T, router_bias.reshape(N_EXPERTS, 1))
    counts = cnt[:, 0].astype(I32)
    padded = (counts + EXPERT_ROWS - 1) // EXPERT_ROWS * EXPERT_ROWS
    seg_end = jnp.cumsum(padded)
    seg_start = seg_end - padded
    experts = jnp.arange(N_EXPERTS, dtype=I32)
    dest = rank + jnp.sum(jnp.where(idx[..., None] == experts, seg_start, 0), axis=-1)
    total_blocks = -(-(n_tok * TOP_K) // EXPERT_ROWS) + N_EXPERTS
    n_used = (seg_end[-1:] // EXPERT_ROWS).astype(I32)

    xs = _dispatch(seg_start + counts, dest, x2, total_blocks * EXPERT_ROWS)
    nonempty = counts > 0
    first_from = lax.cummin(jnp.where(nonempty, experts, N_EXPERTS), reverse=True)
    next_expert = jnp.concatenate([first_from[1:], jnp.full((1,), N_EXPERTS, I32)])
    wslot = (jnp.cumsum(nonempty.astype(I32)) - 1) & 1
    ys = _experts(seg_start // EXPERT_ROWS, padded // EXPERT_ROWS, n_used, next_expert, wslot, xs,
                  w_gate, w_up, w_down, total_blocks)
    out = _combine(dest, gates.T, x2, ys, ws_gate.astype(BF16), ws_up.astype(BF16),
                   ws_down.astype(BF16), row(ln3_g), row(ln3_b))
    return out.reshape(batch, seq, d)


def kernel(x, mem, w_in, conv_w, g_attn_out, g_conv_out, w_out, ln1_g, ln1_b, w_q_mem, w_kv_mem, w_o_mem, ln2_g, ln2_b, w_router, router_bias, w_gate, w_up, w_down, ws_gate, ws_up, ws_down, ln3_g, ln3_b):
    for l in range(DEPTH):
        x = _layer(x, mem, w_in[l], conv_w[l], g_attn_out[l], g_conv_out[l], w_out[l], ln1_g[l],
                   ln1_b[l], w_q_mem[l], w_kv_mem[l], w_o_mem[l], ln2_g[l], ln2_b[l], w_router[l],
                   router_bias[l], w_gate[l], w_up[l], w_down[l], ws_gate[l], ws_up[l], ws_down[l],
                   ln3_g[l], ln3_b[l])
    return x
```

```python
import functools

import jax
import jax.numpy as jnp
from jax import lax
from jax.experimental import pallas as pl
from jax.experimental.pallas import tpu as pltpu

F32 = jnp.float32
BF16 = jnp.bfloat16
I32 = jnp.int32
U32 = jnp.uint32

HEAD_DIM = 128
N_ATTN_HEADS = 8
ATTN_WIDTH = N_ATTN_HEADS * HEAD_DIM
CONV_WIDTH = 1024
PROJ_PARTS = 6
DILATED_PATTERNS = ((128, 1), (512, 4), (2048, 16))
ATTN_BLOCK = 128
ATTN_SPAN = 16 * ATTN_BLOCK
N_MEM_HEADS = 4
N_EXPERTS = 64
TOP_K = 8
N_GROUPS = 8
GROUP_SIZE = N_EXPERTS // N_GROUPS
TOPK_GROUPS = 4
ROUTED_SCALE = 2.5
LN_EPS = 1e-5
RMS_EPS = 1e-6
DEPTH = 1
ALPHA = (2.0 * DEPTH) ** 0.25

LANES = 128
SUBLANES = 8
VMEM_LIMIT = 56 * 1024 * 1024
DMA_QUEUES = 2
EXPERT_ROWS = 256
ZERO_GROUPS = EXPERT_ROWS // SUBLANES + 1
TM_PROJ, TN_PROJ = 1024, 512
TM_MIX = 256
TM_XATTN = 256
TM_ROUTER = 512
TM_DISPATCH = 256
TM_COMBINE = 128


def _params(*semantics):
    return pltpu.CompilerParams(dimension_semantics=semantics, vmem_limit_bytes=VMEM_LIMIT)


def _layer_norm(r, g, b):
    mu = jnp.mean(r, axis=-1, keepdims=True)
    c = r - mu
    var = jnp.mean(c * c, axis=-1, keepdims=True)
    return c * lax.rsqrt(var + LN_EPS) * g + b


def _rms_norm(v, g):
    return v * lax.rsqrt(jnp.mean(v * v, axis=-1, keepdims=True) + RMS_EPS) * g


def _dot(a, b):
    return jnp.dot(a, b, preferred_element_type=F32)


def _dot_nt(a, b):
    return lax.dot_general(a, b, (((1,), (1,)), ((), ())), preferred_element_type=F32)


def _silu(v):
    return v / (1.0 + jnp.exp(-v))


def _matmul_kernel(a_ref, w_ref, o_ref, a_bf):
    @pl.when(pl.program_id(1) == 0)
    def _():
        a_bf[...] = a_ref[...].astype(BF16)

    o_ref[...] = _dot(a_bf[...], w_ref[...]).astype(o_ref.dtype)


def _matmul(a, w, out_dtype, tm, tn):
    m, k = a.shape
    n = w.shape[1]
    return pl.pallas_call(
        _matmul_kernel,
        grid=(m // tm, n // tn),
        in_specs=[pl.BlockSpec((tm, k), lambda i, j: (i, 0)),
                  pl.BlockSpec((k, tn), lambda i, j: (0, j))],
        out_specs=pl.BlockSpec((tm, tn), lambda i, j: (i, j)),
        out_shape=jax.ShapeDtypeStruct((m, n), out_dtype),
        scratch_shapes=[pltpu.VMEM((tm, k), BF16)],
        compiler_params=_params("parallel", "arbitrary"),
        name="matmul",
    )(a, w)


def _attn_kernel(q_ref, kp_ref, kc_ref, vp_ref, vc_ref, o_ref, kk, vv, o_acc, l_acc):
    j = pl.program_id(1)
    blk, span = ATTN_BLOCK, ATTN_SPAN
    kk[0:span, :] = kp_ref[...]
    kk[span:, :] = kc_ref[...]
    vv[0:span, :] = vp_ref[...]
    vv[span:, :] = vc_ref[...]

    row = lax.broadcasted_iota(I32, (blk, 2 * blk), 0)
    col = lax.broadcasted_iota(I32, (blk, 2 * blk), 1)
    dist = row + blk - col
    in_window = (dist >= 0) & (dist <= blk)
    own_block = col >= blk
    scale = HEAD_DIM ** -0.5

    order = sorted(range(len(DILATED_PATTERNS)), key=lambda i: -DILATED_PATTERNS[i][1])
    assert DILATED_PATTERNS[order[-1]][1] == 1
    sub_blocks = span // blk
    for bi in order:
        dil = DILATED_PATTERNS[bi][1]
        shift = dil.bit_length() - 1

        def body(t, carry, bi=bi, dil=dil, shift=shift):
            r = t & (dil - 1)
            n = t >> shift
            base = n * (blk * dil) + r
            if dil == 1:
                base = pl.multiple_of(base, blk)
                q_rows, kv_rows = pl.ds(base, blk), pl.ds(span + base - blk, 2 * blk)
            else:
                q_rows = pl.ds(base, blk, stride=dil)
                kv_rows = pl.ds(span + base - blk * dil, 2 * blk, stride=dil)
            q = q_ref[q_rows, :].astype(BF16)
            k = kk[kv_rows, :].astype(BF16)
            v = vv[kv_rows, :].astype(BF16)
            s = _dot_nt(q, k) * scale
            valid = in_window & (own_block | (j > 0) | (n > 0))
            s = jnp.where(valid, s, -jnp.inf)
            m = jnp.max(s, axis=-1, keepdims=True)
            p = jnp.exp(s - m)
            l = jnp.sum(p, axis=-1, keepdims=True)
            o = _dot((p / l).astype(BF16), v)
            lse = jnp.broadcast_to(m + jnp.log(l), (blk, LANES))
            if dil != 1:
                o_acc[bi, q_rows, :] = o
                l_acc[bi, q_rows, :] = lse
                return carry
            others = [i for i in order if i != bi]
            lses = [lse] + [l_acc[i, q_rows, :] for i in others]
            outs = [o] + [o_acc[i, q_rows, :] for i in others]
            top = jnp.maximum(jnp.maximum(lses[0], lses[1]), lses[2])
            es = [jnp.exp(v_ - top) for v_ in lses]
            den = es[0] + es[1] + es[2]
            o_ref[q_rows, :] = ((es[0] / den) * outs[0] + (es[1] / den) * outs[1] + (es[2] / den) * outs[2])
            return carry

        lax.fori_loop(0, sub_blocks, body, 0, unroll=True)


def _dilated_attention(proj, batch, seq):
    n_tok, _ = proj.shape
    span = ATTN_SPAN
    spans = seq // span
    heads = N_ATTN_HEADS

    def spec(part, prev):
        def index(b, j, h):
            return (b * spans + (jnp.maximum(j - 1, 0) if prev else j), part * heads + h)
        return pl.BlockSpec((span, HEAD_DIM), index)

    return pl.pallas_call(
        _attn_kernel,
        grid=(batch, spans, heads),
        in_specs=[spec(0, False), spec(1, True), spec(1, False), spec(2, True), spec(2, False)],
        out_specs=pl.BlockSpec((span, HEAD_DIM), lambda b, j, h: (b * spans + j, h)),
        out_shape=jax.ShapeDtypeStruct((n_tok, ATTN_WIDTH), F32),
        scratch_shapes=[pltpu.VMEM((2 * span, HEAD_DIM), F32), pltpu.VMEM((2 * span, HEAD_DIM), F32),
                        pltpu.VMEM((len(DILATED_PATTERNS), span, HEAD_DIM), F32),
                        pltpu.VMEM((len(DILATED_PATTERNS), span, HEAD_DIM), F32)],
        compiler_params=_params("parallel", "parallel", "parallel"),
        name="dilated_attn",
    )(proj, proj, proj, proj, proj)


def _mix_out_kernel(a_ref, b_ref, c_ref, h_ref, cp_ref, hp_ref, x_ref,
                    w_ref, cw_ref, ga_ref, gc_ref, lg_ref, lb_ref, o_ref, *, tiles_per_seq):
    i = pl.program_id(0)
    z = c_ref[...] * h_ref[...]
    zp = jnp.where(i % tiles_per_seq == 0, 0.0, cp_ref[...] * hp_ref[...])
    zz = jnp.concatenate([zp, z], axis=0)
    z1 = pltpu.roll(zz, 1, axis=0)[SUBLANES:]
    z2 = pltpu.roll(zz, 2, axis=0)[SUBLANES:]
    conv = b_ref[...] * (cw_ref[0:1, :] * z2 + cw_ref[1:2, :] * z1 + cw_ref[2:3, :] * z)

    mixed = jnp.concatenate([_rms_norm(a_ref[...], ga_ref[...]), _rms_norm(conv, gc_ref[...])], axis=1)
    y = _dot(mixed.astype(BF16), w_ref[...])
    o_ref[...] = _layer_norm(ALPHA * x_ref[...] + y, lg_ref[...], lb_ref[...])


def _mix_out(attn, proj, x, w_out, conv_w, g_attn, g_conv, ln_g, ln_b, seq):
    n_tok, d = x.shape
    tm = TM_MIX
    rows8 = tm // SUBLANES
    row = lambda i: (i, 0)
    const = lambda i: (0, 0)
    prev = lambda part: (lambda i: (jnp.maximum(i * rows8 - 1, 0), part))
    return pl.pallas_call(
        functools.partial(_mix_out_kernel, tiles_per_seq=seq // tm),
        grid=(n_tok // tm,),
        in_specs=[
            pl.BlockSpec((tm, ATTN_WIDTH), row),
            pl.BlockSpec((tm, CONV_WIDTH), lambda i: (i, 3)),
            pl.BlockSpec((tm, CONV_WIDTH), lambda i: (i, 4)),
            pl.BlockSpec((tm, CONV_WIDTH), lambda i: (i, 5)),
            pl.BlockSpec((SUBLANES, CONV_WIDTH), prev(4)),
            pl.BlockSpec((SUBLANES, CONV_WIDTH), prev(5)),
            pl.BlockSpec((tm, d), row),
            pl.BlockSpec(w_out.shape, const),
            pl.BlockSpec(conv_w.shape, const),
            pl.BlockSpec(g_attn.shape, const),
            pl.BlockSpec(g_conv.shape, const),
            pl.BlockSpec(ln_g.shape, const),
            pl.BlockSpec(ln_b.shape, const)],
        out_specs=pl.BlockSpec((tm, d), row),
        out_shape=jax.ShapeDtypeStruct((n_tok, d), F32),
        compiler_params=_params("parallel"),
        name="mix_out_ln1",
    )(attn, proj, proj, proj, proj, proj, x, w_out, conv_w, g_attn, g_conv, ln_g, ln_b)


def _pack_halves(v):
    c = v.shape[1] // 2
    as_bits = lambda part: lax.bitcast_convert_type(part.astype(BF16).astype(F32), U32)
    return (as_bits(v[:, :c]) >> 16) | (as_bits(v[:, c:]) & jnp.uint32(0xFFFF0000))


def _unpack_halves(p):
    lo = lax.bitcast_convert_type(p << 16, F32).astype(BF16)
    hi = lax.bitcast_convert_type(p & jnp.uint32(0xFFFF0000), F32).astype(BF16)
    return lo, hi


def _xattn_kernel(x_ref, wq_ref, k_ref, v_ref, wo_ref, lg_ref, lb_ref, o_ref, op_ref):
    x = x_ref[...]
    d = x.shape[1]
    hd = d // N_MEM_HEADS
    q = _dot(x.astype(BF16), wq_ref[...]).astype(BF16)
    scale = hd ** -0.5
    outs = []
    for h in range(N_MEM_HEADS):
        sl = slice(h * hd, (h + 1) * hd)
        s = _dot_nt(q[:, sl], k_ref[:, sl]) * scale
        m = jnp.max(s, axis=-1, keepdims=True)
        p = jnp.exp(s - m)
        p = p / jnp.sum(p, axis=-1, keepdims=True)
        outs.append(_dot(p.astype(BF16), v_ref[:, sl]).astype(BF16))
    y = _dot(jnp.concatenate(outs, axis=1), wo_ref[...])
    out = _layer_norm(ALPHA * x + y, lg_ref[...], lb_ref[...])
    o_ref[...] = out
    op_ref[...] = _pack_halves(out)


def _xattn(x, w_q, kv, w_o, ln_g, ln_b, seq, mem_len):
    n_tok, d = x.shape
    tm = TM_XATTN
    row = lambda i: (i, 0)
    const = lambda i: (0, 0)
    tiles_per_seq = seq // tm
    return pl.pallas_call(
        _xattn_kernel,
        grid=(n_tok // tm,),
        in_specs=[pl.BlockSpec((tm, d), row),
                  pl.BlockSpec(w_q.shape, const),
                  pl.BlockSpec((mem_len, d), lambda i: (i // tiles_per_seq, 0)),
                  pl.BlockSpec((mem_len, d), lambda i: (i // tiles_per_seq, 1)),
                  pl.BlockSpec(w_o.shape, const),
                  pl.BlockSpec(ln_g.shape, const),
                  pl.BlockSpec(ln_b.shape, const)],
        out_specs=[pl.BlockSpec((tm, d), row), pl.BlockSpec((tm, d // 2), row)],
        out_shape=[jax.ShapeDtypeStruct((n_tok, d), F32), jax.ShapeDtypeStruct((n_tok, d // 2), U32)],
        compiler_params=_params("parallel"),
        name="xattn_ln2",
    )(x, w_q, kv, kv, w_o, ln_g, ln_b)


def _first_argmax(vals, index, sentinel):
    m = jnp.max(vals, axis=0, keepdims=True)
    i = jnp.min(jnp.where(vals == m, index, sentinel), axis=0, keepdims=True)
    return m, i


def _router_kernel(x_ref, wt_ref, bias_ref, idx_ref, gate_ref, rank_ref, cnt_ref):
    step = pl.program_id(0)
    tm = x_ref.shape[0]

    @pl.when(step == 0)
    def _():
        cnt_ref[...] = jnp.zeros_like(cnt_ref)

    x = x_ref[...]
    xh = x.astype(BF16)
    xl = (x - xh.astype(F32)).astype(BF16)
    w = wt_ref[...]
    wh = w.astype(BF16)
    wl = (w - wh.astype(F32)).astype(BF16)
    both = _dot_nt(jnp.concatenate([wh, wl], axis=0), xh)
    logits = both[:N_EXPERTS] + both[N_EXPERTS:] + _dot_nt(wh, xl)
    scores = 1.0 / (1.0 + jnp.exp(-logits))
    choice = scores + bias_ref[...]

    sub = lax.broadcasted_iota(I32, (GROUP_SIZE, tm), 0)
    group_scores = []
    for g in range(N_GROUPS):
        c = choice[g * GROUP_SIZE:(g + 1) * GROUP_SIZE, :]
        m1, i1 = _first_argmax(c, sub, GROUP_SIZE)
        m2 = jnp.max(jnp.where(sub == i1, -jnp.inf, c), axis=0, keepdims=True)
        group_scores.append(m1 + m2)
    gs = jnp.concatenate(group_scores, axis=0)
    gidx = lax.broadcasted_iota(I32, (N_GROUPS, tm), 0)
    gsel = jnp.zeros((N_GROUPS, tm), F32)
    for _ in range(TOPK_GROUPS):
        _, gi = _first_argmax(gs, gidx, N_GROUPS)
        hit = gidx == gi
        gsel = jnp.where(hit, 1.0, gsel)
        gs = jnp.where(hit, -jnp.inf, gs)
    masked = jnp.concatenate(
        [jnp.where(gsel[g:g + 1, :] > 0.0, choice[g * GROUP_SIZE:(g + 1) * GROUP_SIZE, :], -jnp.inf)
         for g in range(N_GROUPS)], axis=0)

    eidx = lax.broadcasted_iota(I32, (N_EXPERTS, tm), 0)
    hits, idxs, ws = [], [], []
    for _ in range(TOP_K):
        _, ei = _first_argmax(masked, eidx, N_EXPERTS)
        hit = eidx == ei
        hits.append(hit)
        idxs.append(ei)
        ws.append(jnp.sum(jnp.where(hit, scores, 0.0), axis=0, keepdims=True))
        masked = jnp.where(hit, -jnp.inf, masked)
    wsum = ws[0]
    for w in ws[1:]:
        wsum = wsum + w
    gate_ref[...] = jnp.concatenate([w / wsum * ROUTED_SCALE for w in ws], axis=0)
    idx_ref[...] = jnp.concatenate(idxs, axis=0)

    sel = hits[0]
    for hit in hits[1:]:
        sel = sel | hit
    self32 = jnp.where(sel, 1.0, 0.0)
    t_row = lax.broadcasted_iota(I32, (tm, tm), 0)
    t_col = lax.broadcasted_iota(I32, (tm, tm), 1)
    tri = jnp.where(t_row < t_col, 1.0, 0.0).astype(BF16)
    before = _dot(self32.astype(BF16), tri) + cnt_ref[:, 0:1]
    rank_ref[...] = jnp.concatenate(
        [jnp.sum(jnp.where(hit, before, 0.0), axis=0, keepdims=True) for hit in hits],
        axis=0).astype(I32)
    cnt_ref[...] += jnp.sum(self32, axis=1, keepdims=True)


def _router(x, w_router_t, bias_col):
    n_tok, d = x.shape
    tm = TM_ROUTER
    tok = lambda i: (0, i)
    return pl.pallas_call(
        _router_kernel,
        grid=(n_tok // tm,),
        in_specs=[pl.BlockSpec((tm, d), lambda i: (i, 0)),
                  pl.BlockSpec(w_router_t.shape, lambda i: (0, 0)),
                  pl.BlockSpec(bias_col.shape, lambda i: (0, 0))],
        out_specs=[pl.BlockSpec((TOP_K, tm), tok), pl.BlockSpec((TOP_K, tm), tok),
                   pl.BlockSpec((TOP_K, tm), tok),
                   pl.BlockSpec((N_EXPERTS, LANES), lambda i: (0, 0))],
        out_shape=[jax.ShapeDtypeStruct((TOP_K, n_tok), I32),
                   jax.ShapeDtypeStruct((TOP_K, n_tok), F32),
                   jax.ShapeDtypeStruct((TOP_K, n_tok), I32),
                   jax.ShapeDtypeStruct((N_EXPERTS, LANES), F32)],
        compiler_params=_params("arbitrary"),
        name="router",
    )(x, w_router_t, bias_col)


def _tile_view(x):
    r, d = x.shape
    return (x.reshape(r // SUBLANES, SUBLANES, d // LANES, LANES).transpose(0, 2, 1, 3)
            .reshape(r // SUBLANES, d // LANES, SUBLANES, 1, LANES))


def _row_view(v):
    g, c, s, _, l = v.shape
    return v.reshape(g, c, s, l).transpose(0, 2, 1, 3).reshape(g * s, c * l)


def _tile_row(view_ref, row):
    return view_ref.at[row >> 3, :, row & (SUBLANES - 1)]


def _dispatch_kernel(pad_ref, dest_ref, x_ref, xs_ref, zero_ref, sem):
    step = pl.program_id(0)
    groups = x_ref.shape[0]

    @pl.when(step == 0)
    def _():
        zero_ref[...] = jnp.zeros_like(zero_ref)

        def fill(e):
            return pltpu.make_async_copy(zero_ref, xs_ref.at[pl.ds(pad_ref[e] >> 3, ZERO_GROUPS)], sem)

        def start(e, c):
            fill(e).start()
            return c

        def wait(e, c):
            fill(e).wait()
            return c

        lax.fori_loop(0, N_EXPERTS, start, 0)
        lax.fori_loop(0, N_EXPERTS, wait, 0)

    def copies(g):
        return [pltpu.make_async_copy(x_ref.at[g, :, s], _tile_row(xs_ref, dest_ref[k, g * SUBLANES + s]), sem)
                for s in range(SUBLANES) for k in range(TOP_K)]

    def start_rows(g, c):
        for n, cp in enumerate(copies(g)):
            cp.start(priority=n % DMA_QUEUES)
        return c

    def wait_rows(g, c):
        for cp in copies(g):
            cp.wait()
        return c

    lax.fori_loop(0, groups, start_rows, 0)
    lax.fori_loop(0, groups, wait_rows, 0)


def _dispatch(pad_start, dest, x, n_rows):
    n_tok, w = x.shape
    tm = TM_DISPATCH
    chunks = w // LANES
    block = (tm // SUBLANES, chunks, SUBLANES, 1, LANES)
    total_groups = n_rows // SUBLANES + ZERO_GROUPS
    out = pl.pallas_call(
        _dispatch_kernel,
        grid_spec=pltpu.PrefetchScalarGridSpec(
            num_scalar_prefetch=1,
            grid=(n_tok // tm,),
            in_specs=[pl.BlockSpec((TOP_K, tm), lambda i, pad: (0, i), memory_space=pltpu.SMEM),
                      pl.BlockSpec(block, lambda i, pad: (i, 0, 0, 0, 0))],
            out_specs=pl.BlockSpec(memory_space=pl.ANY),
            scratch_shapes=[pltpu.VMEM((ZERO_GROUPS, chunks, SUBLANES, 1, LANES), x.dtype),
                            pltpu.SemaphoreType.DMA(())]),
        out_shape=jax.ShapeDtypeStruct((total_groups, chunks, SUBLANES, 1, LANES), x.dtype),
        compiler_params=_params("arbitrary"),
        name="dispatch",
    )(pad_start, dest, _tile_view(x))
    return _row_view(out)


def _expert_kernel(first_ref, nblk_ref, nused_ref, next_ref, wslot_ref, wg_hbm, wu_hbm, wd_hbm, xs_hbm, ys_hbm,
                   xbuf, ybuf, wg_f, wu_f, wd_f, wg_s, wu_s, wd_s, xsem, ysem, wsem):
    e = pl.program_id(0)
    n_used = nused_ref[0]

    def x_copy(blk, slot):
        rows = pl.ds(pl.multiple_of(blk * EXPERT_ROWS, EXPERT_ROWS), EXPERT_ROWS)
        return pltpu.make_async_copy(xs_hbm.at[rows, :], xbuf.at[slot], xsem.at[slot])

    def y_copy(blk, slot):
        rows = pl.ds(pl.multiple_of(blk * EXPERT_ROWS, EXPERT_ROWS), EXPERT_ROWS)
        return pltpu.make_async_copy(ybuf.at[slot], ys_hbm.at[rows, :], ysem.at[slot])

    def w_copies(expert, slot):
        return [pltpu.make_async_copy(src.at[expert], dst.at[slot], wsem.at[slot])
                for src, dst in ((wg_hbm, wg_f), (wu_hbm, wu_f), (wd_hbm, wd_f))]

    @pl.when(e == 0)
    def _():
        x_copy(0, 0).start()

    @pl.when(nblk_ref[e] > 0)
    def _():
        wslot = wslot_ref[e]

        @pl.when(first_ref[e] == 0)
        def _():
            for cp in w_copies(e, wslot):
                cp.start()

        for cp in w_copies(e, wslot):
            cp.wait()
        wg_s[...] = wg_f[wslot].astype(BF16)
        wu_s[...] = wu_f[wslot].astype(BF16)
        wd_s[...] = wd_f[wslot].astype(BF16)
        e_next = next_ref[e]

        @pl.when(e_next < N_EXPERTS)
        def _():
            for cp in w_copies(e_next, 1 - wslot):
                cp.start(priority=DMA_QUEUES - 1)

        def block(i, carry):
            blk = first_ref[e] + i
            slot = blk & 1
            x_copy(blk, slot).wait()

            @pl.when(blk + 1 < n_used)
            def _():
                x_copy(blk + 1, 1 - slot).start()

            @pl.when(blk >= 2)
            def _():
                y_copy(blk - 2, slot).wait()

            lo, hi = _unpack_halves(xbuf[slot])
            half = lo.shape[1]
            gate = _dot(lo, wg_s[:half, :]) + _dot(hi, wg_s[half:, :])
            up = _dot(lo, wu_s[:half, :]) + _dot(hi, wu_s[half:, :])
            hmid = (_silu(gate) * up).astype(BF16)
            ybuf[slot] = _dot(hmid, wd_s[...])
            y_copy(blk, slot).start()
            return carry

        lax.fori_loop(0, nblk_ref[e], block, 0)

    @pl.when(e == pl.num_programs(0) - 1)
    def _():
        @pl.when(n_used >= 2)
        def _():
            y_copy(n_used - 2, (n_used - 2) & 1).wait()

        y_copy(n_used - 1, (n_used - 1) & 1).wait()


def _experts(first_block, n_blocks_of, n_used, next_expert, wslot, xs, w_gate, w_up, w_down, total_blocks):
    n_exp, d, ff = w_gate.shape
    hbm = pl.BlockSpec(memory_space=pl.ANY)
    return pl.pallas_call(
        _expert_kernel,
        grid_spec=pltpu.PrefetchScalarGridSpec(
            num_scalar_prefetch=5,
            grid=(n_exp,),
            in_specs=[hbm, hbm, hbm, hbm],
            out_specs=hbm,
            scratch_shapes=[pltpu.VMEM((2, EXPERT_ROWS, d // 2), U32), pltpu.VMEM((2, EXPERT_ROWS, d), F32),
                            pltpu.VMEM((2, d, ff), F32), pltpu.VMEM((2, d, ff), F32),
                            pltpu.VMEM((2, ff, d), F32),
                            pltpu.VMEM((d, ff), BF16), pltpu.VMEM((d, ff), BF16),
                            pltpu.VMEM((ff, d), BF16),
                            pltpu.SemaphoreType.DMA((2,)), pltpu.SemaphoreType.DMA((2,)),
                            pltpu.SemaphoreType.DMA((2,))]),
        out_shape=jax.ShapeDtypeStruct((total_blocks * EXPERT_ROWS, d), F32),
        compiler_params=_params("arbitrary"),
        name="experts",
    )(first_block, n_blocks_of, n_used, next_expert, wslot, w_gate, w_up, w_down, xs)


def _combine_kernel(dest_ref, gate_ref, x_ref, ys_ref, wsg_ref, wsu_ref, wsd_ref, lg_ref, lb_ref,
                    o_ref, buf, sem):
    tm, d = x_ref.shape
    groups, chunks = tm // SUBLANES, d // LANES
    dma_views = [buf.at[k].reshape(groups, chunks, SUBLANES, 1, LANES) for k in range(TOP_K)]
    load_views = [buf.at[k].reshape(groups, chunks, SUBLANES, LANES) for k in range(TOP_K)]

    def copies(g):
        return [pltpu.make_async_copy(_tile_row(ys_ref, dest_ref[k, g * SUBLANES + s]),
                                      dma_views[k].at[g, :, s], sem)
                for s in range(SUBLANES) for k in range(TOP_K)]

    def start_rows(g, c):
        for n, cp in enumerate(copies(g)):
            cp.start(priority=n % DMA_QUEUES)
        return c

    def wait_rows(g, c):
        for cp in copies(g):
            cp.wait()
        return c

    lax.fori_loop(0, groups, start_rows, 0)

    x = x_ref[...]
    xb = x.astype(BF16)
    hmid = (_silu(_dot(xb, wsg_ref[...])) * _dot(xb, wsu_ref[...])).astype(BF16)
    acc = ALPHA * x + _dot(hmid, wsd_ref[...])

    lax.fori_loop(0, groups, wait_rows, 0)
    gates = gate_ref[...]

    def rows_of(k):
        return jnp.concatenate([load_views[k][:, c].reshape(tm, LANES) for c in range(chunks)], axis=1)

    routed = rows_of(0) * gates[:, 0:1]
    for k in range(1, TOP_K):
        routed += rows_of(k) * gates[:, k:k + 1]
    o_ref[...] = _layer_norm(acc + routed, lg_ref[...], lb_ref[...])


def _combine(dest, gates_t, x, ys, ws_gate, ws_up, ws_down, ln_g, ln_b):
    n_tok, d = x.shape
    tm = TM_COMBINE
    row = lambda i: (i, 0)
    const = lambda i: (0, 0)
    return pl.pallas_call(
        _combine_kernel,
        grid=(n_tok // tm,),
        in_specs=[pl.BlockSpec((TOP_K, tm), lambda i: (0, i), memory_space=pltpu.SMEM),
                  pl.BlockSpec((tm, TOP_K), row),
                  pl.BlockSpec((tm, d), row),
                  pl.BlockSpec(memory_space=pl.ANY),
                  pl.BlockSpec(ws_gate.shape, const),
                  pl.BlockSpec(ws_up.shape, const),
                  pl.BlockSpec(ws_down.shape, const),
                  pl.BlockSpec(ln_g.shape, const),
                  pl.BlockSpec(ln_b.shape, const)],
        out_specs=pl.BlockSpec((tm, d), row),
        out_shape=jax.ShapeDtypeStruct((n_tok, d), F32),
        scratch_shapes=[pltpu.VMEM((TOP_K, tm * (d // LANES), LANES), F32), pltpu.SemaphoreType.DMA(())],
        compiler_params=_params("arbitrary"),
        name="combine_shared_ln3",
    )(dest, gates_t, x, _tile_view(ys), ws_gate, ws_up, ws_down, ln_g, ln_b)


def _layer(x, mem, w_in, conv_w, g_attn_out, g_conv_out, w_out, ln1_g, ln1_b, w_q_mem, w_kv_mem,
           w_o_mem, ln2_g, ln2_b, w_router, router_bias, w_gate, w_up, w_down, ws_gate, ws_up,
           ws_down, ln3_g, ln3_b):
    batch, seq, d = x.shape
    mem_len = mem.shape[1]
    n_tok = batch * seq
    xf = x.reshape(n_tok, d)
    row = lambda v: v.reshape(1, -1)

    proj = _matmul(xf, w_in.astype(BF16), F32, TM_PROJ, TN_PROJ)
    attn = _dilated_attention(proj, batch, seq)
    x1 = _mix_out(attn, proj, xf, w_out.astype(BF16), conv_w, row(g_attn_out), row(g_conv_out),
                  row(ln1_g), row(ln1_b), seq)

    kv = _matmul(mem.reshape(batch * mem_len, d), w_kv_mem.astype(BF16), BF16,
                 mem_len, 1024)
    x2, x2_packed = _xattn(x1, w_q_mem.astype(BF16), kv, w_o_mem.astype(BF16), row(ln2_g), row(ln2_b),
                           seq, mem_len)

    idx, gates, rank, cnt = _router(x2, w_router.T, router_bias.reshape(N_EXPERTS, 1))
    counts = cnt[:, 0].astype(I32)
    padded = (counts + EXPERT_ROWS - 1) // EXPERT_ROWS * EXPERT_ROWS
    seg_end = jnp.cumsum(padded)
    seg_start = seg_end - padded
    experts = jnp.arange(N_EXPERTS, dtype=I32)
    dest = rank + jnp.sum(jnp.where(idx[..., None] == experts, seg_start, 0), axis=-1)
    total_blocks = -(-(n_tok * TOP_K) // EXPERT_ROWS) + N_EXPERTS
    n_used = (seg_end[-1:] // EXPERT_ROWS).astype(I32)

    xs = _dispatch(seg_start + counts, dest, x2_packed, total_blocks * EXPERT_ROWS)
    nonempty = counts > 0
    first_from = lax.cummin(jnp.where(nonempty, experts, N_EXPERTS), reverse=True)
    next_expert = jnp.concatenate([first_from[1:], jnp.full((1,), N_EXPERTS, I32)])
    wslot = (jnp.cumsum(nonempty.astype(I32)) - 1) & 1
    ys = _experts(seg_start // EXPERT_ROWS, padded // EXPERT_ROWS, n_used, next_expert, wslot, xs,
                  w_gate, w_up, w_down, total_blocks)
    out = _combine(dest, gates.T, x2, ys, ws_gate.astype(BF16), ws_up.astype(BF16),
                   ws_down.astype(BF16), row(ln3_g), row(ln3_b))
    return out.reshape(batch, seq, d)


def kernel(x, mem, w_in, conv_w, g_attn_out, g_conv_out, w_out, ln1_g, ln1_b, w_q_mem, w_kv_mem, w_o_mem, ln2_g, ln2_b, w_router, router_bias, w_gate, w_up, w_down, ws_gate, ws_up, ws_down, ln3_g, ln3_b):
    for l in range(DEPTH):
        x = _layer(x, mem, w_in[l], conv_w[l], g_attn_out[l], g_conv_out[l], w_out[l], ln1_g[l],
                   ln1_b[l], w_q_mem[l], w_kv_mem[l], w_o_mem[l], ln2_g[l], ln2_b[l], w_router[l],
                   router_bias[l], w_gate[l], w_up[l], w_down[l], ws_gate[l], ws_up[l], ws_down[l],
                   ln3_g[l], ln3_b[l])
    return x
```

```python
import functools

import jax
import jax.numpy as jnp
from jax import lax
from jax.experimental import pallas as pl
from jax.experimental.pallas import tpu as pltpu

F32 = jnp.float32
BF16 = jnp.bfloat16
I32 = jnp.int32
U32 = jnp.uint32

HEAD_DIM = 128
N_ATTN_HEADS = 8
ATTN_WIDTH = N_ATTN_HEADS * HEAD_DIM
CONV_WIDTH = 1024
DILATED_PATTERNS = ((128, 1), (512, 4), (2048, 16))
ATTN_BLOCK = 128
ATTN_SPAN = 16 * ATTN_BLOCK
N_MEM_HEADS = 4
N_EXPERTS = 64
TOP_K = 8
N_GROUPS = 8
GROUP_SIZE = N_EXPERTS // N_GROUPS
TOPK_GROUPS = 4
ROUTED_SCALE = 2.5
LN_EPS = 1e-5
RMS_EPS = 1e-6
DEPTH = 1
ALPHA = (2.0 * DEPTH) ** 0.25

LANES = 128
SUBLANES = 8
SUBLANE_SHIFT = SUBLANES.bit_length() - 1
VMEM_LIMIT = 56 * 1024 * 1024
DMA_QUEUES = 2
EXPERT_ROWS = 256
ZERO_GROUPS = EXPERT_ROWS // SUBLANES + 1
TM_PROJ, TN_PROJ = 1024, 512
TM_MIX = 512
TM_XATTN = 512
TM_ROUTER = 512
TM_DISPATCH = 256
TM_COMBINE = 128


def _params(*semantics):
    return pltpu.CompilerParams(dimension_semantics=semantics, vmem_limit_bytes=VMEM_LIMIT)


def _layer_norm(r, g, b):
    mu = jnp.mean(r, axis=-1, keepdims=True)
    c = r - mu
    var = jnp.mean(c * c, axis=-1, keepdims=True)
    return c * lax.rsqrt(var + LN_EPS) * g + b


def _rms_norm(v, g):
    return v * lax.rsqrt(jnp.mean(v * v, axis=-1, keepdims=True) + RMS_EPS) * g


def _dot(a, b):
    return jnp.dot(a, b, preferred_element_type=F32)


def _dot_nt(a, b):
    return lax.dot_general(a, b, (((1,), (1,)), ((), ())), preferred_element_type=F32)


def _silu(v):
    return v / (1.0 + jnp.exp(-v))


def _matmul_kernel(a_ref, w_ref, o_ref, a_bf):
    @pl.when(pl.program_id(1) == 0)
    def _():
        a_bf[...] = a_ref[...].astype(BF16)

    o_ref[...] = _dot(a_bf[...], w_ref[...]).astype(o_ref.dtype)


def _matmul(a, w, out_dtype, tm, tn):
    m, k = a.shape
    n = w.shape[1]
    return pl.pallas_call(
        _matmul_kernel,
        grid=(m // tm, n // tn),
        in_specs=[pl.BlockSpec((tm, k), lambda i, j: (i, 0)),
                  pl.BlockSpec((k, tn), lambda i, j: (0, j))],
        out_specs=pl.BlockSpec((tm, tn), lambda i, j: (i, j)),
        out_shape=jax.ShapeDtypeStruct((m, n), out_dtype),
        scratch_shapes=[pltpu.VMEM((tm, k), BF16)],
        compiler_params=_params("parallel", "arbitrary"),
        name="matmul",
    )(a, w)


def _attn_kernel(q_ref, kp_ref, kc_ref, vp_ref, vc_ref, o_ref, kk, vv, o_acc, l_acc):
    j = pl.program_id(1)
    blk, span = ATTN_BLOCK, ATTN_SPAN
    kk[0:span, :] = kp_ref[...]
    kk[span:, :] = kc_ref[...]
    vv[0:span, :] = vp_ref[...]
    vv[span:, :] = vc_ref[...]

    row = lax.broadcasted_iota(I32, (blk, 2 * blk), 0)
    col = lax.broadcasted_iota(I32, (blk, 2 * blk), 1)
    dist = row + blk - col
    in_window = (dist >= 0) & (dist <= blk)
    own_block = col >= blk
    scale = HEAD_DIM ** -0.5

    order = sorted(range(len(DILATED_PATTERNS)), key=lambda i: -DILATED_PATTERNS[i][1])
    assert DILATED_PATTERNS[order[-1]][1] == 1
    acc_slot = {bi: n for n, bi in enumerate(order[:-1])}
    sub_blocks = span // blk
    for bi in order:
        dil = DILATED_PATTERNS[bi][1]
        shift = dil.bit_length() - 1

        def body(t, carry, bi=bi, dil=dil, shift=shift):
            r = t & (dil - 1)
            n = t >> shift
            base = n * (blk * dil) + r
            if dil == 1:
                base = pl.multiple_of(base, blk)
                q_rows, kv_rows = pl.ds(base, blk), pl.ds(span + base - blk, 2 * blk)
            else:
                q_rows = pl.ds(base, blk, stride=dil)
                kv_rows = pl.ds(span + base - blk * dil, 2 * blk, stride=dil)
            q = q_ref[q_rows, :].astype(BF16)
            k = kk[kv_rows, :].astype(BF16)
            v = vv[kv_rows, :].astype(BF16)
            s = _dot_nt(q, k) * scale
            valid = in_window & (own_block | (j > 0) | (n > 0))
            s = jnp.where(valid, s, -jnp.inf)
            m = jnp.max(s, axis=-1, keepdims=True)
            p = jnp.exp(s - m)
            l = jnp.sum(p, axis=-1, keepdims=True)
            o = _dot((p / l).astype(BF16), v)
            lse = jnp.broadcast_to(m + jnp.log(l), (blk, LANES))
            if dil != 1:
                o_acc[acc_slot[bi], q_rows, :] = o
                l_acc[acc_slot[bi], q_rows, :] = lse
                return carry
            others = [i for i in order if i != bi]
            lses = [lse] + [l_acc[acc_slot[i], q_rows, :] for i in others]
            outs = [o] + [o_acc[acc_slot[i], q_rows, :] for i in others]
            top = jnp.maximum(jnp.maximum(lses[0], lses[1]), lses[2])
            es = [jnp.exp(branch_lse - top) for branch_lse in lses]
            den = es[0] + es[1] + es[2]
            o_ref[q_rows, :] = ((es[0] / den) * outs[0] + (es[1] / den) * outs[1] + (es[2] / den) * outs[2])
            return carry

        lax.fori_loop(0, sub_blocks, body, 0, unroll=True)


def _dilated_attention(proj, batch, seq):
    n_tok, _ = proj.shape
    span = ATTN_SPAN
    spans = seq // span
    heads = N_ATTN_HEADS

    def spec(part, prev):
        def index(b, j, h):
            return (b * spans + (jnp.maximum(j - 1, 0) if prev else j), part * heads + h)
        return pl.BlockSpec((span, HEAD_DIM), index)

    return pl.pallas_call(
        _attn_kernel,
        grid=(batch, spans, heads),
        in_specs=[spec(0, False), spec(1, True), spec(1, False), spec(2, True), spec(2, False)],
        out_specs=pl.BlockSpec((span, HEAD_DIM), lambda b, j, h: (b * spans + j, h)),
        out_shape=jax.ShapeDtypeStruct((n_tok, ATTN_WIDTH), F32),
        scratch_shapes=[pltpu.VMEM((2 * span, HEAD_DIM), F32), pltpu.VMEM((2 * span, HEAD_DIM), F32),
                        pltpu.VMEM((len(DILATED_PATTERNS) - 1, span, HEAD_DIM), F32),
                        pltpu.VMEM((len(DILATED_PATTERNS) - 1, span, HEAD_DIM), F32)],
        compiler_params=_params("parallel", "parallel", "parallel"),
        name="dilated_attn",
    )(proj, proj, proj, proj, proj)


def _mix_out_kernel(a_ref, b_ref, c_ref, h_ref, cp_ref, hp_ref, x_ref,
                    w_ref, cw_ref, ga_ref, gc_ref, lg_ref, lb_ref, o_ref, *, tiles_per_seq):
    i = pl.program_id(0)
    z = c_ref[...] * h_ref[...]
    zp = jnp.where(i % tiles_per_seq == 0, 0.0, cp_ref[...] * hp_ref[...])
    zz = jnp.concatenate([zp, z], axis=0)
    z1 = pltpu.roll(zz, 1, axis=0)[SUBLANES:]
    z2 = pltpu.roll(zz, 2, axis=0)[SUBLANES:]
    conv = b_ref[...] * (cw_ref[0:1, :] * z2 + cw_ref[1:2, :] * z1 + cw_ref[2:3, :] * z)

    mixed = jnp.concatenate([_rms_norm(a_ref[...], ga_ref[...]), _rms_norm(conv, gc_ref[...])], axis=1)
    y = _dot(mixed.astype(BF16), w_ref[...])
    o_ref[...] = _layer_norm(ALPHA * x_ref[...] + y, lg_ref[...], lb_ref[...])


def _mix_out(attn, proj, x, w_out, conv_w, g_attn, g_conv, ln_g, ln_b, seq):
    n_tok, d = x.shape
    tm = TM_MIX
    rows8 = tm // SUBLANES
    row = lambda i: (i, 0)
    const = lambda i: (0, 0)
    prev = lambda part: (lambda i: (jnp.maximum(i * rows8 - 1, 0), part))
    return pl.pallas_call(
        functools.partial(_mix_out_kernel, tiles_per_seq=seq // tm),
        grid=(n_tok // tm,),
        in_specs=[
            pl.BlockSpec((tm, ATTN_WIDTH), row),
            pl.BlockSpec((tm, CONV_WIDTH), lambda i: (i, 3)),
            pl.BlockSpec((tm, CONV_WIDTH), lambda i: (i, 4)),
            pl.BlockSpec((tm, CONV_WIDTH), lambda i: (i, 5)),
            pl.BlockSpec((SUBLANES, CONV_WIDTH), prev(4)),
            pl.BlockSpec((SUBLANES, CONV_WIDTH), prev(5)),
            pl.BlockSpec((tm, d), row),
            pl.BlockSpec(w_out.shape, const, pipeline_mode=pl.Buffered(1)),
            pl.BlockSpec(conv_w.shape, const),
            pl.BlockSpec(g_attn.shape, const),
            pl.BlockSpec(g_conv.shape, const),
            pl.BlockSpec(ln_g.shape, const),
            pl.BlockSpec(ln_b.shape, const)],
        out_specs=pl.BlockSpec((tm, d), row),
        out_shape=jax.ShapeDtypeStruct((n_tok, d), F32),
        compiler_params=_params("parallel"),
        name="mix_out_ln1",
    )(attn, proj, proj, proj, proj, proj, x, w_out, conv_w, g_attn, g_conv, ln_g, ln_b)


def _pack_halves(v):
    c = v.shape[1] // 2
    as_bits = lambda part: lax.bitcast_convert_type(part.astype(BF16).astype(F32), U32)
    return (as_bits(v[:, :c]) >> 16) | (as_bits(v[:, c:]) & jnp.uint32(0xFFFF0000))


def _unpack_halves(p):
    lo = lax.bitcast_convert_type(p << 16, F32).astype(BF16)
    hi = lax.bitcast_convert_type(p & jnp.uint32(0xFFFF0000), F32).astype(BF16)
    return lo, hi


def _xattn_kernel(x_ref, wq_ref, k_ref, v_ref, wo_ref, lg_ref, lb_ref, o_ref, op_ref):
    x = x_ref[...]
    d = x.shape[1]
    hd = d // N_MEM_HEADS
    q = _dot(x.astype(BF16), wq_ref[...]).astype(BF16)
    scale = hd ** -0.5
    outs = []
    for h in range(N_MEM_HEADS):
        sl = slice(h * hd, (h + 1) * hd)
        s = _dot_nt(q[:, sl], k_ref[:, sl]) * scale
        m = jnp.max(s, axis=-1, keepdims=True)
        p = jnp.exp(s - m)
        p = p / jnp.sum(p, axis=-1, keepdims=True)
        outs.append(_dot(p.astype(BF16), v_ref[:, sl]).astype(BF16))
    y = _dot(jnp.concatenate(outs, axis=1), wo_ref[...])
    out = _layer_norm(ALPHA * x + y, lg_ref[...], lb_ref[...])
    o_ref[...] = out
    op_ref[...] = _pack_halves(out)


def _xattn(x, w_q, kv, w_o, ln_g, ln_b, seq, mem_len):
    n_tok, d = x.shape
    tm = TM_XATTN
    row = lambda i: (i, 0)
    const = lambda i: (0, 0)
    tiles_per_seq = seq // tm
    return pl.pallas_call(
        _xattn_kernel,
        grid=(n_tok // tm,),
        in_specs=[pl.BlockSpec((tm, d), row),
                  pl.BlockSpec(w_q.shape, const, pipeline_mode=pl.Buffered(1)),
                  pl.BlockSpec((mem_len, d), lambda i: (i // tiles_per_seq, 0)),
                  pl.BlockSpec((mem_len, d), lambda i: (i // tiles_per_seq, 1)),
                  pl.BlockSpec(w_o.shape, const, pipeline_mode=pl.Buffered(1)),
                  pl.BlockSpec(ln_g.shape, const),
                  pl.BlockSpec(ln_b.shape, const)],
        out_specs=[pl.BlockSpec((tm, d), row), pl.BlockSpec((tm, d // 2), row)],
        out_shape=[jax.ShapeDtypeStruct((n_tok, d), F32), jax.ShapeDtypeStruct((n_tok, d // 2), U32)],
        compiler_params=_params("parallel"),
        name="xattn_ln2",
    )(x, w_q, kv, kv, w_o, ln_g, ln_b)


def _first_argmax(vals, index, sentinel):
    m = jnp.max(vals, axis=0, keepdims=True)
    i = jnp.min(jnp.where(vals == m, index, sentinel), axis=0, keepdims=True)
    return m, i


def _router_kernel(x_ref, wt_ref, bias_ref, idx_ref, gate_ref, rank_ref, cnt_ref):
    step = pl.program_id(0)
    tm = x_ref.shape[0]

    @pl.when(step == 0)
    def _():
        cnt_ref[...] = jnp.zeros_like(cnt_ref)

    x = x_ref[...]
    xh = x.astype(BF16)
    xl = (x - xh.astype(F32)).astype(BF16)
    w = wt_ref[...]
    wh = w.astype(BF16)
    wl = (w - wh.astype(F32)).astype(BF16)
    both = _dot_nt(jnp.concatenate([wh, wl], axis=0), xh)
    logits = both[:N_EXPERTS] + both[N_EXPERTS:] + _dot_nt(wh, xl)
    scores = 1.0 / (1.0 + jnp.exp(-logits))
    choice = scores + bias_ref[...]

    sub = lax.broadcasted_iota(I32, (GROUP_SIZE, tm), 0)
    group_scores = []
    for g in range(N_GROUPS):
        c = choice[g * GROUP_SIZE:(g + 1) * GROUP_SIZE, :]
        m1, i1 = _first_argmax(c, sub, GROUP_SIZE)
        m2 = jnp.max(jnp.where(sub == i1, -jnp.inf, c), axis=0, keepdims=True)
        group_scores.append(m1 + m2)
    gs = jnp.concatenate(group_scores, axis=0)
    gidx = lax.broadcasted_iota(I32, (N_GROUPS, tm), 0)
    gsel = jnp.zeros((N_GROUPS, tm), F32)
    for _ in range(TOPK_GROUPS):
        _, gi = _first_argmax(gs, gidx, N_GROUPS)
        hit = gidx == gi
        gsel = jnp.where(hit, 1.0, gsel)
        gs = jnp.where(hit, -jnp.inf, gs)
    masked = jnp.concatenate(
        [jnp.where(gsel[g:g + 1, :] > 0.0, choice[g * GROUP_SIZE:(g + 1) * GROUP_SIZE, :], -jnp.inf)
         for g in range(N_GROUPS)], axis=0)

    eidx = lax.broadcasted_iota(I32, (N_EXPERTS, tm), 0)
    hits, idxs, ws = [], [], []
    for _ in range(TOP_K):
        _, ei = _first_argmax(masked, eidx, N_EXPERTS)
        hit = eidx == ei
        hits.append(hit)
        idxs.append(ei)
        ws.append(jnp.sum(jnp.where(hit, scores, 0.0), axis=0, keepdims=True))
        masked = jnp.where(hit, -jnp.inf, masked)
    wsum = ws[0]
    for w in ws[1:]:
        wsum = wsum + w
    gate_ref[...] = jnp.concatenate([w / wsum * ROUTED_SCALE for w in ws], axis=0)
    idx_ref[...] = jnp.concatenate(idxs, axis=0)

    sel = hits[0]
    for hit in hits[1:]:
        sel = sel | hit
    self32 = jnp.where(sel, 1.0, 0.0)
    t_row = lax.broadcasted_iota(I32, (tm, tm), 0)
    t_col = lax.broadcasted_iota(I32, (tm, tm), 1)
    tri = jnp.where(t_row < t_col, 1.0, 0.0).astype(BF16)
    before = _dot(self32.astype(BF16), tri) + cnt_ref[:, 0:1]
    rank_ref[...] = jnp.concatenate(
        [jnp.sum(jnp.where(hit, before, 0.0), axis=0, keepdims=True) for hit in hits],
        axis=0).astype(I32)
    cnt_ref[...] += jnp.sum(self32, axis=1, keepdims=True)


def _router(x, w_router_t, bias_col):
    n_tok, d = x.shape
    tm = TM_ROUTER
    tok = lambda i: (0, i)
    return pl.pallas_call(
        _router_kernel,
        grid=(n_tok // tm,),
        in_specs=[pl.BlockSpec((tm, d), lambda i: (i, 0)),
                  pl.BlockSpec(w_router_t.shape, lambda i: (0, 0)),
                  pl.BlockSpec(bias_col.shape, lambda i: (0, 0))],
        out_specs=[pl.BlockSpec((TOP_K, tm), tok), pl.BlockSpec((TOP_K, tm), tok),
                   pl.BlockSpec((TOP_K, tm), tok),
                   pl.BlockSpec((N_EXPERTS, LANES), lambda i: (0, 0))],
        out_shape=[jax.ShapeDtypeStruct((TOP_K, n_tok), I32),
                   jax.ShapeDtypeStruct((TOP_K, n_tok), F32),
                   jax.ShapeDtypeStruct((TOP_K, n_tok), I32),
                   jax.ShapeDtypeStruct((N_EXPERTS, LANES), F32)],
        compiler_params=_params("arbitrary"),
        name="router",
    )(x, w_router_t, bias_col)


def _tile_view(x):
    r, d = x.shape
    return (x.reshape(r // SUBLANES, SUBLANES, d // LANES, LANES).transpose(0, 2, 1, 3)
            .reshape(r // SUBLANES, d // LANES, SUBLANES, 1, LANES))


def _row_view(v):
    g, c, s, _, l = v.shape
    return v.reshape(g, c, s, l).transpose(0, 2, 1, 3).reshape(g * s, c * l)


def _tile_row(view_ref, row):
    return view_ref.at[row >> SUBLANE_SHIFT, :, row & (SUBLANES - 1)]


def _dispatch_kernel(pad_ref, dest_ref, x_ref, xs_ref, zero_ref, sem):
    step = pl.program_id(0)
    groups = x_ref.shape[0]

    @pl.when(step == 0)
    def _():
        zero_ref[...] = jnp.zeros_like(zero_ref)

        def fill(e):
            return pltpu.make_async_copy(zero_ref, xs_ref.at[pl.ds(pad_ref[e] >> SUBLANE_SHIFT, ZERO_GROUPS)], sem)

        def start(e, c):
            fill(e).start()
            return c

        def wait(e, c):
            fill(e).wait()
            return c

        lax.fori_loop(0, N_EXPERTS, start, 0)
        lax.fori_loop(0, N_EXPERTS, wait, 0)

    def copies(g):
        return [pltpu.make_async_copy(x_ref.at[g, :, s], _tile_row(xs_ref, dest_ref[k, g * SUBLANES + s]), sem)
                for s in range(SUBLANES) for k in range(TOP_K)]

    def start_rows(g, c):
        for n, cp in enumerate(copies(g)):
            cp.start(priority=n % DMA_QUEUES)
        return c

    def wait_rows(g, c):
        for cp in copies(g):
            cp.wait()
        return c

    lax.fori_loop(0, groups, start_rows, 0)
    lax.fori_loop(0, groups, wait_rows, 0)


def _dispatch(pad_start, dest, x, n_rows):
    n_tok, w = x.shape
    tm = TM_DISPATCH
    chunks = w // LANES
    block = (tm // SUBLANES, chunks, SUBLANES, 1, LANES)
    total_groups = n_rows // SUBLANES + ZERO_GROUPS
    out = pl.pallas_call(
        _dispatch_kernel,
        grid_spec=pltpu.PrefetchScalarGridSpec(
            num_scalar_prefetch=1,
            grid=(n_tok // tm,),
            in_specs=[pl.BlockSpec((TOP_K, tm), lambda i, pad: (0, i), memory_space=pltpu.SMEM),
                      pl.BlockSpec(block, lambda i, pad: (i, 0, 0, 0, 0))],
            out_specs=pl.BlockSpec(memory_space=pl.ANY),
            scratch_shapes=[pltpu.VMEM((ZERO_GROUPS, chunks, SUBLANES, 1, LANES), x.dtype),
                            pltpu.SemaphoreType.DMA(())]),
        out_shape=jax.ShapeDtypeStruct((total_groups, chunks, SUBLANES, 1, LANES), x.dtype),
        compiler_params=_params("arbitrary"),
        name="dispatch",
    )(pad_start, dest, _tile_view(x))
    return _row_view(out)


def _expert_kernel(first_ref, nblk_ref, nused_ref, next_ref, wslot_ref, wg_hbm, wu_hbm, wd_hbm, xs_hbm, ys_hbm,
                   xbuf, ybuf, wg_f, wu_f, wd_f, wg_s, wu_s, wd_s, xsem, ysem, wsem):
    e = pl.program_id(0)
    n_used = nused_ref[0]

    def x_copy(blk, slot):
        rows = pl.ds(pl.multiple_of(blk * EXPERT_ROWS, EXPERT_ROWS), EXPERT_ROWS)
        return pltpu.make_async_copy(xs_hbm.at[rows, :], xbuf.at[slot], xsem.at[slot])

    def y_copy(blk, slot):
        rows = pl.ds(pl.multiple_of(blk * EXPERT_ROWS, EXPERT_ROWS), EXPERT_ROWS)
        return pltpu.make_async_copy(ybuf.at[slot], ys_hbm.at[rows, :], ysem.at[slot])

    def w_copies(expert, slot):
        return [pltpu.make_async_copy(src.at[expert], dst.at[slot], wsem.at[slot])
                for src, dst in ((wg_hbm, wg_f), (wu_hbm, wu_f), (wd_hbm, wd_f))]

    @pl.when(e == 0)
    def _():
        x_copy(0, 0).start()

    @pl.when(nblk_ref[e] > 0)
    def _():
        wslot = wslot_ref[e]

        @pl.when(first_ref[e] == 0)
        def _():
            for cp in w_copies(e, wslot):
                cp.start()

        for cp in w_copies(e, wslot):
            cp.wait()
        wg_s[...] = wg_f[wslot].astype(BF16)
        wu_s[...] = wu_f[wslot].astype(BF16)
        wd_s[...] = wd_f[wslot].astype(BF16)
        e_next = next_ref[e]

        @pl.when(e_next < N_EXPERTS)
        def _():
            for cp in w_copies(e_next, 1 - wslot):
                cp.start(priority=DMA_QUEUES - 1)

        def block(i, carry):
            blk = first_ref[e] + i
            slot = blk & 1
            x_copy(blk, slot).wait()

            @pl.when(blk + 1 < n_used)
            def _():
                x_copy(blk + 1, 1 - slot).start()

            @pl.when(blk >= 2)
            def _():
                y_copy(blk - 2, slot).wait()

            lo, hi = _unpack_halves(xbuf[slot])
            half = lo.shape[1]
            gate = _dot(lo, wg_s[:half, :]) + _dot(hi, wg_s[half:, :])
            up = _dot(lo, wu_s[:half, :]) + _dot(hi, wu_s[half:, :])
            hmid = (_silu(gate) * up).astype(BF16)
            ybuf[slot] = _dot(hmid, wd_s[...])
            y_copy(blk, slot).start()
            return carry

        lax.fori_loop(0, nblk_ref[e], block, 0)

    @pl.when(e == pl.num_programs(0) - 1)
    def _():
        @pl.when(n_used >= 2)
        def _():
            y_copy(n_used - 2, (n_used - 2) & 1).wait()

        y_copy(n_used - 1, (n_used - 1) & 1).wait()


def _experts(first_block, n_blocks_of, n_used, next_expert, wslot, xs, w_gate, w_up, w_down, total_blocks):
    n_exp, d, ff = w_gate.shape
    hbm = pl.BlockSpec(memory_space=pl.ANY)
    return pl.pallas_call(
        _expert_kernel,
        grid_spec=pltpu.PrefetchScalarGridSpec(
            num_scalar_prefetch=5,
            grid=(n_exp,),
            in_specs=[hbm, hbm, hbm, hbm],
            out_specs=hbm,
            scratch_shapes=[pltpu.VMEM((2, EXPERT_ROWS, d // 2), U32), pltpu.VMEM((2, EXPERT_ROWS, d), F32),
                            pltpu.VMEM((2, d, ff), F32), pltpu.VMEM((2, d, ff), F32),
                            pltpu.VMEM((2, ff, d), F32),
                            pltpu.VMEM((d, ff), BF16), pltpu.VMEM((d, ff), BF16),
                            pltpu.VMEM((ff, d), BF16),
                            pltpu.SemaphoreType.DMA((2,)), pltpu.SemaphoreType.DMA((2,)),
                            pltpu.SemaphoreType.DMA((2,))]),
        out_shape=jax.ShapeDtypeStruct((total_blocks * EXPERT_ROWS, d), F32),
        compiler_params=_params("arbitrary"),
        name="experts",
    )(first_block, n_blocks_of, n_used, next_expert, wslot, w_gate, w_up, w_down, xs)


def _combine_kernel(dest_ref, gate_ref, x_ref, ys_ref, wsg_ref, wsu_ref, wsd_ref, lg_ref, lb_ref,
                    o_ref, buf, sem):
    tm, d = x_ref.shape
    groups, chunks = tm // SUBLANES, d // LANES
    dma_views = [buf.at[k].reshape(groups, chunks, SUBLANES, 1, LANES) for k in range(TOP_K)]
    load_views = [buf.at[k].reshape(groups, chunks, SUBLANES, LANES) for k in range(TOP_K)]

    def copies(g):
        return [pltpu.make_async_copy(_tile_row(ys_ref, dest_ref[k, g * SUBLANES + s]),
                                      dma_views[k].at[g, :, s], sem)
                for s in range(SUBLANES) for k in range(TOP_K)]

    def start_rows(g, c):
        for n, cp in enumerate(copies(g)):
            cp.start(priority=n % DMA_QUEUES)
        return c

    def wait_rows(g, c):
        for cp in copies(g):
            cp.wait()
        return c

    lax.fori_loop(0, groups, start_rows, 0)

    x = x_ref[...]
    xb = x.astype(BF16)
    hmid = (_silu(_dot(xb, wsg_ref[...])) * _dot(xb, wsu_ref[...])).astype(BF16)
    acc = ALPHA * x + _dot(hmid, wsd_ref[...])

    lax.fori_loop(0, groups, wait_rows, 0)
    gates = gate_ref[...]

    def rows_of(k):
        return jnp.concatenate([load_views[k][:, c].reshape(tm, LANES) for c in range(chunks)], axis=1)

    routed = rows_of(0) * gates[:, 0:1]
    for k in range(1, TOP_K):
        routed += rows_of(k) * gates[:, k:k + 1]
    o_ref[...] = _layer_norm(acc + routed, lg_ref[...], lb_ref[...])


def _combine(dest, gates_t, x, ys, ws_gate, ws_up, ws_down, ln_g, ln_b):
    n_tok, d = x.shape
    tm = TM_COMBINE
    row = lambda i: (i, 0)
    const = lambda i: (0, 0)
    return pl.pallas_call(
        _combine_kernel,
        grid=(n_tok // tm,),
        in_specs=[pl.BlockSpec((TOP_K, tm), lambda i: (0, i), memory_space=pltpu.SMEM),
                  pl.BlockSpec((tm, TOP_K), row),
                  pl.BlockSpec((tm, d), row),
                  pl.BlockSpec(memory_space=pl.ANY),
                  pl.BlockSpec(ws_gate.shape, const),
                  pl.BlockSpec(ws_up.shape, const),
                  pl.BlockSpec(ws_down.shape, const),
                  pl.BlockSpec(ln_g.shape, const),
                  pl.BlockSpec(ln_b.shape, const)],
        out_specs=pl.BlockSpec((tm, d), row),
        out_shape=jax.ShapeDtypeStruct((n_tok, d), F32),
        scratch_shapes=[pltpu.VMEM((TOP_K, tm * (d // LANES), LANES), F32), pltpu.SemaphoreType.DMA(())],
        compiler_params=_params("arbitrary"),
        name="combine_shared_ln3",
    )(dest, gates_t, x, _tile_view(ys), ws_gate, ws_up, ws_down, ln_g, ln_b)


def _layer(x, mem, w_in, conv_w, g_attn_out, g_conv_out, w_out, ln1_g, ln1_b, w_q_mem, w_kv_mem,
           w_o_mem, ln2_g, ln2_b, w_router, router_bias, w_gate, w_up, w_down, ws_gate, ws_up,
           ws_down, ln3_g, ln3_b):
    batch, seq, d = x.shape
    mem_len = mem.shape[1]
    n_tok = batch * seq
    xf = x.reshape(n_tok, d)
    row = lambda v: v.reshape(1, -1)

    proj = _matmul(xf, w_in.astype(BF16), F32, TM_PROJ, TN_PROJ)
    attn = _dilated_attention(proj, batch, seq)
    x1 = _mix_out(attn, proj, xf, w_out.astype(BF16), conv_w, row(g_attn_out), row(g_conv_out),
                  row(ln1_g), row(ln1_b), seq)

    kv = _matmul(mem.reshape(batch * mem_len, d), w_kv_mem.astype(BF16), BF16,
                 mem_len, 1024)
    x2, x2_packed = _xattn(x1, w_q_mem.astype(BF16), kv, w_o_mem.astype(BF16), row(ln2_g), row(ln2_b),
                           seq, mem_len)

    idx, gates, rank, cnt = _router(x2, w_router.T, router_bias.reshape(N_EXPERTS, 1))
    counts = cnt[:, 0].astype(I32)
    padded = (counts + EXPERT_ROWS - 1) // EXPERT_ROWS * EXPERT_ROWS
    seg_end = jnp.cumsum(padded)
    seg_start = seg_end - padded
    experts = jnp.arange(N_EXPERTS, dtype=I32)
    dest = rank + jnp.sum(jnp.where(idx[..., None] == experts, seg_start, 0), axis=-1)
    total_blocks = -(-(n_tok * TOP_K) // EXPERT_ROWS) + N_EXPERTS
    n_used = (seg_end[-1:] // EXPERT_ROWS).astype(I32)

    xs = _dispatch(seg_start + counts, dest, x2_packed, total_blocks * EXPERT_ROWS)
    nonempty = counts > 0
    first_from = lax.cummin(jnp.where(nonempty, experts, N_EXPERTS), reverse=True)
    next_expert = jnp.concatenate([first_from[1:], jnp.full((1,), N_EXPERTS, I32)])
    wslot = (jnp.cumsum(nonempty.astype(I32)) - 1) & 1
    ys = _experts(seg_start // EXPERT_ROWS, padded // EXPERT_ROWS, n_used, next_expert, wslot, xs,
                  w_gate, w_up, w_down, total_blocks)
    out = _combine(dest, gates.T, x2, ys, ws_gate.astype(BF16), ws_up.astype(BF16),
                   ws_down.astype(BF16), row(ln3_g), row(ln3_b))
    return out.reshape(batch, seq, d)


def kernel(x, mem, w_in, conv_w, g_attn_out, g_conv_out, w_out, ln1_g, ln1_b, w_q_mem, w_kv_mem, w_o_mem, ln2_g, ln2_b, w_router, router_bias, w_gate, w_up, w_down, ws_gate, ws_up, ws_down, ln3_g, ln3_b):
    for l in range(DEPTH):
        x = _layer(x, mem, w_in[l], conv_w[l], g_attn_out[l], g_conv_out[l], w_out[l], ln1_g[l],
                   ln1_b[l], w_q_mem[l], w_kv_mem[l], w_o_mem[l], ln2_g[l], ln2_b[l], w_router[l],
                   router_bias[l], w_gate[l], w_up[l], w_down[l], ws_gate[l], ws_up[l], ws_down[l],
                   ln3_g[l], ln3_b[l])
    return x
```

```python
import functools

import jax
import jax.numpy as jnp
from jax import lax
from jax.experimental import pallas as pl
from jax.experimental.pallas import tpu as pltpu

F32 = jnp.float32
BF16 = jnp.bfloat16
I32 = jnp.int32
U32 = jnp.uint32

HEAD_DIM = 128
N_ATTN_HEADS = 8
ATTN_WIDTH = N_ATTN_HEADS * HEAD_DIM
CONV_WIDTH = 1024
DILATED_PATTERNS = ((128, 1), (512, 4), (2048, 16))
ATTN_BLOCK = 128
ATTN_SPAN = 16 * ATTN_BLOCK
N_MEM_HEADS = 4
N_EXPERTS = 64
TOP_K = 8
N_GROUPS = 8
GROUP_SIZE = N_EXPERTS // N_GROUPS
TOPK_GROUPS = 4
ROUTED_SCALE = 2.5
LN_EPS = 1e-5
RMS_EPS = 1e-6
DEPTH = 1
ALPHA = (2.0 * DEPTH) ** 0.25

LANES = 128
SUBLANES = 8
SUBLANE_SHIFT = SUBLANES.bit_length() - 1
VMEM_LIMIT = 56 * 1024 * 1024
DMA_QUEUES = 2
EXPERT_ROWS = 256
ZERO_GROUPS = EXPERT_ROWS // SUBLANES + 1
TM_PROJ, TN_PROJ = 1024, 512
TM_MIX = 512
TM_XATTN = 512
TM_ROUTER = 512
TM_DISPATCH = 512
TM_COMBINE = 256


def _params(*semantics):
    return pltpu.CompilerParams(dimension_semantics=semantics, vmem_limit_bytes=VMEM_LIMIT)


def _layer_norm(r, g, b):
    mu = jnp.mean(r, axis=-1, keepdims=True)
    c = r - mu
    var = jnp.mean(c * c, axis=-1, keepdims=True)
    return c * lax.rsqrt(var + LN_EPS) * g + b


def _rms_norm(v, g):
    return v * lax.rsqrt(jnp.mean(v * v, axis=-1, keepdims=True) + RMS_EPS) * g


def _dot(a, b):
    return jnp.dot(a, b, preferred_element_type=F32)


def _dot_nt(a, b):
    return lax.dot_general(a, b, (((1,), (1,)), ((), ())), preferred_element_type=F32)


def _silu(v):
    return v / (1.0 + jnp.exp(-v))


def _matmul_kernel(a_ref, w_ref, o_ref, a_bf):
    @pl.when(pl.program_id(1) == 0)
    def _():
        a_bf[...] = a_ref[...].astype(BF16)

    o_ref[...] = _dot(a_bf[...], w_ref[...]).astype(o_ref.dtype)


def _matmul(a, w, out_dtype, tm, tn):
    m, k = a.shape
    n = w.shape[1]
    return pl.pallas_call(
        _matmul_kernel,
        grid=(m // tm, n // tn),
        in_specs=[pl.BlockSpec((tm, k), lambda i, j: (i, 0)),
                  pl.BlockSpec((k, tn), lambda i, j: (0, j))],
        out_specs=pl.BlockSpec((tm, tn), lambda i, j: (i, j)),
        out_shape=jax.ShapeDtypeStruct((m, n), out_dtype),
        scratch_shapes=[pltpu.VMEM((tm, k), BF16)],
        compiler_params=_params("parallel", "arbitrary"),
        name="matmul",
    )(a, w)


def _attn_kernel(q_ref, kp_ref, kc_ref, vp_ref, vc_ref, o_ref, kk, vv, o_acc, l_acc):
    j = pl.program_id(1)
    blk, span = ATTN_BLOCK, ATTN_SPAN
    kk[0:span, :] = kp_ref[...]
    kk[span:, :] = kc_ref[...]
    vv[0:span, :] = vp_ref[...]
    vv[span:, :] = vc_ref[...]

    row = lax.broadcasted_iota(I32, (blk, 2 * blk), 0)
    col = lax.broadcasted_iota(I32, (blk, 2 * blk), 1)
    dist = row + blk - col
    in_window = (dist >= 0) & (dist <= blk)
    own_block = col >= blk
    scale = HEAD_DIM ** -0.5

    order = sorted(range(len(DILATED_PATTERNS)), key=lambda i: -DILATED_PATTERNS[i][1])
    assert DILATED_PATTERNS[order[-1]][1] == 1
    acc_slot = {bi: n for n, bi in enumerate(order[:-1])}
    sub_blocks = span // blk
    for bi in order:
        dil = DILATED_PATTERNS[bi][1]
        shift = dil.bit_length() - 1

        def body(t, carry, bi=bi, dil=dil, shift=shift):
            r = t & (dil - 1)
            n = t >> shift
            base = n * (blk * dil) + r
            if dil == 1:
                base = pl.multiple_of(base, blk)
                q_rows, kv_rows = pl.ds(base, blk), pl.ds(span + base - blk, 2 * blk)
            else:
                q_rows = pl.ds(base, blk, stride=dil)
                kv_rows = pl.ds(span + base - blk * dil, 2 * blk, stride=dil)
            q = q_ref[q_rows, :].astype(BF16)
            k = kk[kv_rows, :].astype(BF16)
            v = vv[kv_rows, :].astype(BF16)
            s = _dot_nt(q, k) * scale
            valid = in_window & (own_block | (j > 0) | (n > 0))
            s = jnp.where(valid, s, -jnp.inf)
            m = jnp.max(s, axis=-1, keepdims=True)
            p = jnp.exp(s - m)
            l = jnp.sum(p, axis=-1, keepdims=True)
            o = _dot((p / l).astype(BF16), v)
            lse = jnp.broadcast_to(m + jnp.log(l), (blk, LANES))
            if dil != 1:
                o_acc[acc_slot[bi], q_rows, :] = o
                l_acc[acc_slot[bi], q_rows, :] = lse
                return carry
            others = [i for i in order if i != bi]
            lses = [lse] + [l_acc[acc_slot[i], q_rows, :] for i in others]
            outs = [o] + [o_acc[acc_slot[i], q_rows, :] for i in others]
            top = jnp.maximum(jnp.maximum(lses[0], lses[1]), lses[2])
            es = [jnp.exp(branch_lse - top) for branch_lse in lses]
            den = es[0] + es[1] + es[2]
            o_ref[q_rows, :] = ((es[0] / den) * outs[0] + (es[1] / den) * outs[1] + (es[2] / den) * outs[2])
            return carry

        lax.fori_loop(0, sub_blocks, body, 0, unroll=True)


def _dilated_attention(proj, batch, seq):
    n_tok, _ = proj.shape
    span = ATTN_SPAN
    spans = seq // span
    heads = N_ATTN_HEADS

    def spec(part, prev):
        def index(b, j, h):
            return (b * spans + (jnp.maximum(j - 1, 0) if prev else j), part * heads + h)
        return pl.BlockSpec((span, HEAD_DIM), index)

    return pl.pallas_call(
        _attn_kernel,
        grid=(batch, spans, heads),
        in_specs=[spec(0, False), spec(1, True), spec(1, False), spec(2, True), spec(2, False)],
        out_specs=pl.BlockSpec((span, HEAD_DIM), lambda b, j, h: (b * spans + j, h)),
        out_shape=jax.ShapeDtypeStruct((n_tok, ATTN_WIDTH), F32),
        scratch_shapes=[pltpu.VMEM((2 * span, HEAD_DIM), F32), pltpu.VMEM((2 * span, HEAD_DIM), F32),
                        pltpu.VMEM((len(DILATED_PATTERNS) - 1, span, HEAD_DIM), F32),
                        pltpu.VMEM((len(DILATED_PATTERNS) - 1, span, HEAD_DIM), F32)],
        compiler_params=_params("parallel", "parallel", "parallel"),
        name="dilated_attn",
    )(proj, proj, proj, proj, proj)


def _mix_out_kernel(a_ref, b_ref, c_ref, h_ref, cp_ref, hp_ref, x_ref,
                    w_ref, cw_ref, ga_ref, gc_ref, lg_ref, lb_ref, o_ref, *, tiles_per_seq):
    i = pl.program_id(0)
    z = c_ref[...] * h_ref[...]
    zp = jnp.where(i % tiles_per_seq == 0, 0.0, cp_ref[...] * hp_ref[...])
    zz = jnp.concatenate([zp, z], axis=0)
    z1 = pltpu.roll(zz, 1, axis=0)[SUBLANES:]
    z2 = pltpu.roll(zz, 2, axis=0)[SUBLANES:]
    conv = b_ref[...] * (cw_ref[0:1, :] * z2 + cw_ref[1:2, :] * z1 + cw_ref[2:3, :] * z)

    mixed = jnp.concatenate([_rms_norm(a_ref[...], ga_ref[...]), _rms_norm(conv, gc_ref[...])], axis=1)
    y = _dot(mixed.astype(BF16), w_ref[...])
    o_ref[...] = _layer_norm(ALPHA * x_ref[...] + y, lg_ref[...], lb_ref[...])


def _mix_out(attn, proj, x, w_out, conv_w, g_attn, g_conv, ln_g, ln_b, seq):
    n_tok, d = x.shape
    tm = TM_MIX
    rows8 = tm // SUBLANES
    row = lambda i: (i, 0)
    const = lambda i: (0, 0)
    prev = lambda part: (lambda i: (jnp.maximum(i * rows8 - 1, 0), part))
    return pl.pallas_call(
        functools.partial(_mix_out_kernel, tiles_per_seq=seq // tm),
        grid=(n_tok // tm,),
        in_specs=[
            pl.BlockSpec((tm, ATTN_WIDTH), row),
            pl.BlockSpec((tm, CONV_WIDTH), lambda i: (i, 3)),
            pl.BlockSpec((tm, CONV_WIDTH), lambda i: (i, 4)),
            pl.BlockSpec((tm, CONV_WIDTH), lambda i: (i, 5)),
            pl.BlockSpec((SUBLANES, CONV_WIDTH), prev(4)),
            pl.BlockSpec((SUBLANES, CONV_WIDTH), prev(5)),
            pl.BlockSpec((tm, d), row),
            pl.BlockSpec(w_out.shape, const, pipeline_mode=pl.Buffered(1)),
            pl.BlockSpec(conv_w.shape, const),
            pl.BlockSpec(g_attn.shape, const),
            pl.BlockSpec(g_conv.shape, const),
            pl.BlockSpec(ln_g.shape, const),
            pl.BlockSpec(ln_b.shape, const)],
        out_specs=pl.BlockSpec((tm, d), row),
        out_shape=jax.ShapeDtypeStruct((n_tok, d), F32),
        compiler_params=_params("parallel"),
        name="mix_out_ln1",
    )(attn, proj, proj, proj, proj, proj, x, w_out, conv_w, g_attn, g_conv, ln_g, ln_b)


def _pack_halves(v):
    c = v.shape[1] // 2
    as_bits = lambda part: lax.bitcast_convert_type(part.astype(BF16).astype(F32), U32)
    return (as_bits(v[:, :c]) >> 16) | (as_bits(v[:, c:]) & jnp.uint32(0xFFFF0000))


def _unpack_halves(p):
    lo = lax.bitcast_convert_type(p << 16, F32).astype(BF16)
    hi = lax.bitcast_convert_type(p & jnp.uint32(0xFFFF0000), F32).astype(BF16)
    return lo, hi


def _xattn_kernel(x_ref, wq_ref, k_ref, v_ref, wo_ref, lg_ref, lb_ref, o_ref, op_ref):
    x = x_ref[...]
    d = x.shape[1]
    hd = d // N_MEM_HEADS
    q = _dot(x.astype(BF16), wq_ref[...]).astype(BF16)
    scale = hd ** -0.5
    outs = []
    for h in range(N_MEM_HEADS):
        sl = slice(h * hd, (h + 1) * hd)
        s = _dot_nt(q[:, sl], k_ref[:, sl]) * scale
        m = jnp.max(s, axis=-1, keepdims=True)
        p = jnp.exp(s - m)
        p = p / jnp.sum(p, axis=-1, keepdims=True)
        outs.append(_dot(p.astype(BF16), v_ref[:, sl]).astype(BF16))
    y = _dot(jnp.concatenate(outs, axis=1), wo_ref[...])
    out = _layer_norm(ALPHA * x + y, lg_ref[...], lb_ref[...])
    o_ref[...] = out
    op_ref[...] = _pack_halves(out)


def _xattn(x, w_q, kv, w_o, ln_g, ln_b, seq, mem_len):
    n_tok, d = x.shape
    tm = TM_XATTN
    row = lambda i: (i, 0)
    const = lambda i: (0, 0)
    tiles_per_seq = seq // tm
    return pl.pallas_call(
        _xattn_kernel,
        grid=(n_tok // tm,),
        in_specs=[pl.BlockSpec((tm, d), row),
                  pl.BlockSpec(w_q.shape, const, pipeline_mode=pl.Buffered(1)),
                  pl.BlockSpec((mem_len, d), lambda i: (i // tiles_per_seq, 0)),
                  pl.BlockSpec((mem_len, d), lambda i: (i // tiles_per_seq, 1)),
                  pl.BlockSpec(w_o.shape, const, pipeline_mode=pl.Buffered(1)),
                  pl.BlockSpec(ln_g.shape, const),
                  pl.BlockSpec(ln_b.shape, const)],
        out_specs=[pl.BlockSpec((tm, d), row), pl.BlockSpec((tm, d // 2), row)],
        out_shape=[jax.ShapeDtypeStruct((n_tok, d), F32), jax.ShapeDtypeStruct((n_tok, d // 2), U32)],
        compiler_params=_params("parallel"),
        name="xattn_ln2",
    )(x, w_q, kv, kv, w_o, ln_g, ln_b)


def _first_argmax(vals, index, sentinel):
    m = jnp.max(vals, axis=0, keepdims=True)
    i = jnp.min(jnp.where(vals == m, index, sentinel), axis=0, keepdims=True)
    return m, i


def _router_kernel(x_ref, wt_ref, bias_ref, idx_ref, gate_ref, rank_ref, cnt_ref):
    step = pl.program_id(0)
    tm = x_ref.shape[0]

    @pl.when(step == 0)
    def _():
        cnt_ref[...] = jnp.zeros_like(cnt_ref)

    x = x_ref[...]
    xh = x.astype(BF16)
    xl = (x - xh.astype(F32)).astype(BF16)
    w = wt_ref[...]
    wh = w.astype(BF16)
    wl = (w - wh.astype(F32)).astype(BF16)
    both = _dot_nt(jnp.concatenate([wh, wl], axis=0), xh)
    logits = both[:N_EXPERTS] + both[N_EXPERTS:] + _dot_nt(wh, xl)
    scores = 1.0 / (1.0 + jnp.exp(-logits))
    choice = scores + bias_ref[...]

    sub = lax.broadcasted_iota(I32, (GROUP_SIZE, tm), 0)
    group_scores = []
    for g in range(N_GROUPS):
        c = choice[g * GROUP_SIZE:(g + 1) * GROUP_SIZE, :]
        m1, i1 = _first_argmax(c, sub, GROUP_SIZE)
        m2 = jnp.max(jnp.where(sub == i1, -jnp.inf, c), axis=0, keepdims=True)
        group_scores.append(m1 + m2)
    gs = jnp.concatenate(group_scores, axis=0)
    gidx = lax.broadcasted_iota(I32, (N_GROUPS, tm), 0)
    gsel = jnp.zeros((N_GROUPS, tm), F32)
    for _ in range(TOPK_GROUPS):
        _, gi = _first_argmax(gs, gidx, N_GROUPS)
        hit = gidx == gi
        gsel = jnp.where(hit, 1.0, gsel)
        gs = jnp.where(hit, -jnp.inf, gs)
    masked = jnp.concatenate(
        [jnp.where(gsel[g:g + 1, :] > 0.0, choice[g * GROUP_SIZE:(g + 1) * GROUP_SIZE, :], -jnp.inf)
         for g in range(N_GROUPS)], axis=0)

    eidx = lax.broadcasted_iota(I32, (N_EXPERTS, tm), 0)
    hits, idxs, ws = [], [], []
    for _ in range(TOP_K):
        _, ei = _first_argmax(masked, eidx, N_EXPERTS)
        hit = eidx == ei
        hits.append(hit)
        idxs.append(ei)
        ws.append(jnp.sum(jnp.where(hit, scores, 0.0), axis=0, keepdims=True))
        masked = jnp.where(hit, -jnp.inf, masked)
    wsum = ws[0]
    for w in ws[1:]:
        wsum = wsum + w
    gate_ref[...] = jnp.concatenate([w / wsum * ROUTED_SCALE for w in ws], axis=0)
    idx_ref[...] = jnp.concatenate(idxs, axis=0)

    sel = hits[0]
    for hit in hits[1:]:
        sel = sel | hit
    self32 = jnp.where(sel, 1.0, 0.0)
    t_row = lax.broadcasted_iota(I32, (tm, tm), 0)
    t_col = lax.broadcasted_iota(I32, (tm, tm), 1)
    tri = jnp.where(t_row < t_col, 1.0, 0.0).astype(BF16)
    before = _dot(self32.astype(BF16), tri) + cnt_ref[:, 0:1]
    rank_ref[...] = jnp.concatenate(
        [jnp.sum(jnp.where(hit, before, 0.0), axis=0, keepdims=True) for hit in hits],
        axis=0).astype(I32)
    cnt_ref[...] += jnp.sum(self32, axis=1, keepdims=True)


def _router(x, w_router_t, bias_col):
    n_tok, d = x.shape
    tm = TM_ROUTER
    tok = lambda i: (0, i)
    return pl.pallas_call(
        _router_kernel,
        grid=(n_tok // tm,),
        in_specs=[pl.BlockSpec((tm, d), lambda i: (i, 0)),
                  pl.BlockSpec(w_router_t.shape, lambda i: (0, 0)),
                  pl.BlockSpec(bias_col.shape, lambda i: (0, 0))],
        out_specs=[pl.BlockSpec((TOP_K, tm), tok), pl.BlockSpec((TOP_K, tm), tok),
                   pl.BlockSpec((TOP_K, tm), tok),
                   pl.BlockSpec((N_EXPERTS, LANES), lambda i: (0, 0))],
        out_shape=[jax.ShapeDtypeStruct((TOP_K, n_tok), I32),
                   jax.ShapeDtypeStruct((TOP_K, n_tok), F32),
                   jax.ShapeDtypeStruct((TOP_K, n_tok), I32),
                   jax.ShapeDtypeStruct((N_EXPERTS, LANES), F32)],
        compiler_params=_params("arbitrary"),
        name="router",
    )(x, w_router_t, bias_col)


def _tile_view(x):
    r, d = x.shape
    return (x.reshape(r // SUBLANES, SUBLANES, d // LANES, LANES).transpose(0, 2, 1, 3)
            .reshape(r // SUBLANES, d // LANES, SUBLANES, 1, LANES))


def _row_view(v):
    g, c, s, _, l = v.shape
    return v.reshape(g, c, s, l).transpose(0, 2, 1, 3).reshape(g * s, c * l)


def _tile_row(view_ref, row):
    return view_ref.at[row >> SUBLANE_SHIFT, :, row & (SUBLANES - 1)]


def _dispatch_kernel(pad_ref, dest_ref, x_ref, xs_ref, zero_ref, sem):
    step = pl.program_id(0)
    groups = x_ref.shape[0]

    @pl.when(step == 0)
    def _():
        zero_ref[...] = jnp.zeros_like(zero_ref)

        def fill(e):
            return pltpu.make_async_copy(zero_ref, xs_ref.at[pl.ds(pad_ref[e] >> SUBLANE_SHIFT, ZERO_GROUPS)], sem)

        def start(e, c):
            fill(e).start()
            return c

        def wait(e, c):
            fill(e).wait()
            return c

        lax.fori_loop(0, N_EXPERTS, start, 0)
        lax.fori_loop(0, N_EXPERTS, wait, 0)

    def copies(g):
        return [pltpu.make_async_copy(x_ref.at[g, :, s], _tile_row(xs_ref, dest_ref[k, g * SUBLANES + s]), sem)
                for s in range(SUBLANES) for k in range(TOP_K)]

    def start_rows(g, c):
        for n, cp in enumerate(copies(g)):
            cp.start(priority=n % DMA_QUEUES)
        return c

    def wait_rows(g, c):
        for cp in copies(g):
            cp.wait()
        return c

    lax.fori_loop(0, groups, start_rows, 0)
    lax.fori_loop(0, groups, wait_rows, 0)


def _dispatch(pad_start, dest, x, n_rows):
    n_tok, w = x.shape
    tm = TM_DISPATCH
    chunks = w // LANES
    block = (tm // SUBLANES, chunks, SUBLANES, 1, LANES)
    total_groups = n_rows // SUBLANES + ZERO_GROUPS
    out = pl.pallas_call(
        _dispatch_kernel,
        grid_spec=pltpu.PrefetchScalarGridSpec(
            num_scalar_prefetch=1,
            grid=(n_tok // tm,),
            in_specs=[pl.BlockSpec((TOP_K, tm), lambda i, pad: (0, i), memory_space=pltpu.SMEM),
                      pl.BlockSpec(block, lambda i, pad: (i, 0, 0, 0, 0))],
            out_specs=pl.BlockSpec(memory_space=pl.ANY),
            scratch_shapes=[pltpu.VMEM((ZERO_GROUPS, chunks, SUBLANES, 1, LANES), x.dtype),
                            pltpu.SemaphoreType.DMA(())]),
        out_shape=jax.ShapeDtypeStruct((total_groups, chunks, SUBLANES, 1, LANES), x.dtype),
        compiler_params=_params("arbitrary"),
        name="dispatch",
    )(pad_start, dest, _tile_view(x))
    return _row_view(out)


def _expert_kernel(first_ref, nblk_ref, nused_ref, next_ref, wslot_ref, wg_hbm, wu_hbm, wd_hbm, xs_hbm, ys_hbm,
                   xbuf, ybuf, wg_f, wu_f, wd_f, wg_s, wu_s, wd_s, xsem, ysem, wsem):
    e = pl.program_id(0)
    n_used = nused_ref[0]

    def x_copy(blk, slot):
        rows = pl.ds(pl.multiple_of(blk * EXPERT_ROWS, EXPERT_ROWS), EXPERT_ROWS)
        return pltpu.make_async_copy(xs_hbm.at[rows, :], xbuf.at[slot], xsem.at[slot])

    def y_copy(blk, slot):
        rows = pl.ds(pl.multiple_of(blk * EXPERT_ROWS, EXPERT_ROWS), EXPERT_ROWS)
        return pltpu.make_async_copy(ybuf.at[slot], ys_hbm.at[rows, :], ysem.at[slot])

    def w_copies(expert, slot):
        return [pltpu.make_async_copy(src.at[expert], dst.at[slot], wsem.at[slot])
                for src, dst in ((wg_hbm, wg_f), (wu_hbm, wu_f), (wd_hbm, wd_f))]

    @pl.when(e == 0)
    def _():
        x_copy(0, 0).start()

    @pl.when(nblk_ref[e] > 0)
    def _():
        wslot = wslot_ref[e]

        @pl.when(first_ref[e] == 0)
        def _():
            for cp in w_copies(e, wslot):
                cp.start()

        for cp in w_copies(e, wslot):
            cp.wait()
        wg_s[...] = wg_f[wslot].astype(BF16)
        wu_s[...] = wu_f[wslot].astype(BF16)
        wd_s[...] = wd_f[wslot].astype(BF16)
        e_next = next_ref[e]

        @pl.when(e_next < N_EXPERTS)
        def _():
            for cp in w_copies(e_next, 1 - wslot):
                cp.start(priority=DMA_QUEUES - 1)

        def block(i, carry):
            blk = first_ref[e] + i
            slot = blk & 1
            x_copy(blk, slot).wait()

            @pl.when(blk + 1 < n_used)
            def _():
                x_copy(blk + 1, 1 - slot).start()

            @pl.when(blk >= 2)
            def _():
                y_copy(blk - 2, slot).wait()

            lo, hi = _unpack_halves(xbuf[slot])
            half = lo.shape[1]
            gate = _dot(lo, wg_s[:half, :]) + _dot(hi, wg_s[half:, :])
            up = _dot(lo, wu_s[:half, :]) + _dot(hi, wu_s[half:, :])
            hmid = (_silu(gate) * up).astype(BF16)
            ybuf[slot] = _dot(hmid, wd_s[...])
            y_copy(blk, slot).start()
            return carry

        lax.fori_loop(0, nblk_ref[e], block, 0)

    @pl.when(e == pl.num_programs(0) - 1)
    def _():
        @pl.when(n_used >= 2)
        def _():
            y_copy(n_used - 2, (n_used - 2) & 1).wait()

        y_copy(n_used - 1, (n_used - 1) & 1).wait()


def _experts(first_block, n_blocks_of, n_used, next_expert, wslot, xs, w_gate, w_up, w_down, total_blocks):
    n_exp, d, ff = w_gate.shape
    hbm = pl.BlockSpec(memory_space=pl.ANY)
    return pl.pallas_call(
        _expert_kernel,
        grid_spec=pltpu.PrefetchScalarGridSpec(
            num_scalar_prefetch=5,
            grid=(n_exp,),
            in_specs=[hbm, hbm, hbm, hbm],
            out_specs=hbm,
            scratch_shapes=[pltpu.VMEM((2, EXPERT_ROWS, d // 2), U32), pltpu.VMEM((2, EXPERT_ROWS, d), F32),
                            pltpu.VMEM((2, d, ff), F32), pltpu.VMEM((2, d, ff), F32),
                            pltpu.VMEM((2, ff, d), F32),
                            pltpu.VMEM((d, ff), BF16), pltpu.VMEM((d, ff), BF16),
                            pltpu.VMEM((ff, d), BF16),
                            pltpu.SemaphoreType.DMA((2,)), pltpu.SemaphoreType.DMA((2,)),
                            pltpu.SemaphoreType.DMA((2,))]),
        out_shape=jax.ShapeDtypeStruct((total_blocks * EXPERT_ROWS, d), F32),
        compiler_params=_params("arbitrary"),
        name="experts",
    )(first_block, n_blocks_of, n_used, next_expert, wslot, w_gate, w_up, w_down, xs)


def _combine_kernel(dest_ref, gate_ref, x_ref, ys_ref, wsg_ref, wsu_ref, wsd_ref, lg_ref, lb_ref,
                    o_ref, buf, sem):
    tm, d = x_ref.shape
    groups, chunks = tm // SUBLANES, d // LANES
    dma_views = [buf.at[k].reshape(groups, chunks, SUBLANES, 1, LANES) for k in range(TOP_K)]
    load_views = [buf.at[k].reshape(groups, chunks, SUBLANES, LANES) for k in range(TOP_K)]

    def copies(g):
        return [pltpu.make_async_copy(_tile_row(ys_ref, dest_ref[k, g * SUBLANES + s]),
                                      dma_views[k].at[g, :, s], sem)
                for s in range(SUBLANES) for k in range(TOP_K)]

    def start_rows(g, c):
        for n, cp in enumerate(copies(g)):
            cp.start(priority=n % DMA_QUEUES)
        return c

    def wait_rows(g, c):
        for cp in copies(g):
            cp.wait()
        return c

    lax.fori_loop(0, groups, start_rows, 0)

    x = x_ref[...]
    xb = x.astype(BF16)
    hmid = (_silu(_dot(xb, wsg_ref[...])) * _dot(xb, wsu_ref[...])).astype(BF16)
    acc = ALPHA * x + _dot(hmid, wsd_ref[...])

    lax.fori_loop(0, groups, wait_rows, 0)
    gates = gate_ref[...]

    def rows_of(k):
        return jnp.concatenate([load_views[k][:, c].reshape(tm, LANES) for c in range(chunks)], axis=1)

    routed = rows_of(0) * gates[:, 0:1]
    for k in range(1, TOP_K):
        routed += rows_of(k) * gates[:, k:k + 1]
    o_ref[...] = _layer_norm(acc + routed, lg_ref[...], lb_ref[...])


def _combine(dest, gates_t, x, ys, ws_gate, ws_up, ws_down, ln_g, ln_b):
    n_tok, d = x.shape
    tm = TM_COMBINE
    row = lambda i: (i, 0)
    const = lambda i: (0, 0)
    return pl.pallas_call(
        _combine_kernel,
        grid=(n_tok // tm,),
        in_specs=[pl.BlockSpec((TOP_K, tm), lambda i: (0, i), memory_space=pltpu.SMEM),
                  pl.BlockSpec((tm, TOP_K), row),
                  pl.BlockSpec((tm, d), row),
                  pl.BlockSpec(memory_space=pl.ANY),
                  pl.BlockSpec(ws_gate.shape, const),
                  pl.BlockSpec(ws_up.shape, const),
                  pl.BlockSpec(ws_down.shape, const),
                  pl.BlockSpec(ln_g.shape, const),
                  pl.BlockSpec(ln_b.shape, const)],
        out_specs=pl.BlockSpec((tm, d), row),
        out_shape=jax.ShapeDtypeStruct((n_tok, d), F32),
        scratch_shapes=[pltpu.VMEM((TOP_K, tm * (d // LANES), LANES), F32), pltpu.SemaphoreType.DMA(())],
        compiler_params=_params("arbitrary"),
        name="combine_shared_ln3",
    )(dest, gates_t, x, _tile_view(ys), ws_gate, ws_up, ws_down, ln_g, ln_b)


def _layer(x, mem, w_in, conv_w, g_attn_out, g_conv_out, w_out, ln1_g, ln1_b, w_q_mem, w_kv_mem,
           w_o_mem, ln2_g, ln2_b, w_router, router_bias, w_gate, w_up, w_down, ws_gate, ws_up,
           ws_down, ln3_g, ln3_b):
    batch, seq, d = x.shape
    mem_len = mem.shape[1]
    n_tok = batch * seq
    xf = x.reshape(n_tok, d)
    row = lambda v: v.reshape(1, -1)

    proj = _matmul(xf, w_in.astype(BF16), F32, TM_PROJ, TN_PROJ)
    attn = _dilated_attention(proj, batch, seq)
    x1 = _mix_out(attn, proj, xf, w_out.astype(BF16), conv_w, row(g_attn_out), row(g_conv_out),
                  row(ln1_g), row(ln1_b), seq)

    kv = _matmul(mem.reshape(batch * mem_len, d), w_kv_mem.astype(BF16), BF16,
                 mem_len, 1024)
    x2, x2_packed = _xattn(x1, w_q_mem.astype(BF16), kv, w_o_mem.astype(BF16), row(ln2_g), row(ln2_b),
                           seq, mem_len)

    idx, gates, rank, cnt = _router(x2, w_router.T, router_bias.reshape(N_EXPERTS, 1))
    counts = cnt[:, 0].astype(I32)
    padded = (counts + EXPERT_ROWS - 1) // EXPERT_ROWS * EXPERT_ROWS
    seg_end = jnp.cumsum(padded)
    seg_start = seg_end - padded
    experts = jnp.arange(N_EXPERTS, dtype=I32)
    dest = rank + jnp.sum(jnp.where(idx[..., None] == experts, seg_start, 0), axis=-1)
    total_blocks = -(-(n_tok * TOP_K) // EXPERT_ROWS) + N_EXPERTS
    n_used = (seg_end[-1:] // EXPERT_ROWS).astype(I32)

    xs = _dispatch(seg_start + counts, dest, x2_packed, total_blocks * EXPERT_ROWS)
    nonempty = counts > 0
    first_from = lax.cummin(jnp.where(nonempty, experts, N_EXPERTS), reverse=True)
    next_expert = jnp.concatenate([first_from[1:], jnp.full((1,), N_EXPERTS, I32)])
    wslot = (jnp.cumsum(nonempty.astype(I32)) - 1) & 1
    ys = _experts(seg_start // EXPERT_ROWS, padded // EXPERT_ROWS, n_used, next_expert, wslot, xs,
                  w_gate, w_up, w_down, total_blocks)
    out = _combine(dest, gates.T, x2, ys, ws_gate.astype(BF16), ws_up.astype(BF16),
                   ws_down.astype(BF16), row(ln3_g), row(ln3_b))
    return out.reshape(batch, seq, d)


def kernel(x, mem, w_in, conv_w, g_attn_out, g_conv_out, w_out, ln1_g, ln1_b, w_q_mem, w_kv_mem, w_o_mem, ln2_g, ln2_b, w_router, router_bias, w_gate, w_up, w_down, ws_gate, ws_up, ws_down, ln3_g, ln3_b):
    for l in range(DEPTH):
        x = _layer(x, mem, w_in[l], conv_w[l], g_attn_out[l], g_conv_out[l], w_out[l], ln1_g[l],
                   ln1_b[l], w_q_mem[l], w_kv_mem[l], w_o_mem[l], ln2_g[l], ln2_b[l], w_router[l],
                   router_bias[l], w_gate[l], w_up[l], w_down[l], ws_gate[l], ws_up[l], ws_down[l],
                   ln3_g[l], ln3_b[l])
    return x
```

```python
import functools

import jax
import jax.numpy as jnp
from jax import lax
from jax.experimental import pallas as pl
from jax.experimental.pallas import tpu as pltpu

F32 = jnp.float32
BF16 = jnp.bfloat16
I32 = jnp.int32
U32 = jnp.uint32

HEAD_DIM = 128
N_ATTN_HEADS = 8
ATTN_WIDTH = N_ATTN_HEADS * HEAD_DIM
CONV_WIDTH = 1024
DILATED_PATTERNS = ((128, 1), (512, 4), (2048, 16))
ATTN_BLOCK = 128
ATTN_SPAN = 16 * ATTN_BLOCK
N_MEM_HEADS = 4
N_EXPERTS = 64
TOP_K = 8
N_GROUPS = 8
GROUP_SIZE = N_EXPERTS // N_GROUPS
TOPK_GROUPS = 4
ROUTED_SCALE = 2.5
LN_EPS = 1e-5
RMS_EPS = 1e-6
DEPTH = 1
ALPHA = (2.0 * DEPTH) ** 0.25

LANES = 128
SUBLANES = 8
SUBLANE_SHIFT = SUBLANES.bit_length() - 1
VMEM_LIMIT = 56 * 1024 * 1024
DMA_QUEUES = 2
EXPERT_ROWS = 256
ZERO_GROUPS = EXPERT_ROWS // SUBLANES + 1
TM_PROJ, TN_PROJ = 1024, 512
TM_MIX = 512
TM_XATTN = 512
TM_ROUTER = 512
TM_DISPATCH = 512
TM_COMBINE = 256


def _params(*semantics):
    return pltpu.CompilerParams(dimension_semantics=semantics, vmem_limit_bytes=VMEM_LIMIT)


def _layer_norm(r, g, b):
    mu = jnp.mean(r, axis=-1, keepdims=True)
    c = r - mu
    var = jnp.mean(c * c, axis=-1, keepdims=True)
    return c * lax.rsqrt(var + LN_EPS) * g + b


def _rms_norm(v, g):
    return v * lax.rsqrt(jnp.mean(v * v, axis=-1, keepdims=True) + RMS_EPS) * g


def _dot(a, b):
    return jnp.dot(a, b, preferred_element_type=F32)


def _dot_nt(a, b):
    return lax.dot_general(a, b, (((1,), (1,)), ((), ())), preferred_element_type=F32)


def _silu(v):
    return v / (1.0 + jnp.exp(-v))


def _matmul_kernel(a_ref, w_ref, o_ref, a_bf):
    @pl.when(pl.program_id(1) == 0)
    def _():
        a_bf[...] = a_ref[...].astype(BF16)

    o_ref[...] = _dot(a_bf[...], w_ref[...]).astype(o_ref.dtype)


def _matmul(a, w, out_dtype, tm, tn):
    m, k = a.shape
    n = w.shape[1]
    return pl.pallas_call(
        _matmul_kernel,
        grid=(m // tm, n // tn),
        in_specs=[pl.BlockSpec((tm, k), lambda i, j: (i, 0)),
                  pl.BlockSpec((k, tn), lambda i, j: (0, j))],
        out_specs=pl.BlockSpec((tm, tn), lambda i, j: (i, j)),
        out_shape=jax.ShapeDtypeStruct((m, n), out_dtype),
        scratch_shapes=[pltpu.VMEM((tm, k), BF16)],
        compiler_params=_params("parallel", "arbitrary"),
        name="matmul",
    )(a, w)


def _attn_kernel(q_ref, kp_ref, kc_ref, vp_ref, vc_ref, o_ref, kk, vv, o_acc, l_acc):
    j = pl.program_id(1)
    blk, span = ATTN_BLOCK, ATTN_SPAN
    kk[0:span, :] = kp_ref[...]
    kk[span:, :] = kc_ref[...]
    vv[0:span, :] = vp_ref[...]
    vv[span:, :] = vc_ref[...]

    row = lax.broadcasted_iota(I32, (blk, 2 * blk), 0)
    col = lax.broadcasted_iota(I32, (blk, 2 * blk), 1)
    dist = row + blk - col
    in_window = (dist >= 0) & (dist <= blk)
    own_block = col >= blk
    scale = HEAD_DIM ** -0.5

    order = sorted(range(len(DILATED_PATTERNS)), key=lambda i: -DILATED_PATTERNS[i][1])
    assert DILATED_PATTERNS[order[-1]][1] == 1
    acc_slot = {bi: n for n, bi in enumerate(order[:-1])}
    sub_blocks = span // blk
    for bi in order:
        dil = DILATED_PATTERNS[bi][1]
        shift = dil.bit_length() - 1

        def body(t, carry, bi=bi, dil=dil, shift=shift):
            r = t & (dil - 1)
            n = t >> shift
            base = n * (blk * dil) + r
            if dil == 1:
                base = pl.multiple_of(base, blk)
                q_rows, kv_rows = pl.ds(base, blk), pl.ds(span + base - blk, 2 * blk)
            else:
                q_rows = pl.ds(base, blk, stride=dil)
                kv_rows = pl.ds(span + base - blk * dil, 2 * blk, stride=dil)
            q = q_ref[q_rows, :].astype(BF16)
            k = kk[kv_rows, :].astype(BF16)
            v = vv[kv_rows, :].astype(BF16)
            s = _dot_nt(q, k) * scale
            valid = in_window & (own_block | (j > 0) | (n > 0))
            s = jnp.where(valid, s, -jnp.inf)
            m = jnp.max(s, axis=-1, keepdims=True)
            p = jnp.exp(s - m)
            l = jnp.sum(p, axis=-1, keepdims=True)
            o = _dot((p / l).astype(BF16), v)
            lse = jnp.broadcast_to(m + jnp.log(l), (blk, LANES))
            if dil != 1:
                o_acc[acc_slot[bi], q_rows, :] = o
                l_acc[acc_slot[bi], q_rows, :] = lse
                return carry
            others = [i for i in order if i != bi]
            lses = [lse] + [l_acc[acc_slot[i], q_rows, :] for i in others]
            outs = [o] + [o_acc[acc_slot[i], q_rows, :] for i in others]
            top = jnp.maximum(jnp.maximum(lses[0], lses[1]), lses[2])
            es = [jnp.exp(branch_lse - top) for branch_lse in lses]
            den = es[0] + es[1] + es[2]
            o_ref[q_rows, :] = ((es[0] / den) * outs[0] + (es[1] / den) * outs[1] + (es[2] / den) * outs[2])
            return carry

        lax.fori_loop(0, sub_blocks, body, 0, unroll=True)


def _dilated_attention(proj, batch, seq):
    n_tok, _ = proj.shape
    span = ATTN_SPAN
    spans = seq // span
    heads = N_ATTN_HEADS

    def spec(part, prev):
        def index(b, j, h):
            return (b * spans + (jnp.maximum(j - 1, 0) if prev else j), part * heads + h)
        return pl.BlockSpec((span, HEAD_DIM), index)

    return pl.pallas_call(
        _attn_kernel,
        grid=(batch, spans, heads),
        in_specs=[spec(0, False), spec(1, True), spec(1, False), spec(2, True), spec(2, False)],
        out_specs=pl.BlockSpec((span, HEAD_DIM), lambda b, j, h: (b * spans + j, h)),
        out_shape=jax.ShapeDtypeStruct((n_tok, ATTN_WIDTH), F32),
        scratch_shapes=[pltpu.VMEM((2 * span, HEAD_DIM), F32), pltpu.VMEM((2 * span, HEAD_DIM), F32),
                        pltpu.VMEM((len(DILATED_PATTERNS) - 1, span, HEAD_DIM), F32),
                        pltpu.VMEM((len(DILATED_PATTERNS) - 1, span, HEAD_DIM), F32)],
        compiler_params=_params("parallel", "parallel", "parallel"),
        name="dilated_attn",
    )(proj, proj, proj, proj, proj)


def _mix_out_kernel(a_ref, b_ref, c_ref, h_ref, cp_ref, hp_ref, x_ref,
                    w_ref, cw_ref, ga_ref, gc_ref, lg_ref, lb_ref, o_ref, *, tiles_per_seq):
    i = pl.program_id(0)
    z = c_ref[...] * h_ref[...]
    zp = jnp.where(i % tiles_per_seq == 0, 0.0, cp_ref[...] * hp_ref[...])
    zz = jnp.concatenate([zp, z], axis=0)
    z1 = pltpu.roll(zz, 1, axis=0)[SUBLANES:]
    z2 = pltpu.roll(zz, 2, axis=0)[SUBLANES:]
    conv = b_ref[...] * (cw_ref[0:1, :] * z2 + cw_ref[1:2, :] * z1 + cw_ref[2:3, :] * z)

    mixed = jnp.concatenate([_rms_norm(a_ref[...], ga_ref[...]), _rms_norm(conv, gc_ref[...])], axis=1)
    y = _dot(mixed.astype(BF16), w_ref[...])
    o_ref[...] = _layer_norm(ALPHA * x_ref[...] + y, lg_ref[...], lb_ref[...])


def _mix_out(attn, proj, x, w_out, conv_w, g_attn, g_conv, ln_g, ln_b, seq):
    n_tok, d = x.shape
    tm = TM_MIX
    rows8 = tm // SUBLANES
    row = lambda i: (i, 0)
    const = lambda i: (0, 0)
    prev = lambda part: (lambda i: (jnp.maximum(i * rows8 - 1, 0), part))
    return pl.pallas_call(
        functools.partial(_mix_out_kernel, tiles_per_seq=seq // tm),
        grid=(n_tok // tm,),
        in_specs=[
            pl.BlockSpec((tm, ATTN_WIDTH), row),
            pl.BlockSpec((tm, CONV_WIDTH), lambda i: (i, 3)),
            pl.BlockSpec((tm, CONV_WIDTH), lambda i: (i, 4)),
            pl.BlockSpec((tm, CONV_WIDTH), lambda i: (i, 5)),
            pl.BlockSpec((SUBLANES, CONV_WIDTH), prev(4)),
            pl.BlockSpec((SUBLANES, CONV_WIDTH), prev(5)),
            pl.BlockSpec((tm, d), row),
            pl.BlockSpec(w_out.shape, const, pipeline_mode=pl.Buffered(1)),
            pl.BlockSpec(conv_w.shape, const),
            pl.BlockSpec(g_attn.shape, const),
            pl.BlockSpec(g_conv.shape, const),
            pl.BlockSpec(ln_g.shape, const),
            pl.BlockSpec(ln_b.shape, const)],
        out_specs=pl.BlockSpec((tm, d), row),
        out_shape=jax.ShapeDtypeStruct((n_tok, d), F32),
        compiler_params=_params("parallel"),
        name="mix_out_ln1",
    )(attn, proj, proj, proj, proj, proj, x, w_out, conv_w, g_attn, g_conv, ln_g, ln_b)


def _pack_halves(v):
    c = v.shape[1] // 2
    as_bits = lambda part: lax.bitcast_convert_type(part.astype(BF16).astype(F32), U32)
    return (as_bits(v[:, :c]) >> 16) | (as_bits(v[:, c:]) & jnp.uint32(0xFFFF0000))


def _unpack_halves(p):
    lo = lax.bitcast_convert_type(p << 16, F32).astype(BF16)
    hi = lax.bitcast_convert_type(p & jnp.uint32(0xFFFF0000), F32).astype(BF16)
    return lo, hi


def _xattn_kernel(x_ref, wq_ref, k_ref, v_ref, wo_ref, lg_ref, lb_ref, o_ref, op_ref):
    x = x_ref[...]
    d = x.shape[1]
    hd = d // N_MEM_HEADS
    q = _dot(x.astype(BF16), wq_ref[...]).astype(BF16)
    scale = hd ** -0.5
    outs = []
    for h in range(N_MEM_HEADS):
        sl = slice(h * hd, (h + 1) * hd)
        s = _dot_nt(q[:, sl], k_ref[:, sl]) * scale
        m = jnp.max(s, axis=-1, keepdims=True)
        p = jnp.exp(s - m)
        p = p / jnp.sum(p, axis=-1, keepdims=True)
        outs.append(_dot(p.astype(BF16), v_ref[:, sl]).astype(BF16))
    y = _dot(jnp.concatenate(outs, axis=1), wo_ref[...])
    out = _layer_norm(ALPHA * x + y, lg_ref[...], lb_ref[...])
    o_ref[...] = out
    op_ref[...] = _pack_halves(out)


def _xattn(x, w_q, kv, w_o, ln_g, ln_b, seq, mem_len):
    n_tok, d = x.shape
    tm = TM_XATTN
    row = lambda i: (i, 0)
    const = lambda i: (0, 0)
    tiles_per_seq = seq // tm
    return pl.pallas_call(
        _xattn_kernel,
        grid=(n_tok // tm,),
        in_specs=[pl.BlockSpec((tm, d), row),
                  pl.BlockSpec(w_q.shape, const, pipeline_mode=pl.Buffered(1)),
                  pl.BlockSpec((mem_len, d), lambda i: (i // tiles_per_seq, 0)),
                  pl.BlockSpec((mem_len, d), lambda i: (i // tiles_per_seq, 1)),
                  pl.BlockSpec(w_o.shape, const, pipeline_mode=pl.Buffered(1)),
                  pl.BlockSpec(ln_g.shape, const),
                  pl.BlockSpec(ln_b.shape, const)],
        out_specs=[pl.BlockSpec((tm, d), row), pl.BlockSpec((tm, d // 2), row)],
        out_shape=[jax.ShapeDtypeStruct((n_tok, d), F32), jax.ShapeDtypeStruct((n_tok, d // 2), U32)],
        compiler_params=_params("parallel"),
        name="xattn_ln2",
    )(x, w_q, kv, kv, w_o, ln_g, ln_b)


def _first_argmax(vals, index, sentinel):
    m = jnp.max(vals, axis=0, keepdims=True)
    i = jnp.min(jnp.where(vals == m, index, sentinel), axis=0, keepdims=True)
    return m, i


def _router_kernel(x_ref, wt_ref, bias_ref, idx_ref, gate_ref, rank_ref, cnt_ref):
    step = pl.program_id(0)
    tm = x_ref.shape[0]

    @pl.when(step == 0)
    def _():
        cnt_ref[...] = jnp.zeros_like(cnt_ref)

    x = x_ref[...]
    xh = x.astype(BF16)
    xl = (x - xh.astype(F32)).astype(BF16)
    w = wt_ref[...]
    wh = w.astype(BF16)
    wl = (w - wh.astype(F32)).astype(BF16)
    both = _dot_nt(jnp.concatenate([wh, wl], axis=0), xh)
    logits = both[:N_EXPERTS] + both[N_EXPERTS:] + _dot_nt(wh, xl)
    scores = 1.0 / (1.0 + jnp.exp(-logits))
    choice = scores + bias_ref[...]

    sub = lax.broadcasted_iota(I32, (GROUP_SIZE, tm), 0)
    group_scores = []
    for g in range(N_GROUPS):
        c = choice[g * GROUP_SIZE:(g + 1) * GROUP_SIZE, :]
        m1, i1 = _first_argmax(c, sub, GROUP_SIZE)
        m2 = jnp.max(jnp.where(sub == i1, -jnp.inf, c), axis=0, keepdims=True)
        group_scores.append(m1 + m2)
    gs = jnp.concatenate(group_scores, axis=0)
    gidx = lax.broadcasted_iota(I32, (N_GROUPS, tm), 0)
    gsel = jnp.zeros((N_GROUPS, tm), F32)
    for _ in range(TOPK_GROUPS):
        _, gi = _first_argmax(gs, gidx, N_GROUPS)
        hit = gidx == gi
        gsel = jnp.where(hit, 1.0, gsel)
        gs = jnp.where(hit, -jnp.inf, gs)
    masked = jnp.concatenate(
        [jnp.where(gsel[g:g + 1, :] > 0.0, choice[g * GROUP_SIZE:(g + 1) * GROUP_SIZE, :], -jnp.inf)
         for g in range(N_GROUPS)], axis=0)

    eidx = lax.broadcasted_iota(I32, (N_EXPERTS, tm), 0)
    hits, idxs, ws = [], [], []
    for _ in range(TOP_K):
        _, ei = _first_argmax(masked, eidx, N_EXPERTS)
        hit = eidx == ei
        hits.append(hit)
        idxs.append(ei)
        ws.append(jnp.sum(jnp.where(hit, scores, 0.0), axis=0, keepdims=True))
        masked = jnp.where(hit, -jnp.inf, masked)
    wsum = ws[0]
    for w in ws[1:]:
        wsum = wsum + w
    gate_ref[...] = jnp.concatenate([w / wsum * ROUTED_SCALE for w in ws], axis=0)
    idx_ref[...] = jnp.concatenate(idxs, axis=0)

    sel = hits[0]
    for hit in hits[1:]:
        sel = sel | hit
    self32 = jnp.where(sel, 1.0, 0.0)
    t_row = lax.broadcasted_iota(I32, (tm, tm), 0)
    t_col = lax.broadcasted_iota(I32, (tm, tm), 1)
    tri = jnp.where(t_row < t_col, 1.0, 0.0).astype(BF16)
    before = _dot(self32.astype(BF16), tri) + cnt_ref[:, 0:1]
    rank_ref[...] = jnp.concatenate(
        [jnp.sum(jnp.where(hit, before, 0.0), axis=0, keepdims=True) for hit in hits],
        axis=0).astype(I32)
    cnt_ref[...] += jnp.sum(self32, axis=1, keepdims=True)


def _router(x, w_router_t, bias_col):
    n_tok, d = x.shape
    tm = TM_ROUTER
    tok = lambda i: (0, i)
    return pl.pallas_call(
        _router_kernel,
        grid=(n_tok // tm,),
        in_specs=[pl.BlockSpec((tm, d), lambda i: (i, 0)),
                  pl.BlockSpec(w_router_t.shape, lambda i: (0, 0)),
                  pl.BlockSpec(bias_col.shape, lambda i: (0, 0))],
        out_specs=[pl.BlockSpec((TOP_K, tm), tok), pl.BlockSpec((TOP_K, tm), tok),
                   pl.BlockSpec((TOP_K, tm), tok),
                   pl.BlockSpec((N_EXPERTS, LANES), lambda i: (0, 0))],
        out_shape=[jax.ShapeDtypeStruct((TOP_K, n_tok), I32),
                   jax.ShapeDtypeStruct((TOP_K, n_tok), F32),
                   jax.ShapeDtypeStruct((TOP_K, n_tok), I32),
                   jax.ShapeDtypeStruct((N_EXPERTS, LANES), F32)],
        compiler_params=_params("arbitrary"),
        name="router",
    )(x, w_router_t, bias_col)


def _tile_view(x):
    r, d = x.shape
    return (x.reshape(r // SUBLANES, SUBLANES, d // LANES, LANES).transpose(0, 2, 1, 3)
            .reshape(r // SUBLANES, d // LANES, SUBLANES, 1, LANES))


def _row_view(v):
    g, c, s, _, l = v.shape
    return v.reshape(g, c, s, l).transpose(0, 2, 1, 3).reshape(g * s, c * l)


def _tile_row(view_ref, row):
    return view_ref.at[row >> SUBLANE_SHIFT, :, row & (SUBLANES - 1)]


def _dispatch_kernel(pad_ref, dest_ref, x_ref, xs_ref, zero_ref, sem):
    step = pl.program_id(0)
    groups = x_ref.shape[0]

    @pl.when(step == 0)
    def _():
        zero_ref[...] = jnp.zeros_like(zero_ref)

        def fill(e):
            return pltpu.make_async_copy(zero_ref, xs_ref.at[pl.ds(pad_ref[e] >> SUBLANE_SHIFT, ZERO_GROUPS)], sem)

        def start(e, c):
            fill(e).start()
            return c

        def wait(e, c):
            fill(e).wait()
            return c

        lax.fori_loop(0, N_EXPERTS, start, 0)
        lax.fori_loop(0, N_EXPERTS, wait, 0)

    def copies(g):
        return [pltpu.make_async_copy(x_ref.at[g, :, s], _tile_row(xs_ref, dest_ref[k, g * SUBLANES + s]), sem)
                for s in range(SUBLANES) for k in range(TOP_K)]

    def start_rows(g, c):
        for n, cp in enumerate(copies(g)):
            cp.start(priority=n % DMA_QUEUES)
        return c

    def wait_rows(g, c):
        for cp in copies(g):
            cp.wait()
        return c

    lax.fori_loop(0, groups, start_rows, 0, unroll=True)
    lax.fori_loop(0, groups, wait_rows, 0)


def _dispatch(pad_start, dest, x, n_rows):
    n_tok, w = x.shape
    tm = TM_DISPATCH
    chunks = w // LANES
    block = (tm // SUBLANES, chunks, SUBLANES, 1, LANES)
    total_groups = n_rows // SUBLANES + ZERO_GROUPS
    out = pl.pallas_call(
        _dispatch_kernel,
        grid_spec=pltpu.PrefetchScalarGridSpec(
            num_scalar_prefetch=1,
            grid=(n_tok // tm,),
            in_specs=[pl.BlockSpec((TOP_K, tm), lambda i, pad: (0, i), memory_space=pltpu.SMEM),
                      pl.BlockSpec(block, lambda i, pad: (i, 0, 0, 0, 0))],
            out_specs=pl.BlockSpec(memory_space=pl.ANY),
            scratch_shapes=[pltpu.VMEM((ZERO_GROUPS, chunks, SUBLANES, 1, LANES), x.dtype),
                            pltpu.SemaphoreType.DMA(())]),
        out_shape=jax.ShapeDtypeStruct((total_groups, chunks, SUBLANES, 1, LANES), x.dtype),
        compiler_params=_params("arbitrary"),
        name="dispatch",
    )(pad_start, dest, _tile_view(x))
    return _row_view(out)


def _expert_kernel(first_ref, nblk_ref, nused_ref, next_ref, wslot_ref, wg_hbm, wu_hbm, wd_hbm, xs_hbm, ys_hbm,
                   xbuf, ybuf, wg_f, wu_f, wd_f, wg_s, wu_s, wd_s, xsem, ysem, wsem):
    e = pl.program_id(0)
    n_used = nused_ref[0]

    def x_copy(blk, slot):
        rows = pl.ds(pl.multiple_of(blk * EXPERT_ROWS, EXPERT_ROWS), EXPERT_ROWS)
        return pltpu.make_async_copy(xs_hbm.at[rows, :], xbuf.at[slot], xsem.at[slot])

    def y_copy(blk, slot):
        rows = pl.ds(pl.multiple_of(blk * EXPERT_ROWS, EXPERT_ROWS), EXPERT_ROWS)
        return pltpu.make_async_copy(ybuf.at[slot], ys_hbm.at[rows, :], ysem.at[slot])

    def w_copies(expert, slot):
        return [pltpu.make_async_copy(src.at[expert], dst.at[slot], wsem.at[slot])
                for src, dst in ((wg_hbm, wg_f), (wu_hbm, wu_f), (wd_hbm, wd_f))]

    @pl.when(e == 0)
    def _():
        x_copy(0, 0).start()

    @pl.when(nblk_ref[e] > 0)
    def _():
        wslot = wslot_ref[e]

        @pl.when(first_ref[e] == 0)
        def _():
            for cp in w_copies(e, wslot):
                cp.start()

        for cp in w_copies(e, wslot):
            cp.wait()
        wg_s[...] = wg_f[wslot].astype(BF16)
        wu_s[...] = wu_f[wslot].astype(BF16)
        wd_s[...] = wd_f[wslot].astype(BF16)
        e_next = next_ref[e]

        @pl.when(e_next < N_EXPERTS)
        def _():
            for cp in w_copies(e_next, 1 - wslot):
                cp.start(priority=DMA_QUEUES - 1)

        def block(i, carry):
            blk = first_ref[e] + i
            slot = blk & 1
            x_copy(blk, slot).wait()

            @pl.when(blk + 1 < n_used)
            def _():
                x_copy(blk + 1, 1 - slot).start()

            @pl.when(blk >= 2)
            def _():
                y_copy(blk - 2, slot).wait()

            lo, hi = _unpack_halves(xbuf[slot])
            half = lo.shape[1]
            gate = _dot(lo, wg_s[:half, :]) + _dot(hi, wg_s[half:, :])
            up = _dot(lo, wu_s[:half, :]) + _dot(hi, wu_s[half:, :])
            hmid = (_silu(gate) * up).astype(BF16)
            ybuf[slot] = _dot(hmid, wd_s[...])
            y_copy(blk, slot).start()
            return carry

        lax.fori_loop(0, nblk_ref[e], block, 0)

    @pl.when(e == pl.num_programs(0) - 1)
    def _():
        @pl.when(n_used >= 2)
        def _():
            y_copy(n_used - 2, (n_used - 2) & 1).wait()

        y_copy(n_used - 1, (n_used - 1) & 1).wait()


def _experts(first_block, n_blocks_of, n_used, next_expert, wslot, xs, w_gate, w_up, w_down, total_blocks):
    n_exp, d, ff = w_gate.shape
    hbm = pl.BlockSpec(memory_space=pl.ANY)
    return pl.pallas_call(
        _expert_kernel,
        grid_spec=pltpu.PrefetchScalarGridSpec(
            num_scalar_prefetch=5,
            grid=(n_exp,),
            in_specs=[hbm, hbm, hbm, hbm],
            out_specs=hbm,
            scratch_shapes=[pltpu.VMEM((2, EXPERT_ROWS, d // 2), U32), pltpu.VMEM((2, EXPERT_ROWS, d), F32),
                            pltpu.VMEM((2, d, ff), F32), pltpu.VMEM((2, d, ff), F32),
                            pltpu.VMEM((2, ff, d), F32),
                            pltpu.VMEM((d, ff), BF16), pltpu.VMEM((d, ff), BF16),
                            pltpu.VMEM((ff, d), BF16),
                            pltpu.SemaphoreType.DMA((2,)), pltpu.SemaphoreType.DMA((2,)),
                            pltpu.SemaphoreType.DMA((2,))]),
        out_shape=jax.ShapeDtypeStruct((total_blocks * EXPERT_ROWS, d), F32),
        compiler_params=_params("arbitrary"),
        name="experts",
    )(first_block, n_blocks_of, n_used, next_expert, wslot, w_gate, w_up, w_down, xs)


def _combine_kernel(dest_ref, gate_ref, x_ref, ys_ref, wsg_ref, wsu_ref, wsd_ref, lg_ref, lb_ref,
                    o_ref, buf, sem):
    tm, d = x_ref.shape
    groups, chunks = tm // SUBLANES, d // LANES
    dma_views = [buf.at[k].reshape(groups, chunks, SUBLANES, 1, LANES) for k in range(TOP_K)]
    load_views = [buf.at[k].reshape(groups, chunks, SUBLANES, LANES) for k in range(TOP_K)]

    def copies(g):
        return [pltpu.make_async_copy(_tile_row(ys_ref, dest_ref[k, g * SUBLANES + s]),
                                      dma_views[k].at[g, :, s], sem)
                for s in range(SUBLANES) for k in range(TOP_K)]

    def start_rows(g, c):
        for n, cp in enumerate(copies(g)):
            cp.start(priority=n % DMA_QUEUES)
        return c

    def wait_rows(g, c):
        for cp in copies(g):
            cp.wait()
        return c

    lax.fori_loop(0, groups, start_rows, 0, unroll=True)

    x = x_ref[...]
    xb = x.astype(BF16)
    hmid = (_silu(_dot(xb, wsg_ref[...])) * _dot(xb, wsu_ref[...])).astype(BF16)
    acc = ALPHA * x + _dot(hmid, wsd_ref[...])

    lax.fori_loop(0, groups, wait_rows, 0)
    gates = gate_ref[...]

    def rows_of(k):
        return jnp.concatenate([load_views[k][:, c].reshape(tm, LANES) for c in range(chunks)], axis=1)

    routed = rows_of(0) * gates[:, 0:1]
    for k in range(1, TOP_K):
        routed += rows_of(k) * gates[:, k:k + 1]
    o_ref[...] = _layer_norm(acc + routed, lg_ref[...], lb_ref[...])


def _combine(dest, gates_t, x, ys, ws_gate, ws_up, ws_down, ln_g, ln_b):
    n_tok, d = x.shape
    tm = TM_COMBINE
    row = lambda i: (i, 0)
    const = lambda i: (0, 0)
    return pl.pallas_call(
        _combine_kernel,
        grid=(n_tok // tm,),
        in_specs=[pl.BlockSpec((TOP_K, tm), lambda i: (0, i), memory_space=pltpu.SMEM),
                  pl.BlockSpec((tm, TOP_K), row),
                  pl.BlockSpec((tm, d), row),
                  pl.BlockSpec(memory_space=pl.ANY),
                  pl.BlockSpec(ws_gate.shape, const),
                  pl.BlockSpec(ws_up.shape, const),
                  pl.BlockSpec(ws_down.shape, const),
                  pl.BlockSpec(ln_g.shape, const),
                  pl.BlockSpec(ln_b.shape, const)],
        out_specs=pl.BlockSpec((tm, d), row),
        out_shape=jax.ShapeDtypeStruct((n_tok, d), F32),
        scratch_shapes=[pltpu.VMEM((TOP_K, tm * (d // LANES), LANES), F32), pltpu.SemaphoreType.DMA(())],
        compiler_params=_params("arbitrary"),
        name="combine_shared_ln3",
    )(dest, gates_t, x, _tile_view(ys), ws_gate, ws_up, ws_down, ln_g, ln_b)


def _layer(x, mem, w_in, conv_w, g_attn_out, g_conv_out, w_out, ln1_g, ln1_b, w_q_mem, w_kv_mem,
           w_o_mem, ln2_g, ln2_b, w_router, router_bias, w_gate, w_up, w_down, ws_gate, ws_up,
           ws_down, ln3_g, ln3_b):
    batch, seq, d = x.shape
    mem_len = mem.shape[1]
    n_tok = batch * seq
    xf = x.reshape(n_tok, d)
    row = lambda v: v.reshape(1, -1)

    proj = _matmul(xf, w_in.astype(BF16), F32, TM_PROJ, TN_PROJ)
    attn = _dilated_attention(proj, batch, seq)
    x1 = _mix_out(attn, proj, xf, w_out.astype(BF16), conv_w, row(g_attn_out), row(g_conv_out),
                  row(ln1_g), row(ln1_b), seq)

    kv = _matmul(mem.reshape(batch * mem_len, d), w_kv_mem.astype(BF16), BF16,
                 mem_len, 1024)
    x2, x2_packed = _xattn(x1, w_q_mem.astype(BF16), kv, w_o_mem.astype(BF16), row(ln2_g), row(ln2_b),
                           seq, mem_len)

    idx, gates, rank, cnt = _router(x2, w_router.T, router_bias.reshape(N_EXPERTS, 1))
    counts = cnt[:, 0].astype(I32)
    padded = (counts + EXPERT_ROWS - 1) // EXPERT_ROWS * EXPERT_ROWS
    seg_end = jnp.cumsum(padded)
    seg_start = seg_end - padded
    experts = jnp.arange(N_EXPERTS, dtype=I32)
    dest = rank + jnp.sum(jnp.where(idx[..., None] == experts, seg_start, 0), axis=-1)
    total_blocks = -(-(n_tok * TOP_K) // EXPERT_ROWS) + N_EXPERTS
    n_used = (seg_end[-1:] // EXPERT_ROWS).astype(I32)

    xs = _dispatch(seg_start + counts, dest, x2_packed, total_blocks * EXPERT_ROWS)
    nonempty = counts > 0
    first_from = lax.cummin(jnp.where(nonempty, experts, N_EXPERTS), reverse=True)
    next_expert = jnp.concatenate([first_from[1:], jnp.full((1,), N_EXPERTS, I32)])
    wslot = (jnp.cumsum(nonempty.astype(I32)) - 1) & 1
    ys = _experts(seg_start // EXPERT_ROWS, padded // EXPERT_ROWS, n_used, next_expert, wslot, xs,
                  w_gate, w_up, w_down, total_blocks)
    out = _combine(dest, gates.T, x2, ys, ws_gate.astype(BF16), ws_up.astype(BF16),
                   ws_down.astype(BF16), row(ln3_g), row(ln3_b))
    return out.reshape(batch, seq, d)


def kernel(x, mem, w_in, conv_w, g_attn_out, g_conv_out, w_out, ln1_g, ln1_b, w_q_mem, w_kv_mem, w_o_mem, ln2_g, ln2_b, w_router, router_bias, w_gate, w_up, w_down, ws_gate, ws_up, ws_down, ln3_g, ln3_b):
    for l in range(DEPTH):
        x = _layer(x, mem, w_in[l], conv_w[l], g_attn_out[l], g_conv_out[l], w_out[l], ln1_g[l],
                   ln1_b[l], w_q_mem[l], w_kv_mem[l], w_o_mem[l], ln2_g[l], ln2_b[l], w_router[l],
                   router_bias[l], w_gate[l], w_up[l], w_down[l], ws_gate[l], ws_up[l], ws_down[l],
                   ln3_g[l], ln3_b[l])
    return x
```

```python
import functools

import jax
import jax.numpy as jnp
from jax import lax
from jax.experimental import pallas as pl
from jax.experimental.pallas import tpu as pltpu

F32 = jnp.float32
BF16 = jnp.bfloat16
I32 = jnp.int32
U32 = jnp.uint32

HEAD_DIM = 128
N_ATTN_HEADS = 8
ATTN_WIDTH = N_ATTN_HEADS * HEAD_DIM
CONV_WIDTH = 1024
DILATED_PATTERNS = ((128, 1), (512, 4), (2048, 16))
ATTN_BLOCK = 128
ATTN_SPAN = 16 * ATTN_BLOCK
N_MEM_HEADS = 4
N_EXPERTS = 64
TOP_K = 8
N_GROUPS = 8
GROUP_SIZE = N_EXPERTS // N_GROUPS
TOPK_GROUPS = 4
ROUTED_SCALE = 2.5
LN_EPS = 1e-5
RMS_EPS = 1e-6
DEPTH = 1
ALPHA = (2.0 * DEPTH) ** 0.25

LANES = 128
SUBLANES = 8
SUBLANE_SHIFT = SUBLANES.bit_length() - 1
VMEM_LIMIT = 56 * 1024 * 1024
DMA_QUEUES = 2
EXPERT_ROWS = 256
ZERO_GROUPS = EXPERT_ROWS // SUBLANES + 1
TM_PROJ, TN_PROJ = 1024, 512
TM_MIX = 512
TM_XATTN = 512
TM_ROUTER = 512
TM_DISPATCH = 512
TM_COMBINE = 256


def _params(*semantics):
    return pltpu.CompilerParams(dimension_semantics=semantics, vmem_limit_bytes=VMEM_LIMIT)


def _layer_norm(r, g, b):
    mu = jnp.mean(r, axis=-1, keepdims=True)
    c = r - mu
    var = jnp.mean(c * c, axis=-1, keepdims=True)
    return c * lax.rsqrt(var + LN_EPS) * g + b


def _rms_norm(v, g):
    return v * lax.rsqrt(jnp.mean(v * v, axis=-1, keepdims=True) + RMS_EPS) * g


def _dot(a, b):
    return jnp.dot(a, b, preferred_element_type=F32)


def _dot_nt(a, b):
    return lax.dot_general(a, b, (((1,), (1,)), ((), ())), preferred_element_type=F32)


def _silu(v):
    return v / (1.0 + jnp.exp(-v))


def _matmul_kernel(a_ref, w_ref, o_ref, a_bf):
    @pl.when(pl.program_id(1) == 0)
    def _():
        a_bf[...] = a_ref[...].astype(BF16)

    o_ref[...] = _dot(a_bf[...], w_ref[...]).astype(o_ref.dtype)


def _matmul(a, w, out_dtype, tm, tn):
    m, k = a.shape
    n = w.shape[1]
    return pl.pallas_call(
        _matmul_kernel,
        grid=(m // tm, n // tn),
        in_specs=[pl.BlockSpec((tm, k), lambda i, j: (i, 0)),
                  pl.BlockSpec((k, tn), lambda i, j: (0, j))],
        out_specs=pl.BlockSpec((tm, tn), lambda i, j: (i, j)),
        out_shape=jax.ShapeDtypeStruct((m, n), out_dtype),
        scratch_shapes=[pltpu.VMEM((tm, k), BF16)],
        compiler_params=_params("parallel", "arbitrary"),
        name="matmul",
    )(a, w)


def _attn_kernel(q_ref, kp_ref, kc_ref, vp_ref, vc_ref, o_ref, kk, vv, o_acc, l_acc):
    j = pl.program_id(1)
    blk, span = ATTN_BLOCK, ATTN_SPAN
    kk[0:span, :] = kp_ref[...]
    kk[span:, :] = kc_ref[...]
    vv[0:span, :] = vp_ref[...]
    vv[span:, :] = vc_ref[...]

    row = lax.broadcasted_iota(I32, (blk, 2 * blk), 0)
    col = lax.broadcasted_iota(I32, (blk, 2 * blk), 1)
    dist = row + blk - col
    in_window = (dist >= 0) & (dist <= blk)
    own_block = col >= blk
    scale = HEAD_DIM ** -0.5

    order = sorted(range(len(DILATED_PATTERNS)), key=lambda i: -DILATED_PATTERNS[i][1])
    assert DILATED_PATTERNS[order[-1]][1] == 1
    acc_slot = {bi: n for n, bi in enumerate(order[:-1])}
    sub_blocks = span // blk
    for bi in order:
        dil = DILATED_PATTERNS[bi][1]
        shift = dil.bit_length() - 1

        def body(t, carry, bi=bi, dil=dil, shift=shift):
            r = t & (dil - 1)
            n = t >> shift
            base = n * (blk * dil) + r
            if dil == 1:
                base = pl.multiple_of(base, blk)
                q_rows, kv_rows = pl.ds(base, blk), pl.ds(span + base - blk, 2 * blk)
            else:
                q_rows = pl.ds(base, blk, stride=dil)
                kv_rows = pl.ds(span + base - blk * dil, 2 * blk, stride=dil)
            q = q_ref[q_rows, :].astype(BF16)
            k = kk[kv_rows, :].astype(BF16)
            v = vv[kv_rows, :].astype(BF16)
            s = _dot_nt(q, k) * scale
            valid = in_window & (own_block | (j > 0) | (n > 0))
            s = jnp.where(valid, s, -jnp.inf)
            m = jnp.max(s, axis=-1, keepdims=True)
            p = jnp.exp(s - m)
            l = jnp.sum(p, axis=-1, keepdims=True)
            o = _dot((p / l).astype(BF16), v)
            lse = jnp.broadcast_to(m + jnp.log(l), (blk, LANES))
            if dil != 1:
                o_acc[acc_slot[bi], q_rows, :] = o
                l_acc[acc_slot[bi], q_rows, :] = lse
                return carry
            others = [i for i in order if i != bi]
            lses = [lse] + [l_acc[acc_slot[i], q_rows, :] for i in others]
            outs = [o] + [o_acc[acc_slot[i], q_rows, :] for i in others]
            top = jnp.maximum(jnp.maximum(lses[0], lses[1]), lses[2])
            es = [jnp.exp(branch_lse - top) for branch_lse in lses]
            den = es[0] + es[1] + es[2]
            o_ref[q_rows, :] = ((es[0] / den) * outs[0] + (es[1] / den) * outs[1] + (es[2] / den) * outs[2])
            return carry

        lax.fori_loop(0, sub_blocks, body, 0, unroll=True)


def _dilated_attention(proj, batch, seq):
    n_tok, _ = proj.shape
    span = ATTN_SPAN
    spans = seq // span
    heads = N_ATTN_HEADS

    def spec(part, prev):
        def index(b, j, h):
            return (b * spans + (jnp.maximum(j - 1, 0) if prev else j), part * heads + h)
        return pl.BlockSpec((span, HEAD_DIM), index)

    return pl.pallas_call(
        _attn_kernel,
        grid=(batch, spans, heads),
        in_specs=[spec(0, False), spec(1, True), spec(1, False), spec(2, True), spec(2, False)],
        out_specs=pl.BlockSpec((span, HEAD_DIM), lambda b, j, h: (b * spans + j, h)),
        out_shape=jax.ShapeDtypeStruct((n_tok, ATTN_WIDTH), F32),
        scratch_shapes=[pltpu.VMEM((2 * span, HEAD_DIM), F32), pltpu.VMEM((2 * span, HEAD_DIM), F32),
                        pltpu.VMEM((len(DILATED_PATTERNS) - 1, span, HEAD_DIM), F32),
                        pltpu.VMEM((len(DILATED_PATTERNS) - 1, span, HEAD_DIM), F32)],
        compiler_params=_params("parallel", "parallel", "parallel"),
        name="dilated_attn",
    )(proj, proj, proj, proj, proj)


def _mix_out_kernel(a_ref, b_ref, c_ref, h_ref, cp_ref, hp_ref, x_ref,
                    w_ref, cw_ref, ga_ref, gc_ref, lg_ref, lb_ref, o_ref, *, tiles_per_seq):
    i = pl.program_id(0)
    z = c_ref[...] * h_ref[...]
    zp = jnp.where(i % tiles_per_seq == 0, 0.0, cp_ref[...] * hp_ref[...])
    zz = jnp.concatenate([zp, z], axis=0)
    z1 = pltpu.roll(zz, 1, axis=0)[SUBLANES:]
    z2 = pltpu.roll(zz, 2, axis=0)[SUBLANES:]
    conv = b_ref[...] * (cw_ref[0:1, :] * z2 + cw_ref[1:2, :] * z1 + cw_ref[2:3, :] * z)

    mixed = jnp.concatenate([_rms_norm(a_ref[...], ga_ref[...]), _rms_norm(conv, gc_ref[...])], axis=1)
    y = _dot(mixed.astype(BF16), w_ref[...])
    o_ref[...] = _layer_norm(ALPHA * x_ref[...] + y, lg_ref[...], lb_ref[...])


def _mix_out(attn, proj, x, w_out, conv_w, g_attn, g_conv, ln_g, ln_b, seq):
    n_tok, d = x.shape
    tm = TM_MIX
    rows8 = tm // SUBLANES
    row = lambda i: (i, 0)
    const = lambda i: (0, 0)
    prev = lambda part: (lambda i: (jnp.maximum(i * rows8 - 1, 0), part))
    return pl.pallas_call(
        functools.partial(_mix_out_kernel, tiles_per_seq=seq // tm),
        grid=(n_tok // tm,),
        in_specs=[
            pl.BlockSpec((tm, ATTN_WIDTH), row),
            pl.BlockSpec((tm, CONV_WIDTH), lambda i: (i, 3)),
            pl.BlockSpec((tm, CONV_WIDTH), lambda i: (i, 4)),
            pl.BlockSpec((tm, CONV_WIDTH), lambda i: (i, 5)),
            pl.BlockSpec((SUBLANES, CONV_WIDTH), prev(4)),
            pl.BlockSpec((SUBLANES, CONV_WIDTH), prev(5)),
            pl.BlockSpec((tm, d), row),
            pl.BlockSpec(w_out.shape, const, pipeline_mode=pl.Buffered(1)),
            pl.BlockSpec(conv_w.shape, const),
            pl.BlockSpec(g_attn.shape, const),
            pl.BlockSpec(g_conv.shape, const),
            pl.BlockSpec(ln_g.shape, const),
            pl.BlockSpec(ln_b.shape, const)],
        out_specs=pl.BlockSpec((tm, d), row),
        out_shape=jax.ShapeDtypeStruct((n_tok, d), F32),
        compiler_params=_params("parallel"),
        name="mix_out_ln1",
    )(attn, proj, proj, proj, proj, proj, x, w_out, conv_w, g_attn, g_conv, ln_g, ln_b)


def _pack_halves(v):
    c = v.shape[1] // 2
    as_bits = lambda part: lax.bitcast_convert_type(part.astype(BF16).astype(F32), U32)
    return (as_bits(v[:, :c]) >> 16) | (as_bits(v[:, c:]) & jnp.uint32(0xFFFF0000))


def _unpack_halves(p):
    lo = lax.bitcast_convert_type(p << 16, F32).astype(BF16)
    hi = lax.bitcast_convert_type(p & jnp.uint32(0xFFFF0000), F32).astype(BF16)
    return lo, hi


def _xattn_kernel(x_ref, wq_ref, k_ref, v_ref, wo_ref, lg_ref, lb_ref, o_ref, op_ref):
    x = x_ref[...]
    d = x.shape[1]
    hd = d // N_MEM_HEADS
    q = _dot(x.astype(BF16), wq_ref[...]).astype(BF16)
    scale = hd ** -0.5
    outs = []
    for h in range(N_MEM_HEADS):
        sl = slice(h * hd, (h + 1) * hd)
        s = _dot_nt(q[:, sl], k_ref[:, sl]) * scale
        m = jnp.max(s, axis=-1, keepdims=True)
        p = jnp.exp(s - m)
        p = p / jnp.sum(p, axis=-1, keepdims=True)
        outs.append(_dot(p.astype(BF16), v_ref[:, sl]).astype(BF16))
    y = _dot(jnp.concatenate(outs, axis=1), wo_ref[...])
    out = _layer_norm(ALPHA * x + y, lg_ref[...], lb_ref[...])
    o_ref[...] = out
    op_ref[...] = _pack_halves(out)


def _xattn(x, w_q, kv, w_o, ln_g, ln_b, seq, mem_len):
    n_tok, d = x.shape
    tm = TM_XATTN
    row = lambda i: (i, 0)
    const = lambda i: (0, 0)
    tiles_per_seq = seq // tm
    return pl.pallas_call(
        _xattn_kernel,
        grid=(n_tok // tm,),
        in_specs=[pl.BlockSpec((tm, d), row),
                  pl.BlockSpec(w_q.shape, const, pipeline_mode=pl.Buffered(1)),
                  pl.BlockSpec((mem_len, d), lambda i: (i // tiles_per_seq, 0)),
                  pl.BlockSpec((mem_len, d), lambda i: (i // tiles_per_seq, 1)),
                  pl.BlockSpec(w_o.shape, const, pipeline_mode=pl.Buffered(1)),
                  pl.BlockSpec(ln_g.shape, const),
                  pl.BlockSpec(ln_b.shape, const)],
        out_specs=[pl.BlockSpec((tm, d), row), pl.BlockSpec((tm, d // 2), row)],
        out_shape=[jax.ShapeDtypeStruct((n_tok, d), F32), jax.ShapeDtypeStruct((n_tok, d // 2), U32)],
        compiler_params=_params("parallel"),
        name="xattn_ln2",
    )(x, w_q, kv, kv, w_o, ln_g, ln_b)


def _first_argmax(vals, index, sentinel):
    m = jnp.max(vals, axis=0, keepdims=True)
    i = jnp.min(jnp.where(vals == m, index, sentinel), axis=0, keepdims=True)
    return m, i


def _router_kernel(x_ref, wt_ref, bias_ref, idx_ref, gate_ref, rank_ref, cnt_ref):
    step = pl.program_id(0)
    tm = x_ref.shape[0]

    @pl.when(step == 0)
    def _():
        cnt_ref[...] = jnp.zeros_like(cnt_ref)

    x = x_ref[...]
    xh = x.astype(BF16)
    xl = (x - xh.astype(F32)).astype(BF16)
    w = wt_ref[...]
    wh = w.astype(BF16)
    wl = (w - wh.astype(F32)).astype(BF16)
    both = _dot_nt(jnp.concatenate([wh, wl], axis=0), xh)
    logits = both[:N_EXPERTS] + both[N_EXPERTS:] + _dot_nt(wh, xl)
    scores = 1.0 / (1.0 + jnp.exp(-logits))
    choice = scores + bias_ref[...]

    sub = lax.broadcasted_iota(I32, (GROUP_SIZE, tm), 0)
    group_scores = []
    for g in range(N_GROUPS):
        c = choice[g * GROUP_SIZE:(g + 1) * GROUP_SIZE, :]
        m1, i1 = _first_argmax(c, sub, GROUP_SIZE)
        m2 = jnp.max(jnp.where(sub == i1, -jnp.inf, c), axis=0, keepdims=True)
        group_scores.append(m1 + m2)
    gs = jnp.concatenate(group_scores, axis=0)
    gidx = lax.broadcasted_iota(I32, (N_GROUPS, tm), 0)
    gsel = jnp.zeros((N_GROUPS, tm), F32)
    for _ in range(TOPK_GROUPS):
        _, gi = _first_argmax(gs, gidx, N_GROUPS)
        hit = gidx == gi
        gsel = jnp.where(hit, 1.0, gsel)
        gs = jnp.where(hit, -jnp.inf, gs)
    masked = jnp.concatenate(
        [jnp.where(gsel[g:g + 1, :] > 0.0, choice[g * GROUP_SIZE:(g + 1) * GROUP_SIZE, :], -jnp.inf)
         for g in range(N_GROUPS)], axis=0)

    eidx = lax.broadcasted_iota(I32, (N_EXPERTS, tm), 0)
    hits, idxs, ws = [], [], []
    for _ in range(TOP_K):
        _, ei = _first_argmax(masked, eidx, N_EXPERTS)
        hit = eidx == ei
        hits.append(hit)
        idxs.append(ei)
        ws.append(jnp.sum(jnp.where(hit, scores, 0.0), axis=0, keepdims=True))
        masked = jnp.where(hit, -jnp.inf, masked)
    wsum = ws[0]
    for w in ws[1:]:
        wsum = wsum + w
    gate_ref[...] = jnp.concatenate([w / wsum * ROUTED_SCALE for w in ws], axis=0)
    idx_ref[...] = jnp.concatenate(idxs, axis=0)

    sel = hits[0]
    for hit in hits[1:]:
        sel = sel | hit
    self32 = jnp.where(sel, 1.0, 0.0)
    t_row = lax.broadcasted_iota(I32, (tm, tm), 0)
    t_col = lax.broadcasted_iota(I32, (tm, tm), 1)
    tri = jnp.where(t_row < t_col, 1.0, 0.0).astype(BF16)
    before = _dot(self32.astype(BF16), tri) + cnt_ref[:, 0:1]
    rank_ref[...] = jnp.concatenate(
        [jnp.sum(jnp.where(hit, before, 0.0), axis=0, keepdims=True) for hit in hits],
        axis=0).astype(I32)
    cnt_ref[...] += jnp.sum(self32, axis=1, keepdims=True)


def _router(x, w_router_t, bias_col):
    n_tok, d = x.shape
    tm = TM_ROUTER
    tok = lambda i: (0, i)
    return pl.pallas_call(
        _router_kernel,
        grid=(n_tok // tm,),
        in_specs=[pl.BlockSpec((tm, d), lambda i: (i, 0)),
                  pl.BlockSpec(w_router_t.shape, lambda i: (0, 0)),
                  pl.BlockSpec(bias_col.shape, lambda i: (0, 0))],
        out_specs=[pl.BlockSpec((TOP_K, tm), tok), pl.BlockSpec((TOP_K, tm), tok),
                   pl.BlockSpec((TOP_K, tm), tok),
                   pl.BlockSpec((N_EXPERTS, LANES), lambda i: (0, 0))],
        out_shape=[jax.ShapeDtypeStruct((TOP_K, n_tok), I32),
                   jax.ShapeDtypeStruct((TOP_K, n_tok), F32),
                   jax.ShapeDtypeStruct((TOP_K, n_tok), I32),
                   jax.ShapeDtypeStruct((N_EXPERTS, LANES), F32)],
        compiler_params=_params("arbitrary"),
        name="router",
    )(x, w_router_t, bias_col)


def _tile_view(x):
    r, d = x.shape
    return (x.reshape(r // SUBLANES, SUBLANES, d // LANES, LANES).transpose(0, 2, 1, 3)
            .reshape(r // SUBLANES, d // LANES, SUBLANES, 1, LANES))


def _row_view(v):
    g, c, s, _, l = v.shape
    return v.reshape(g, c, s, l).transpose(0, 2, 1, 3).reshape(g * s, c * l)


def _split_rows(rows):
    return rows >> SUBLANE_SHIFT, rows & (SUBLANES - 1)


def _dispatch_kernel(pad_ref, dgroup_ref, dsub_ref, x_ref, xs_ref, zero_ref, sem):
    step = pl.program_id(0)
    groups = x_ref.shape[0]

    @pl.when(step == 0)
    def _():
        zero_ref[...] = jnp.zeros_like(zero_ref)

        def fill(e):
            return pltpu.make_async_copy(zero_ref, xs_ref.at[pl.ds(pad_ref[e] >> SUBLANE_SHIFT, ZERO_GROUPS)], sem)

        def start(e, c):
            fill(e).start()
            return c

        def wait(e, c):
            fill(e).wait()
            return c

        lax.fori_loop(0, N_EXPERTS, start, 0)
        lax.fori_loop(0, N_EXPERTS, wait, 0)

    def copies(g):
        return [pltpu.make_async_copy(
                    x_ref.at[g, :, s],
                    xs_ref.at[dgroup_ref[k, g * SUBLANES + s], :, dsub_ref[k, g * SUBLANES + s]], sem)
                for s in range(SUBLANES) for k in range(TOP_K)]

    def start_rows(g, c):
        for n, cp in enumerate(copies(g)):
            cp.start(priority=n % DMA_QUEUES)
        return c

    def wait_rows(g, c):
        for cp in copies(g):
            cp.wait()
        return c

    lax.fori_loop(0, groups, start_rows, 0, unroll=True)
    lax.fori_loop(0, groups, wait_rows, 0)


def _dispatch(pad_start, dest_group, dest_sub, x, n_rows):
    n_tok, w = x.shape
    tm = TM_DISPATCH
    chunks = w // LANES
    block = (tm // SUBLANES, chunks, SUBLANES, 1, LANES)
    total_groups = n_rows // SUBLANES + ZERO_GROUPS
    out = pl.pallas_call(
        _dispatch_kernel,
        grid_spec=pltpu.PrefetchScalarGridSpec(
            num_scalar_prefetch=1,
            grid=(n_tok // tm,),
            in_specs=[pl.BlockSpec((TOP_K, tm), lambda i, pad: (0, i), memory_space=pltpu.SMEM),
                      pl.BlockSpec((TOP_K, tm), lambda i, pad: (0, i), memory_space=pltpu.SMEM),
                      pl.BlockSpec(block, lambda i, pad: (i, 0, 0, 0, 0))],
            out_specs=pl.BlockSpec(memory_space=pl.ANY),
            scratch_shapes=[pltpu.VMEM((ZERO_GROUPS, chunks, SUBLANES, 1, LANES), x.dtype),
                            pltpu.SemaphoreType.DMA(())]),
        out_shape=jax.ShapeDtypeStruct((total_groups, chunks, SUBLANES, 1, LANES), x.dtype),
        compiler_params=_params("arbitrary"),
        name="dispatch",
    )(pad_start, dest_group, dest_sub, _tile_view(x))
    return _row_view(out)


def _expert_kernel(first_ref, nblk_ref, nused_ref, next_ref, wslot_ref, wg_hbm, wu_hbm, wd_hbm, xs_hbm, ys_hbm,
                   xbuf, ybuf, wg_f, wu_f, wd_f, wg_s, wu_s, wd_s, xsem, ysem, wsem):
    e = pl.program_id(0)
    n_used = nused_ref[0]

    def x_copy(blk, slot):
        rows = pl.ds(pl.multiple_of(blk * EXPERT_ROWS, EXPERT_ROWS), EXPERT_ROWS)
        return pltpu.make_async_copy(xs_hbm.at[rows, :], xbuf.at[slot], xsem.at[slot])

    def y_copy(blk, slot):
        rows = pl.ds(pl.multiple_of(blk * EXPERT_ROWS, EXPERT_ROWS), EXPERT_ROWS)
        return pltpu.make_async_copy(ybuf.at[slot], ys_hbm.at[rows, :], ysem.at[slot])

    def w_copies(expert, slot):
        return [pltpu.make_async_copy(src.at[expert], dst.at[slot], wsem.at[slot])
                for src, dst in ((wg_hbm, wg_f), (wu_hbm, wu_f), (wd_hbm, wd_f))]

    @pl.when(e == 0)
    def _():
        x_copy(0, 0).start()

    @pl.when(nblk_ref[e] > 0)
    def _():
        wslot = wslot_ref[e]

        @pl.when(first_ref[e] == 0)
        def _():
            for cp in w_copies(e, wslot):
                cp.start()

        for cp in w_copies(e, wslot):
            cp.wait()
        wg_s[...] = wg_f[wslot].astype(BF16)
        wu_s[...] = wu_f[wslot].astype(BF16)
        wd_s[...] = wd_f[wslot].astype(BF16)
        e_next = next_ref[e]

        @pl.when(e_next < N_EXPERTS)
        def _():
            for cp in w_copies(e_next, 1 - wslot):
                cp.start(priority=DMA_QUEUES - 1)

        def block(i, carry):
            blk = first_ref[e] + i
            slot = blk & 1
            x_copy(blk, slot).wait()

            @pl.when(blk + 1 < n_used)
            def _():
                x_copy(blk + 1, 1 - slot).start()

            @pl.when(blk >= 2)
            def _():
                y_copy(blk - 2, slot).wait()

            lo, hi = _unpack_halves(xbuf[slot])
            half = lo.shape[1]
            gate = _dot(lo, wg_s[:half, :]) + _dot(hi, wg_s[half:, :])
            up = _dot(lo, wu_s[:half, :]) + _dot(hi, wu_s[half:, :])
            hmid = (_silu(gate) * up).astype(BF16)
            ybuf[slot] = _dot(hmid, wd_s[...])
            y_copy(blk, slot).start()
            return carry

        lax.fori_loop(0, nblk_ref[e], block, 0)

    @pl.when(e == pl.num_programs(0) - 1)
    def _():
        @pl.when(n_used >= 2)
        def _():
            y_copy(n_used - 2, (n_used - 2) & 1).wait()

        y_copy(n_used - 1, (n_used - 1) & 1).wait()


def _experts(first_block, n_blocks_of, n_used, next_expert, wslot, xs, w_gate, w_up, w_down, total_blocks):
    n_exp, d, ff = w_gate.shape
    hbm = pl.BlockSpec(memory_space=pl.ANY)
    return pl.pallas_call(
        _expert_kernel,
        grid_spec=pltpu.PrefetchScalarGridSpec(
            num_scalar_prefetch=5,
            grid=(n_exp,),
            in_specs=[hbm, hbm, hbm, hbm],
            out_specs=hbm,
            scratch_shapes=[pltpu.VMEM((2, EXPERT_ROWS, d // 2), U32), pltpu.VMEM((2, EXPERT_ROWS, d), F32),
                            pltpu.VMEM((2, d, ff), F32), pltpu.VMEM((2, d, ff), F32),
                            pltpu.VMEM((2, ff, d), F32),
                            pltpu.VMEM((d, ff), BF16), pltpu.VMEM((d, ff), BF16),
                            pltpu.VMEM((ff, d), BF16),
                            pltpu.SemaphoreType.DMA((2,)), pltpu.SemaphoreType.DMA((2,)),
                            pltpu.SemaphoreType.DMA((2,))]),
        out_shape=jax.ShapeDtypeStruct((total_blocks * EXPERT_ROWS, d), F32),
        compiler_params=_params("arbitrary"),
        name="experts",
    )(first_block, n_blocks_of, n_used, next_expert, wslot, w_gate, w_up, w_down, xs)


def _combine_kernel(dgroup_ref, dsub_ref, gate_ref, x_ref, ys_ref, wsg_ref, wsu_ref, wsd_ref, lg_ref, lb_ref,
                    o_ref, buf, sem):
    tm, d = x_ref.shape
    groups, chunks = tm // SUBLANES, d // LANES
    dma_views = [buf.at[k].reshape(groups, chunks, SUBLANES, 1, LANES) for k in range(TOP_K)]
    load_views = [buf.at[k].reshape(groups, chunks, SUBLANES, LANES) for k in range(TOP_K)]

    def copies(g):
        return [pltpu.make_async_copy(ys_ref.at[dgroup_ref[k, g * SUBLANES + s], :, dsub_ref[k, g * SUBLANES + s]],
                                      dma_views[k].at[g, :, s], sem)
                for s in range(SUBLANES) for k in range(TOP_K)]

    def start_rows(g, c):
        for n, cp in enumerate(copies(g)):
            cp.start(priority=n % DMA_QUEUES)
        return c

    def wait_rows(g, c):
        for cp in copies(g):
            cp.wait()
        return c

    lax.fori_loop(0, groups, start_rows, 0, unroll=True)

    x = x_ref[...]
    xb = x.astype(BF16)
    hmid = (_silu(_dot(xb, wsg_ref[...])) * _dot(xb, wsu_ref[...])).astype(BF16)
    acc = ALPHA * x + _dot(hmid, wsd_ref[...])

    lax.fori_loop(0, groups, wait_rows, 0)
    gates = gate_ref[...]

    def rows_of(k):
        return jnp.concatenate([load_views[k][:, c].reshape(tm, LANES) for c in range(chunks)], axis=1)

    routed = rows_of(0) * gates[:, 0:1]
    for k in range(1, TOP_K):
        routed += rows_of(k) * gates[:, k:k + 1]
    o_ref[...] = _layer_norm(acc + routed, lg_ref[...], lb_ref[...])


def _combine(dest_group, dest_sub, gates_t, x, ys, ws_gate, ws_up, ws_down, ln_g, ln_b):
    n_tok, d = x.shape
    tm = TM_COMBINE
    row = lambda i: (i, 0)
    const = lambda i: (0, 0)
    return pl.pallas_call(
        _combine_kernel,
        grid=(n_tok // tm,),
        in_specs=[pl.BlockSpec((TOP_K, tm), lambda i: (0, i), memory_space=pltpu.SMEM),
                  pl.BlockSpec((TOP_K, tm), lambda i: (0, i), memory_space=pltpu.SMEM),
                  pl.BlockSpec((tm, TOP_K), row),
                  pl.BlockSpec((tm, d), row),
                  pl.BlockSpec(memory_space=pl.ANY),
                  pl.BlockSpec(ws_gate.shape, const),
                  pl.BlockSpec(ws_up.shape, const),
                  pl.BlockSpec(ws_down.shape, const),
                  pl.BlockSpec(ln_g.shape, const),
                  pl.BlockSpec(ln_b.shape, const)],
        out_specs=pl.BlockSpec((tm, d), row),
        out_shape=jax.ShapeDtypeStruct((n_tok, d), F32),
        scratch_shapes=[pltpu.VMEM((TOP_K, tm * (d // LANES), LANES), F32), pltpu.SemaphoreType.DMA(())],
        compiler_params=_params("arbitrary"),
        name="combine_shared_ln3",
    )(dest_group, dest_sub, gates_t, x, _tile_view(ys), ws_gate, ws_up, ws_down, ln_g, ln_b)


def _layer(x, mem, w_in, conv_w, g_attn_out, g_conv_out, w_out, ln1_g, ln1_b, w_q_mem, w_kv_mem,
           w_o_mem, ln2_g, ln2_b, w_router, router_bias, w_gate, w_up, w_down, ws_gate, ws_up,
           ws_down, ln3_g, ln3_b):
    batch, seq, d = x.shape
    mem_len = mem.shape[1]
    n_tok = batch * seq
    xf = x.reshape(n_tok, d)
    row = lambda v: v.reshape(1, -1)

    proj = _matmul(xf, w_in.astype(BF16), F32, TM_PROJ, TN_PROJ)
    attn = _dilated_attention(proj, batch, seq)
    x1 = _mix_out(attn, proj, xf, w_out.astype(BF16), conv_w, row(g_attn_out), row(g_conv_out),
                  row(ln1_g), row(ln1_b), seq)

    kv = _matmul(mem.reshape(batch * mem_len, d), w_kv_mem.astype(BF16), BF16,
                 mem_len, 1024)
    x2, x2_packed = _xattn(x1, w_q_mem.astype(BF16), kv, w_o_mem.astype(BF16), row(ln2_g), row(ln2_b),
                           seq, mem_len)

    idx, gates, rank, cnt = _router(x2, w_router.T, router_bias.reshape(N_EXPERTS, 1))
    counts = cnt[:, 0].astype(I32)
    padded = (counts + EXPERT_ROWS - 1) // EXPERT_ROWS * EXPERT_ROWS
    seg_end = jnp.cumsum(padded)
    seg_start = seg_end - padded
    experts = jnp.arange(N_EXPERTS, dtype=I32)
    dest = rank + jnp.sum(jnp.where(idx[..., None] == experts, seg_start, 0), axis=-1)
    total_blocks = -(-(n_tok * TOP_K) // EXPERT_ROWS) + N_EXPERTS
    n_used = (seg_end[-1:] // EXPERT_ROWS).astype(I32)

    dest_group, dest_sub = _split_rows(dest)
    xs = _dispatch(seg_start + counts, dest_group, dest_sub, x2_packed, total_blocks * EXPERT_ROWS)
    nonempty = counts > 0
    first_from = lax.cummin(jnp.where(nonempty, experts, N_EXPERTS), reverse=True)
    next_expert = jnp.concatenate([first_from[1:], jnp.full((1,), N_EXPERTS, I32)])
    wslot = (jnp.cumsum(nonempty.astype(I32)) - 1) & 1
    ys = _experts(seg_start // EXPERT_ROWS, padded // EXPERT_ROWS, n_used, next_expert, wslot, xs,
                  w_gate, w_up, w_down, total_blocks)
    out = _combine(dest_group, dest_sub, gates.T, x2, ys, ws_gate.astype(BF16), ws_up.astype(BF16),
                   ws_down.astype(BF16), row(ln3_g), row(ln3_b))
    return out.reshape(batch, seq, d)


def kernel(x, mem, w_in, conv_w, g_attn_out, g_conv_out, w_out, ln1_g, ln1_b, w_q_mem, w_kv_mem, w_o_mem, ln2_g, ln2_b, w_router, router_bias, w_gate, w_up, w_down, ws_gate, ws_up, ws_down, ln3_g, ln3_b):
    for l in range(DEPTH):
        x = _layer(x, mem, w_in[l], conv_w[l], g_attn_out[l], g_conv_out[l], w_out[l], ln1_g[l],
                   ln1_b[l], w_q_mem[l], w_kv_mem[l], w_o_mem[l], ln2_g[l], ln2_b[l], w_router[l],
                   router_bias[l], w_gate[l], w_up[l], w_down[l], ws_gate[l], ws_up[l], ws_down[l],
                   ln3_g[l], ln3_b[l])
    return x
```

```python
import functools

import jax
import jax.numpy as jnp
from jax import lax
from jax.experimental import pallas as pl
from jax.experimental.pallas import tpu as pltpu

F32 = jnp.float32
BF16 = jnp.bfloat16
I32 = jnp.int32
U32 = jnp.uint32

HEAD_DIM = 128
N_ATTN_HEADS = 8
ATTN_WIDTH = N_ATTN_HEADS * HEAD_DIM
CONV_WIDTH = 1024
DILATED_PATTERNS = ((128, 1), (512, 4), (2048, 16))
ATTN_BLOCK = 128
ATTN_SPAN = 16 * ATTN_BLOCK
N_MEM_HEADS = 4
N_EXPERTS = 64
TOP_K = 8
N_GROUPS = 8
GROUP_SIZE = N_EXPERTS // N_GROUPS
TOPK_GROUPS = 4
ROUTED_SCALE = 2.5
LN_EPS = 1e-5
RMS_EPS = 1e-6
DEPTH = 1
ALPHA = (2.0 * DEPTH) ** 0.25

LANES = 128
SUBLANES = 8
SUBLANE_SHIFT = SUBLANES.bit_length() - 1
VMEM_LIMIT = 56 * 1024 * 1024
DMA_QUEUES = 2
EXPERT_ROWS = 256
ZERO_GROUPS = EXPERT_ROWS // SUBLANES + 1
TM_PROJ, TN_PROJ = 1024, 512
TM_MIX = 512
TM_XATTN = 512
TM_ROUTER = 512
TM_DISPATCH = 512
TM_COMBINE = 256


def _params(*semantics):
    return pltpu.CompilerParams(dimension_semantics=semantics, vmem_limit_bytes=VMEM_LIMIT)


def _layer_norm(r, g, b):
    mu = jnp.mean(r, axis=-1, keepdims=True)
    c = r - mu
    var = jnp.mean(c * c, axis=-1, keepdims=True)
    return c * lax.rsqrt(var + LN_EPS) * g + b


def _rms_norm(v, g):
    return v * lax.rsqrt(jnp.mean(v * v, axis=-1, keepdims=True) + RMS_EPS) * g


def _dot(a, b):
    return jnp.dot(a, b, preferred_element_type=F32)


def _dot_nt(a, b):
    return lax.dot_general(a, b, (((1,), (1,)), ((), ())), preferred_element_type=F32)


def _silu(v):
    return v / (1.0 + jnp.exp(-v))


def _matmul_kernel(a_ref, w_ref, o_ref, a_bf):
    @pl.when(pl.program_id(1) == 0)
    def _():
        a_bf[...] = a_ref[...].astype(BF16)

    o_ref[...] = _dot(a_bf[...], w_ref[...]).astype(o_ref.dtype)


def _matmul(a, w, out_dtype, tm, tn):
    m, k = a.shape
    n = w.shape[1]
    return pl.pallas_call(
        _matmul_kernel,
        grid=(m // tm, n // tn),
        in_specs=[pl.BlockSpec((tm, k), lambda i, j: (i, 0)),
                  pl.BlockSpec((k, tn), lambda i, j: (0, j))],
        out_specs=pl.BlockSpec((tm, tn), lambda i, j: (i, j)),
        out_shape=jax.ShapeDtypeStruct((m, n), out_dtype),
        scratch_shapes=[pltpu.VMEM((tm, k), BF16)],
        compiler_params=_params("parallel", "arbitrary"),
        name="matmul",
    )(a, w)


def _attn_kernel(q_ref, kp_ref, kc_ref, vp_ref, vc_ref, o_ref, kk, vv, o_acc, l_acc):
    j = pl.program_id(1)
    blk, span = ATTN_BLOCK, ATTN_SPAN
    kk[0:span, :] = kp_ref[...]
    kk[span:, :] = kc_ref[...]
    vv[0:span, :] = vp_ref[...]
    vv[span:, :] = vc_ref[...]

    row = lax.broadcasted_iota(I32, (blk, 2 * blk), 0)
    col = lax.broadcasted_iota(I32, (blk, 2 * blk), 1)
    dist = row + blk - col
    in_window = (dist >= 0) & (dist <= blk)
    own_block = col >= blk
    scale = HEAD_DIM ** -0.5

    order = sorted(range(len(DILATED_PATTERNS)), key=lambda i: -DILATED_PATTERNS[i][1])
    assert DILATED_PATTERNS[order[-1]][1] == 1
    acc_slot = {bi: n for n, bi in enumerate(order[:-1])}
    sub_blocks = span // blk
    for bi in order:
        dil = DILATED_PATTERNS[bi][1]
        shift = dil.bit_length() - 1

        def body(t, carry, bi=bi, dil=dil, shift=shift):
            r = t & (dil - 1)
            n = t >> shift
            base = n * (blk * dil) + r
            if dil == 1:
                base = pl.multiple_of(base, blk)
                q_rows, kv_rows = pl.ds(base, blk), pl.ds(span + base - blk, 2 * blk)
            else:
                q_rows = pl.ds(base, blk, stride=dil)
                kv_rows = pl.ds(span + base - blk * dil, 2 * blk, stride=dil)
            q = q_ref[q_rows, :].astype(BF16)
            k = kk[kv_rows, :].astype(BF16)
            v = vv[kv_rows, :].astype(BF16)
            s = _dot_nt(q, k) * scale
            valid = in_window & (own_block | (j > 0) | (n > 0))
            s = jnp.where(valid, s, -jnp.inf)
            m = jnp.max(s, axis=-1, keepdims=True)
            p = jnp.exp(s - m)
            l = jnp.sum(p, axis=-1, keepdims=True)
            o = _dot((p / l).astype(BF16), v)
            lse = jnp.broadcast_to(m + jnp.log(l), (blk, LANES))
            if dil != 1:
                o_acc[acc_slot[bi], q_rows, :] = o
                l_acc[acc_slot[bi], q_rows, :] = lse
                return carry
            others = [i for i in order if i != bi]
            lses = [lse] + [l_acc[acc_slot[i], q_rows, :] for i in others]
            outs = [o] + [o_acc[acc_slot[i], q_rows, :] for i in others]
            top = jnp.maximum(jnp.maximum(lses[0], lses[1]), lses[2])
            es = [jnp.exp(branch_lse - top) for branch_lse in lses]
            den = es[0] + es[1] + es[2]
            o_ref[q_rows, :] = ((es[0] / den) * outs[0] + (es[1] / den) * outs[1] + (es[2] / den) * outs[2])
            return carry

        lax.fori_loop(0, sub_blocks, body, 0, unroll=True)


def _dilated_attention(proj, batch, seq):
    n_tok, _ = proj.shape
    span = ATTN_SPAN
    spans = seq // span
    heads = N_ATTN_HEADS

    def spec(part, prev):
        def index(b, j, h):
            return (b * spans + (jnp.maximum(j - 1, 0) if prev else j), part * heads + h)
        return pl.BlockSpec((span, HEAD_DIM), index)

    return pl.pallas_call(
        _attn_kernel,
        grid=(batch, spans, heads),
        in_specs=[spec(0, False), spec(1, True), spec(1, False), spec(2, True), spec(2, False)],
        out_specs=pl.BlockSpec((span, HEAD_DIM), lambda b, j, h: (b * spans + j, h)),
        out_shape=jax.ShapeDtypeStruct((n_tok, ATTN_WIDTH), F32),
        scratch_shapes=[pltpu.VMEM((2 * span, HEAD_DIM), F32), pltpu.VMEM((2 * span, HEAD_DIM), F32),
                        pltpu.VMEM((len(DILATED_PATTERNS) - 1, span, HEAD_DIM), F32),
                        pltpu.VMEM((len(DILATED_PATTERNS) - 1, span, HEAD_DIM), F32)],
        compiler_params=_params("parallel", "parallel", "parallel"),
        name="dilated_attn",
    )(proj, proj, proj, proj, proj)


def _mix_out_kernel(a_ref, b_ref, c_ref, h_ref, cp_ref, hp_ref, x_ref,
                    w_ref, cw_ref, ga_ref, gc_ref, lg_ref, lb_ref, o_ref, *, tiles_per_seq):
    i = pl.program_id(0)
    z = c_ref[...] * h_ref[...]
    zp = jnp.where(i % tiles_per_seq == 0, 0.0, cp_ref[...] * hp_ref[...])
    zz = jnp.concatenate([zp, z], axis=0)
    z1 = pltpu.roll(zz, 1, axis=0)[SUBLANES:]
    z2 = pltpu.roll(zz, 2, axis=0)[SUBLANES:]
    conv = b_ref[...] * (cw_ref[0:1, :] * z2 + cw_ref[1:2, :] * z1 + cw_ref[2:3, :] * z)

    mixed = jnp.concatenate([_rms_norm(a_ref[...], ga_ref[...]), _rms_norm(conv, gc_ref[...])], axis=1)
    y = _dot(mixed.astype(BF16), w_ref[...])
    o_ref[...] = _layer_norm(ALPHA * x_ref[...] + y, lg_ref[...], lb_ref[...])


def _mix_out(attn, proj, x, w_out, conv_w, g_attn, g_conv, ln_g, ln_b, seq):
    n_tok, d = x.shape
    tm = TM_MIX
    rows8 = tm // SUBLANES
    row = lambda i: (i, 0)
    const = lambda i: (0, 0)
    prev = lambda part: (lambda i: (jnp.maximum(i * rows8 - 1, 0), part))
    return pl.pallas_call(
        functools.partial(_mix_out_kernel, tiles_per_seq=seq // tm),
        grid=(n_tok // tm,),
        in_specs=[
            pl.BlockSpec((tm, ATTN_WIDTH), row),
            pl.BlockSpec((tm, CONV_WIDTH), lambda i: (i, 3)),
            pl.BlockSpec((tm, CONV_WIDTH), lambda i: (i, 4)),
            pl.BlockSpec((tm, CONV_WIDTH), lambda i: (i, 5)),
            pl.BlockSpec((SUBLANES, CONV_WIDTH), prev(4)),
            pl.BlockSpec((SUBLANES, CONV_WIDTH), prev(5)),
            pl.BlockSpec((tm, d), row),
            pl.BlockSpec(w_out.shape, const, pipeline_mode=pl.Buffered(1)),
            pl.BlockSpec(conv_w.shape, const),
            pl.BlockSpec(g_attn.shape, const),
            pl.BlockSpec(g_conv.shape, const),
            pl.BlockSpec(ln_g.shape, const),
            pl.BlockSpec(ln_b.shape, const)],
        out_specs=pl.BlockSpec((tm, d), row),
        out_shape=jax.ShapeDtypeStruct((n_tok, d), F32),
        compiler_params=_params("parallel"),
        name="mix_out_ln1",
    )(attn, proj, proj, proj, proj, proj, x, w_out, conv_w, g_attn, g_conv, ln_g, ln_b)


def _pack_halves(v):
    c = v.shape[1] // 2
    as_bits = lambda part: lax.bitcast_convert_type(part.astype(BF16).astype(F32), U32)
    return (as_bits(v[:, :c]) >> 16) | (as_bits(v[:, c:]) & jnp.uint32(0xFFFF0000))


def _unpack_halves(p):
    lo = lax.bitcast_convert_type(p << 16, F32).astype(BF16)
    hi = lax.bitcast_convert_type(p & jnp.uint32(0xFFFF0000), F32).astype(BF16)
    return lo, hi


def _xattn_kernel(x_ref, wq_ref, k_ref, v_ref, wo_ref, lg_ref, lb_ref, o_ref, op_ref):
    x = x_ref[...]
    d = x.shape[1]
    hd = d // N_MEM_HEADS
    q = _dot(x.astype(BF16), wq_ref[...]).astype(BF16)
    scale = hd ** -0.5
    outs = []
    for h in range(N_MEM_HEADS):
        sl = slice(h * hd, (h + 1) * hd)
        s = _dot_nt(q[:, sl], k_ref[:, sl]) * scale
        m = jnp.max(s, axis=-1, keepdims=True)
        p = jnp.exp(s - m)
        p = p / jnp.sum(p, axis=-1, keepdims=True)
        outs.append(_dot(p.astype(BF16), v_ref[:, sl]).astype(BF16))
    y = _dot(jnp.concatenate(outs, axis=1), wo_ref[...])
    out = _layer_norm(ALPHA * x + y, lg_ref[...], lb_ref[...])
    o_ref[...] = out
    op_ref[...] = _pack_halves(out)


def _xattn(x, w_q, kv, w_o, ln_g, ln_b, seq, mem_len):
    n_tok, d = x.shape
    tm = TM_XATTN
    row = lambda i: (i, 0)
    const = lambda i: (0, 0)
    tiles_per_seq = seq // tm
    return pl.pallas_call(
        _xattn_kernel,
        grid=(n_tok // tm,),
        in_specs=[pl.BlockSpec((tm, d), row),
                  pl.BlockSpec(w_q.shape, const, pipeline_mode=pl.Buffered(1)),
                  pl.BlockSpec((mem_len, d), lambda i: (i // tiles_per_seq, 0)),
                  pl.BlockSpec((mem_len, d), lambda i: (i // tiles_per_seq, 1)),
                  pl.BlockSpec(w_o.shape, const, pipeline_mode=pl.Buffered(1)),
                  pl.BlockSpec(ln_g.shape, const),
                  pl.BlockSpec(ln_b.shape, const)],
        out_specs=[pl.BlockSpec((tm, d), row), pl.BlockSpec((tm, d // 2), row)],
        out_shape=[jax.ShapeDtypeStruct((n_tok, d), F32), jax.ShapeDtypeStruct((n_tok, d // 2), U32)],
        compiler_params=_params("parallel"),
        name="xattn_ln2",
    )(x, w_q, kv, kv, w_o, ln_g, ln_b)


def _first_argmax(vals, index, sentinel):
    m = jnp.max(vals, axis=0, keepdims=True)
    i = jnp.min(jnp.where(vals == m, index, sentinel), axis=0, keepdims=True)
    return m, i


def _router_kernel(x_ref, wt_ref, bias_ref, idx_ref, gate_ref, rank_ref, cnt_ref):
    step = pl.program_id(0)
    tm = x_ref.shape[0]

    @pl.when(step == 0)
    def _():
        cnt_ref[...] = jnp.zeros_like(cnt_ref)

    x = x_ref[...]
    xh = x.astype(BF16)
    xl = (x - xh.astype(F32)).astype(BF16)
    w = wt_ref[...]
    wh = w.astype(BF16)
    wl = (w - wh.astype(F32)).astype(BF16)
    both = _dot_nt(jnp.concatenate([wh, wl], axis=0), xh)
    logits = both[:N_EXPERTS] + both[N_EXPERTS:] + _dot_nt(wh, xl)
    scores = 1.0 / (1.0 + jnp.exp(-logits))
    choice = scores + bias_ref[...]

    sub = lax.broadcasted_iota(I32, (GROUP_SIZE, tm), 0)
    group_scores = []
    for g in range(N_GROUPS):
        c = choice[g * GROUP_SIZE:(g + 1) * GROUP_SIZE, :]
        m1, i1 = _first_argmax(c, sub, GROUP_SIZE)
        m2 = jnp.max(jnp.where(sub == i1, -jnp.inf, c), axis=0, keepdims=True)
        group_scores.append(m1 + m2)
    gs = jnp.concatenate(group_scores, axis=0)
    gidx = lax.broadcasted_iota(I32, (N_GROUPS, tm), 0)
    gsel = jnp.zeros((N_GROUPS, tm), F32)
    for _ in range(TOPK_GROUPS):
        _, gi = _first_argmax(gs, gidx, N_GROUPS)
        hit = gidx == gi
        gsel = jnp.where(hit, 1.0, gsel)
        gs = jnp.where(hit, -jnp.inf, gs)
    masked = jnp.concatenate(
        [jnp.where(gsel[g:g + 1, :] > 0.0, choice[g * GROUP_SIZE:(g + 1) * GROUP_SIZE, :], -jnp.inf)
         for g in range(N_GROUPS)], axis=0)

    eidx = lax.broadcasted_iota(I32, (N_EXPERTS, tm), 0)
    hits, idxs, ws = [], [], []
    for _ in range(TOP_K):
        _, ei = _first_argmax(masked, eidx, N_EXPERTS)
        hit = eidx == ei
        hits.append(hit)
        idxs.append(ei)
        ws.append(jnp.sum(jnp.where(hit, scores, 0.0), axis=0, keepdims=True))
        masked = jnp.where(hit, -jnp.inf, masked)
    wsum = ws[0]
    for w in ws[1:]:
        wsum = wsum + w
    gate_ref[...] = jnp.concatenate([w / wsum * ROUTED_SCALE for w in ws], axis=0)
    idx_ref[...] = jnp.concatenate(idxs, axis=0)

    sel = hits[0]
    for hit in hits[1:]:
        sel = sel | hit
    self32 = jnp.where(sel, 1.0, 0.0)
    t_row = lax.broadcasted_iota(I32, (tm, tm), 0)
    t_col = lax.broadcasted_iota(I32, (tm, tm), 1)
    tri = jnp.where(t_row < t_col, 1.0, 0.0).astype(BF16)
    before = _dot(self32.astype(BF16), tri) + cnt_ref[:, 0:1]
    rank_ref[...] = jnp.concatenate(
        [jnp.sum(jnp.where(hit, before, 0.0), axis=0, keepdims=True) for hit in hits],
        axis=0).astype(I32)
    cnt_ref[...] += jnp.sum(self32, axis=1, keepdims=True)


def _router(x, w_router_t, bias_col):
    n_tok, d = x.shape
    tm = TM_ROUTER
    tok = lambda i: (0, i)
    return pl.pallas_call(
        _router_kernel,
        grid=(n_tok // tm,),
        in_specs=[pl.BlockSpec((tm, d), lambda i: (i, 0)),
                  pl.BlockSpec(w_router_t.shape, lambda i: (0, 0)),
                  pl.BlockSpec(bias_col.shape, lambda i: (0, 0))],
        out_specs=[pl.BlockSpec((TOP_K, tm), tok), pl.BlockSpec((TOP_K, tm), tok),
                   pl.BlockSpec((TOP_K, tm), tok),
                   pl.BlockSpec((N_EXPERTS, LANES), lambda i: (0, 0))],
        out_shape=[jax.ShapeDtypeStruct((TOP_K, n_tok), I32),
                   jax.ShapeDtypeStruct((TOP_K, n_tok), F32),
                   jax.ShapeDtypeStruct((TOP_K, n_tok), I32),
                   jax.ShapeDtypeStruct((N_EXPERTS, LANES), F32)],
        compiler_params=_params("arbitrary"),
        name="router",
    )(x, w_router_t, bias_col)


def _tile_view(x):
    r, d = x.shape
    return (x.reshape(r // SUBLANES, SUBLANES, d // LANES, LANES).transpose(0, 2, 1, 3)
            .reshape(r // SUBLANES, d // LANES, SUBLANES, 1, LANES))


def _row_view(v):
    g, c, s, _, l = v.shape
    return v.reshape(g, c, s, l).transpose(0, 2, 1, 3).reshape(g * s, c * l)


def _tile_row(view_ref, row):
    return view_ref.at[row >> SUBLANE_SHIFT, :, row & (SUBLANES - 1)]


def _dispatch_kernel(pad_ref, dest_ref, x_ref, xs_ref, zero_ref, sem):
    step = pl.program_id(0)
    groups = x_ref.shape[0]

    @pl.when(step == 0)
    def _():
        zero_ref[...] = jnp.zeros_like(zero_ref)

        def fill(e):
            return pltpu.make_async_copy(zero_ref, xs_ref.at[pl.ds(pad_ref[e] >> SUBLANE_SHIFT, ZERO_GROUPS)], sem)

        def start(e, c):
            fill(e).start()
            return c

        def wait(e, c):
            fill(e).wait()
            return c

        lax.fori_loop(0, N_EXPERTS, start, 0)
        lax.fori_loop(0, N_EXPERTS, wait, 0)

    def copies(g):
        return [pltpu.make_async_copy(x_ref.at[g, :, s], _tile_row(xs_ref, dest_ref[k, g * SUBLANES + s]), sem)
                for s in range(SUBLANES) for k in range(TOP_K)]

    def start_rows(g, c):
        for n, cp in enumerate(copies(g)):
            cp.start(priority=n % DMA_QUEUES)
        return c

    def wait_rows(g, c):
        for cp in copies(g):
            cp.wait()
        return c

    lax.fori_loop(0, groups, start_rows, 0, unroll=True)
    lax.fori_loop(0, groups, wait_rows, 0)


def _dispatch(pad_start, dest, x, n_rows):
    n_tok, w = x.shape
    tm = TM_DISPATCH
    chunks = w // LANES
    block = (tm // SUBLANES, chunks, SUBLANES, 1, LANES)
    total_groups = n_rows // SUBLANES + ZERO_GROUPS
    out = pl.pallas_call(
        _dispatch_kernel,
        grid_spec=pltpu.PrefetchScalarGridSpec(
            num_scalar_prefetch=1,
            grid=(n_tok // tm,),
            in_specs=[pl.BlockSpec((TOP_K, tm), lambda i, pad: (0, i), memory_space=pltpu.SMEM),
                      pl.BlockSpec(block, lambda i, pad: (i, 0, 0, 0, 0))],
            out_specs=pl.BlockSpec(memory_space=pl.ANY),
            scratch_shapes=[pltpu.VMEM((ZERO_GROUPS, chunks, SUBLANES, 1, LANES), x.dtype),
                            pltpu.SemaphoreType.DMA(())]),
        out_shape=jax.ShapeDtypeStruct((total_groups, chunks, SUBLANES, 1, LANES), x.dtype),
        compiler_params=_params("arbitrary"),
        name="dispatch",
    )(pad_start, dest, _tile_view(x))
    return _row_view(out)


def _expert_kernel(first_ref, nblk_ref, nused_ref, next_ref, wslot_ref, wg_hbm, wu_hbm, wd_hbm, xs_hbm, ys_hbm,
                   xbuf, ybuf, wg_f, wu_f, wd_f, wg_s, wu_s, wd_s, xsem, ysem, wsem):
    e = pl.program_id(0)
    n_used = nused_ref[0]

    def x_copy(blk, slot):
        rows = pl.ds(pl.multiple_of(blk * EXPERT_ROWS, EXPERT_ROWS), EXPERT_ROWS)
        return pltpu.make_async_copy(xs_hbm.at[rows, :], xbuf.at[slot], xsem.at[slot])

    def y_copy(blk, slot):
        rows = pl.ds(pl.multiple_of(blk * EXPERT_ROWS, EXPERT_ROWS), EXPERT_ROWS)
        return pltpu.make_async_copy(ybuf.at[slot], ys_hbm.at[rows, :], ysem.at[slot])

    def w_copies(expert, slot):
        return [pltpu.make_async_copy(src.at[expert], dst.at[slot], wsem.at[slot])
                for src, dst in ((wg_hbm, wg_f), (wu_hbm, wu_f), (wd_hbm, wd_f))]

    @pl.when(e == 0)
    def _():
        x_copy(0, 0).start()

    @pl.when(nblk_ref[e] > 0)
    def _():
        wslot = wslot_ref[e]

        @pl.when(first_ref[e] == 0)
        def _():
            for cp in w_copies(e, wslot):
                cp.start()

        for cp in w_copies(e, wslot):
            cp.wait()
        wg_s[...] = wg_f[wslot].astype(BF16)
        wu_s[...] = wu_f[wslot].astype(BF16)
        wd_s[...] = wd_f[wslot].astype(BF16)
        e_next = next_ref[e]

        @pl.when(e_next < N_EXPERTS)
        def _():
            for cp in w_copies(e_next, 1 - wslot):
                cp.start(priority=DMA_QUEUES - 1)

        def block(i, carry):
            blk = first_ref[e] + i
            slot = blk & 1
            x_copy(blk, slot).wait()

            @pl.when(blk + 1 < n_used)
            def _():
                x_copy(blk + 1, 1 - slot).start()

            @pl.when(blk >= 2)
            def _():
                y_copy(blk - 2, slot).wait()

            lo, hi = _unpack_halves(xbuf[slot])
            half = lo.shape[1]
            gate = _dot(lo, wg_s[:half, :]) + _dot(hi, wg_s[half:, :])
            up = _dot(lo, wu_s[:half, :]) + _dot(hi, wu_s[half:, :])
            hmid = (_silu(gate) * up).astype(BF16)
            ybuf[slot] = _pack_halves(_dot(hmid, wd_s[...]))
            y_copy(blk, slot).start()
            return carry

        lax.fori_loop(0, nblk_ref[e], block, 0)

    @pl.when(e == pl.num_programs(0) - 1)
    def _():
        @pl.when(n_used >= 2)
        def _():
            y_copy(n_used - 2, (n_used - 2) & 1).wait()

        y_copy(n_used - 1, (n_used - 1) & 1).wait()


def _experts(first_block, n_blocks_of, n_used, next_expert, wslot, xs, w_gate, w_up, w_down, total_blocks):
    n_exp, d, ff = w_gate.shape
    hbm = pl.BlockSpec(memory_space=pl.ANY)
    return pl.pallas_call(
        _expert_kernel,
        grid_spec=pltpu.PrefetchScalarGridSpec(
            num_scalar_prefetch=5,
            grid=(n_exp,),
            in_specs=[hbm, hbm, hbm, hbm],
            out_specs=hbm,
            scratch_shapes=[pltpu.VMEM((2, EXPERT_ROWS, d // 2), U32), pltpu.VMEM((2, EXPERT_ROWS, d // 2), U32),
                            pltpu.VMEM((2, d, ff), F32), pltpu.VMEM((2, d, ff), F32),
                            pltpu.VMEM((2, ff, d), F32),
                            pltpu.VMEM((d, ff), BF16), pltpu.VMEM((d, ff), BF16),
                            pltpu.VMEM((ff, d), BF16),
                            pltpu.SemaphoreType.DMA((2,)), pltpu.SemaphoreType.DMA((2,)),
                            pltpu.SemaphoreType.DMA((2,))]),
        out_shape=jax.ShapeDtypeStruct((total_blocks * EXPERT_ROWS, d // 2), U32),
        compiler_params=_params("arbitrary"),
        name="experts",
    )(first_block, n_blocks_of, n_used, next_expert, wslot, w_gate, w_up, w_down, xs)


def _combine_kernel(dest_ref, gate_ref, x_ref, ys_ref, wsg_ref, wsu_ref, wsd_ref, lg_ref, lb_ref,
                    o_ref, buf, sem):
    tm, d = x_ref.shape
    groups, chunks = tm // SUBLANES, d // 2 // LANES
    dma_views = [buf.at[k].reshape(groups, chunks, SUBLANES, 1, LANES) for k in range(TOP_K)]
    load_views = [buf.at[k].reshape(groups, chunks, SUBLANES, LANES) for k in range(TOP_K)]

    def copies(g):
        return [pltpu.make_async_copy(_tile_row(ys_ref, dest_ref[k, g * SUBLANES + s]),
                                      dma_views[k].at[g, :, s], sem)
                for s in range(SUBLANES) for k in range(TOP_K)]

    def start_rows(g, c):
        for n, cp in enumerate(copies(g)):
            cp.start(priority=n % DMA_QUEUES)
        return c

    def wait_rows(g, c):
        for cp in copies(g):
            cp.wait()
        return c

    lax.fori_loop(0, groups, start_rows, 0, unroll=True)

    x = x_ref[...]
    xb = x.astype(BF16)
    hmid = (_silu(_dot(xb, wsg_ref[...])) * _dot(xb, wsu_ref[...])).astype(BF16)
    acc = ALPHA * x + _dot(hmid, wsd_ref[...])

    lax.fori_loop(0, groups, wait_rows, 0)
    gates = gate_ref[...]

    def rows_of(k):
        packed = jnp.concatenate([load_views[k][:, c].reshape(tm, LANES) for c in range(chunks)], axis=1)
        lo, hi = _unpack_halves(packed)
        return jnp.concatenate([lo.astype(F32), hi.astype(F32)], axis=1)

    routed = rows_of(0) * gates[:, 0:1]
    for k in range(1, TOP_K):
        routed += rows_of(k) * gates[:, k:k + 1]
    o_ref[...] = _layer_norm(acc + routed, lg_ref[...], lb_ref[...])


def _combine(dest, gates_t, x, ys, ws_gate, ws_up, ws_down, ln_g, ln_b):
    n_tok, d = x.shape
    tm = TM_COMBINE
    row = lambda i: (i, 0)
    const = lambda i: (0, 0)
    return pl.pallas_call(
        _combine_kernel,
        grid=(n_tok // tm,),
        in_specs=[pl.BlockSpec((TOP_K, tm), lambda i: (0, i), memory_space=pltpu.SMEM),
                  pl.BlockSpec((tm, TOP_K), row),
                  pl.BlockSpec((tm, d), row),
                  pl.BlockSpec(memory_space=pl.ANY),
                  pl.BlockSpec(ws_gate.shape, const),
                  pl.BlockSpec(ws_up.shape, const),
                  pl.BlockSpec(ws_down.shape, const),
                  pl.BlockSpec(ln_g.shape, const),
                  pl.BlockSpec(ln_b.shape, const)],
        out_specs=pl.BlockSpec((tm, d), row),
        out_shape=jax.ShapeDtypeStruct((n_tok, d), F32),
        scratch_shapes=[pltpu.VMEM((TOP_K, tm * (d // 2 // LANES), LANES), U32), pltpu.SemaphoreType.DMA(())],
        compiler_params=_params("arbitrary"),
        name="combine_shared_ln3",
    )(dest, gates_t, x, _tile_view(ys), ws_gate, ws_up, ws_down, ln_g, ln_b)


def _layer(x, mem, w_in, conv_w, g_attn_out, g_conv_out, w_out, ln1_g, ln1_b, w_q_mem, w_kv_mem,
           w_o_mem, ln2_g, ln2_b, w_router, router_bias, w_gate, w_up, w_down, ws_gate, ws_up,
           ws_down, ln3_g, ln3_b):
    batch, seq, d = x.shape
    mem_len = mem.shape[1]
    n_tok = batch * seq
    xf = x.reshape(n_tok, d)
    row = lambda v: v.reshape(1, -1)

    proj = _matmul(xf, w_in.astype(BF16), F32, TM_PROJ, TN_PROJ)
    attn = _dilated_attention(proj, batch, seq)
    x1 = _mix_out(attn, proj, xf, w_out.astype(BF16), conv_w, row(g_attn_out), row(g_conv_out),
                  row(ln1_g), row(ln1_b), seq)

    kv = _matmul(mem.reshape(batch * mem_len, d), w_kv_mem.astype(BF16), BF16,
                 mem_len, 1024)
    x2, x2_packed = _xattn(x1, w_q_mem.astype(BF16), kv, w_o_mem.astype(BF16), row(ln2_g), row(ln2_b),
                           seq, mem_len)

    idx, gates, rank, cnt = _router(x2, w_router.T, router_bias.reshape(N_EXPERTS, 1))
    counts = cnt[:, 0].astype(I32)
    padded = (counts + EXPERT_ROWS - 1) // EXPERT_ROWS * EXPERT_ROWS
    seg_end = jnp.cumsum(padded)
    seg_start = seg_end - padded
    experts = jnp.arange(N_EXPERTS, dtype=I32)
    dest = rank + jnp.sum(jnp.where(idx[..., None] == experts, seg_start, 0), axis=-1)
    total_blocks = -(-(n_tok * TOP_K) // EXPERT_ROWS) + N_EXPERTS
    n_used = (seg_end[-1:] // EXPERT_ROWS).astype(I32)

    xs = _dispatch(seg_start + counts, dest, x2_packed, total_blocks * EXPERT_ROWS)
    nonempty = counts > 0
    first_from = lax.cummin(jnp.where(nonempty, experts, N_EXPERTS), reverse=True)
    next_expert = jnp.concatenate([first_from[1:], jnp.full((1,), N_EXPERTS, I32)])
    wslot = (jnp.cumsum(nonempty.astype(I32)) - 1) & 1
    ys = _experts(seg_start // EXPERT_ROWS, padded // EXPERT_ROWS, n_used, next_expert, wslot, xs,
                  w_gate, w_up, w_down, total_blocks)
    out = _combine(dest, gates.T, x2, ys, ws_gate.astype(BF16), ws_up.astype(BF16),
                   ws_down.astype(BF16), row(ln3_g), row(ln3_b))
    return out.reshape(batch, seq, d)


def kernel(x, mem, w_in, conv_w, g_attn_out, g_conv_out, w_out, ln1_g, ln1_b, w_q_mem, w_kv_mem, w_o_mem, ln2_g, ln2_b, w_router, router_bias, w_gate, w_up, w_down, ws_gate, ws_up, ws_down, ln3_g, ln3_b):
    for l in range(DEPTH):
        x = _layer(x, mem, w_in[l], conv_w[l], g_attn_out[l], g_conv_out[l], w_out[l], ln1_g[l],
                   ln1_b[l], w_q_mem[l], w_kv_mem[l], w_o_mem[l], ln2_g[l], ln2_b[l], w_router[l],
                   router_bias[l], w_gate[l], w_up[l], w_down[l], ws_gate[l], ws_up[l], ws_down[l],
                   ln3_g[l], ln3_b[l])
    return x
```
